```python
import math
import jax
import jax.numpy as jnp
from jax import lax
import numpy as np

D_MODEL = 1024
BATCH = 2
SEQ = 8192
DEPTH = 1
DEC_BATCH = 128
DEC_SEQ = 4
PAST_LEN = 8192
PAGE_SIZE = 128

NSA_HEADS = 8
NSA_KV_HEADS = 2
NSA_GROUP = NSA_HEADS // NSA_KV_HEADS
HEAD_DIM = 64
NSA_WIDTH = NSA_HEADS * HEAD_DIM
NSA_KV_WIDTH = NSA_KV_HEADS * HEAD_DIM
CMP_BLOCK = 32
CMP_STRIDE = 16
CMP_HIDDEN = 64
SEL_BLOCK = 64
N_SEL = 16
WINDOW = 512
Q_BLOCK = 128

GDN_HEADS = 8
GDN_DK = 64
GDN_DV = 64
GDN_QK_WIDTH = GDN_HEADS * GDN_DK
GDN_WIDTH = GDN_HEADS * GDN_DV
CONV_W = 4
CONV_DIM = 2 * GDN_QK_WIDTH + GDN_WIDTH
GDN_CHUNK = 64

IN_SIZES = (NSA_WIDTH, 6 * NSA_KV_WIDTH, 3 * NSA_HEADS, NSA_WIDTH, CONV_DIM, GDN_HEADS, GDN_HEADS, GDN_WIDTH, 2 * D_MODEL)
IN_WIDTH = NSA_WIDTH + 6 * NSA_KV_WIDTH + 3 * NSA_HEADS + NSA_WIDTH + CONV_DIM + 2 * GDN_HEADS + GDN_WIDTH + 2 * D_MODEL
EPS = 1e-6

kernel_name = 'hybrid_nsa_gdn_step'


def _rmsnorm(x, g):
    xf = x.astype(jnp.float32)
    y = xf * lax.rsqrt(jnp.mean(xf * xf, axis=-1, keepdims=True) + EPS)
    return (y * g.astype(jnp.float32)).astype(x.dtype)


def _l2norm(x):
    xf = x.astype(jnp.float32)
    return xf * lax.rsqrt(jnp.sum(xf * xf, axis=-1, keepdims=True) + EPS)


def _alibi_slopes():
    h = np.arange(1, NSA_HEADS + 1, dtype=np.float32)
    s = np.power(np.float32(2.0), -np.float32(8.0) * h / np.float32(NSA_HEADS))
    return jnp.asarray(s, dtype=jnp.float32).reshape(NSA_KV_HEADS, NSA_GROUP)


def _masked_softmax(s, mask):
    s = jnp.where(mask, s, -jnp.inf)
    m = jnp.max(s, axis=-1, keepdims=True)
    m = jnp.where(jnp.isfinite(m), m, 0.0)
    p = jnp.exp(s - m)
    return p / jnp.maximum(jnp.sum(p, axis=-1, keepdims=True), 1e-30)


def _to_groups(a):
    B, T = a.shape[:2]
    return a.reshape(B, T, NSA_KV_HEADS, NSA_GROUP, -1).transpose(0, 2, 3, 1, 4)


def _compress(rows, pe, w1, w2):
    B, L, G, dk = rows.shape
    r_cnt = CMP_BLOCK // CMP_STRIDE
    n16 = L // CMP_STRIDE
    n_cmp = n16 - r_cnt + 1
    ch = rows[:, :n16 * CMP_STRIDE].reshape(B, n16, CMP_STRIDE, G, dk)
    w1r = w1.reshape(r_cnt, CMP_STRIDE, dk, CMP_HIDDEN)
    pre = jnp.einsum('sd,sde->e', pe, w1)
    for r in range(r_cnt):
        pre = pre + jnp.einsum('bnsgd,sde->bnge', ch[:, r:r + n_cmp], w1r[r])
    return jnp.einsum('bnge,ed->bngd', jax.nn.silu(pre), w2)


def _cmp_to_sel(n_cmp, n_blk):
    i = np.arange(n_cmp)[:, None] * CMP_STRIDE
    j = np.arange(n_blk)[None, :] * SEL_BLOCK
    ov = np.minimum(i + CMP_BLOCK, j + SEL_BLOCK) - np.maximum(i, j)
    return jnp.asarray(np.clip(ov, 0, None).astype(np.float32) / np.float32(CMP_BLOCK))


def _cmp_attend(q, kc, vc, t, slopes):
    n_cmp = kc.shape[1]
    end = jnp.arange(n_cmp) * CMP_STRIDE + (CMP_BLOCK - 1)
    dist = t[:, None] - end[None, :]
    s = jnp.einsum('bghqd,bcgd->bghqc', q, kc).astype(jnp.float32)
    s = s - slopes[:, :, None, None] * dist.astype(jnp.float32)
    p = _masked_softmax(s, dist >= 0)
    return jnp.einsum('bghqc,bcgd->bghqd', p, vc.astype(jnp.float32)), p


def _select_blocks(p, t, sel_map):
    n_blk = sel_map.shape[1]
    imp = jnp.einsum('bghqc,cn->bgqn', p, sel_map)
    j = jnp.arange(n_blk)[None, :]
    cur = (t // SEL_BLOCK)[:, None]
    valid = j <= cur
    forced = valid & ((j == 0) | (j == cur) | (j == cur - 1))
    score = jnp.where(forced, jnp.inf, jnp.where(valid, imp, -jnp.inf))
    return lax.top_k(score, min(N_SEL, n_blk))[1]


def _sel_attend(q, rows, idx, t, slopes):
    B, G, Q, K = idx.shape
    pos = (idx[..., None] * SEL_BLOCK + jnp.arange(SEL_BLOCK)).reshape(B, G, Q, K * SEL_BLOCK)
    ks = rows[..., 0, :].reshape(B, G, Q, K * SEL_BLOCK, HEAD_DIM)
    vs = rows[..., 1, :].reshape(B, G, Q, K * SEL_BLOCK, HEAD_DIM)
    dist = t[:, None] - pos
    s = jnp.einsum('bghqd,bgqkd->bghqk', q, ks).astype(jnp.float32)
    s = s - slopes[:, :, None, None] * dist[:, :, None].astype(jnp.float32)
    p = _masked_softmax(s, dist[:, :, None] >= 0)
    return jnp.einsum('bghqk,bgqkd->bghqd', p, vs.astype(jnp.float32))


def _win_attend(q, kv, kpos, t, slopes):
    dist = t[:, None] - kpos[None, :]
    mask = (dist >= 0) & (dist < WINDOW) & (kpos >= 0)[None, :]
    s = jnp.einsum('bghqd,bsgd->bghqs', q, kv[:, :, 0]).astype(jnp.float32)
    s = s - slopes[:, :, None, None] * dist.astype(jnp.float32)
    p = _masked_softmax(s, mask)
    return jnp.einsum('bghqs,bsgd->bghqd', p, kv[:, :, 1].astype(jnp.float32))


def _combine(g, o_c, o_s, o_w):
    return g[..., 0:1] * o_c + g[..., 1:2] * o_s + g[..., 2:3] * o_w


def _nsa_prompt(q, kv_cmp, kv_sel, kv_win, gates, cmpk, cmpv):
    B, T = q.shape[:2]
    slopes = _alibi_slopes()
    qg = _to_groups(q.reshape(B, T, NSA_HEADS, HEAD_DIM)) * HEAD_DIM ** -0.5
    gg = _to_groups(jax.nn.sigmoid(gates.reshape(B, T, NSA_HEADS, 3)))
    kc = _compress(kv_cmp[:, :, 0], *cmpk)
    vc = _compress(kv_cmp[:, :, 1], *cmpv)
    n_blk = -(-T // SEL_BLOCK)
    sel_map = _cmp_to_sel(kc.shape[1], n_blk)
    zpad = ((0, 0), (0, 0), (0, 0))
    sel = jnp.pad(kv_sel, ((0, 0), (0, n_blk * SEL_BLOCK - T)) + zpad)
    sel = sel.reshape(B, n_blk, SEL_BLOCK, 2, NSA_KV_HEADS, HEAD_DIM).transpose(0, 4, 1, 2, 3, 5)
    win = jnp.pad(kv_win, ((0, 0), (WINDOW, 0)) + zpad)

    def block(i):
        q0 = i * Q_BLOCK
        t = q0 + jnp.arange(Q_BLOCK)
        qb = lax.dynamic_slice_in_dim(qg, q0, Q_BLOCK, axis=3)
        gb = lax.dynamic_slice_in_dim(gg, q0, Q_BLOCK, axis=3)
        o_c, p = _cmp_attend(qb, kc, vc, t, slopes)
        idx = _select_blocks(p, t, sel_map)
        rows = jax.vmap(jax.vmap(lambda sb, ix: sb[ix]))(sel, idx)
        o_s = _sel_attend(qb, rows, idx, t, slopes)
        wb = lax.dynamic_slice_in_dim(win, q0, WINDOW + Q_BLOCK, axis=1)
        kpos = q0 - WINDOW + jnp.arange(WINDOW + Q_BLOCK)
        o_w = _win_attend(qb, wb, kpos, t, slopes)
        return _combine(gb, o_c, o_s, o_w)

    o = lax.map(block, jnp.arange(T // Q_BLOCK))
    return o.transpose(1, 0, 4, 2, 3, 5).reshape(B, T, NSA_WIDTH)


def _nsa_sample(q, kv_cmp, kv_sel, kv_win, gates, cache_cmp, cache_sel, cache_win, page_table, layer, cmpk, cmpv):
    Bd, Tn = q.shape[:2]
    L = PAST_LEN + Tn
    t = PAST_LEN + jnp.arange(Tn)
    slopes = _alibi_slopes()
    qg = _to_groups(q.reshape(Bd, Tn, NSA_HEADS, HEAD_DIM)) * HEAD_DIM ** -0.5
    gg = _to_groups(jax.nn.sigmoid(gates.reshape(Bd, Tn, NSA_HEADS, 3)))
    past = cache_cmp[layer, page_table].reshape(Bd, PAST_LEN, 2, NSA_KV_HEADS, HEAD_DIM)
    full = jnp.concatenate([past, kv_cmp], axis=1)
    kc = _compress(full[:, :, 0], *cmpk)
    vc = _compress(full[:, :, 1], *cmpv)
    o_c, p = _cmp_attend(qg, kc, vc, t, slopes)
    n_blk = -(-L // SEL_BLOCK)
    idx = _select_blocks(p, t, _cmp_to_sel(kc.shape[1], n_blk))
    n_past_blk = PAST_LEN // SEL_BLOCK
    n_new_blk = n_blk - n_past_blk
    per_page = PAGE_SIZE // SEL_BLOCK
    off = jnp.arange(SEL_BLOCK)
    b_ix = jnp.arange(Bd)[:, None, None, None]
    g_ix = jnp.arange(NSA_KV_HEADS)[None, :, None, None, None]
    jp = jnp.minimum(idx, n_past_blk - 1)
    phys = page_table[b_ix, jp // per_page]
    in_page = ((jp % per_page) * SEL_BLOCK)[..., None] + off
    past_rows = cache_sel[layer, phys[..., None], in_page, :, g_ix]
    new = jnp.pad(kv_sel, ((0, 0), (0, n_new_blk * SEL_BLOCK - Tn), (0, 0), (0, 0), (0, 0)))
    jn = jnp.clip(idx - n_past_blk, 0, n_new_blk - 1)
    new_rows = new[b_ix[..., None], (jn * SEL_BLOCK)[..., None] + off, :, g_ix]
    rows = jnp.where((idx < n_past_blk)[..., None, None, None], past_rows, new_rows.astype(past_rows.dtype))
    o_s = _sel_attend(qg, rows, idx, t, slopes)
    kw = jnp.concatenate([cache_win, kv_win.astype(cache_win.dtype)], axis=1)
    nbuf = cache_win.shape[1]
    kpos = PAST_LEN - nbuf + jnp.arange(nbuf + Tn)
    o_w = _win_attend(qg, kw, kpos, t, slopes)
    o = _combine(gg, o_c, o_s, o_w)
    keep = min(WINDOW, nbuf + Tn)
    return o.transpose(0, 3, 1, 2, 4).reshape(Bd, Tn, NSA_WIDTH), kw[:, nbuf + Tn - keep:]


def _short_conv(u, buf, w, b):
    T = u.shape[1]
    ext = jnp.concatenate([buf.astype(u.dtype), u], axis=1)
    y = b
    for i in range(CONV_W):
        y = y + ext[:, i:i + T] * w[i]
    return jax.nn.silu(y), ext[:, T:]


def _gated_delta(q, k, v, g, beta, s0):
    B, T, H, dk = k.shape
    dv = v.shape[-1]
    C = min(GDN_CHUNK, T)
    n = -(-T // C)
    pad = n * C - T

    def ch4(a):
        a = jnp.pad(a.astype(jnp.float32), ((0, 0), (0, pad), (0, 0), (0, 0)))
        return a.reshape(B, n, C, H, a.shape[-1]).transpose(1, 0, 3, 2, 4)

    def ch3(a):
        a = jnp.pad(a.astype(jnp.float32), ((0, 0), (0, pad), (0, 0)))
        return a.reshape(B, n, C, H).transpose(1, 0, 3, 2)

    qc = ch4(q) * dk ** -0.5
    kc = ch4(k)
    vc = ch4(v)
    bc = ch3(beta)
    gc = jnp.cumsum(ch3(g), axis=-1)
    i = jnp.arange(C)
    tri = i[:, None] >= i[None, :]
    strict = i[:, None] > i[None, :]
    decay = jnp.exp(jnp.where(tri, gc[..., :, None] - gc[..., None, :], -jnp.inf))
    kb = kc * bc[..., None]
    lmat = jnp.where(strict, jnp.einsum('...id,...jd->...ij', kb, kc) * decay, 0.0)
    eye = jnp.broadcast_to(jnp.eye(C, dtype=jnp.float32), lmat.shape)
    tmat = lax.linalg.triangular_solve(eye + lmat, eye, left_side=True, lower=True)
    u = tmat @ (vc * bc[..., None])
    w = tmat @ (kb * jnp.exp(gc)[..., None])
    a_intra = jnp.where(tri, jnp.einsum('...id,...jd->...ij', qc, kc) * decay, 0.0)

    def step(s, xs):
        q_, k_, u_, w_, g_, a_ = xs
        v_new = u_ - w_ @ s
        o = (q_ * jnp.exp(g_)[..., None]) @ s + a_ @ v_new
        g_last = g_[..., -1:]
        s = s * jnp.exp(g_last)[..., None] + jnp.einsum('bhcd,bhce->bhde', k_ * jnp.exp(g_last - g_)[..., None], v_new)
        return s, o

    s, o = lax.scan(step, s0.astype(jnp.float32), (qc, kc, u, w, gc, a_intra))
    o = o.transpose(1, 0, 3, 2, 4).reshape(B, n * C, H, dv)[:, :T]
    return o, s


def _gdn(qkv, a, b, z, conv_buf, s0, conv_w, conv_b, a_log, dt_bias, gnorm):
    B, T, _ = qkv.shape
    u, conv_new = _short_conv(qkv, conv_buf, conv_w, conv_b)
    q, k, v = jnp.split(u, [GDN_QK_WIDTH, 2 * GDN_QK_WIDTH], axis=-1)
    q = _l2norm(q.reshape(B, T, GDN_HEADS, GDN_DK))
    k = _l2norm(k.reshape(B, T, GDN_HEADS, GDN_DK))
    v = v.reshape(B, T, GDN_HEADS, GDN_DV)
    beta = jax.nn.sigmoid(b.astype(jnp.float32))
    g = -jnp.exp(a_log.astype(jnp.float32)) * jax.nn.softplus(a.astype(jnp.float32) + dt_bias.astype(jnp.float32))
    o, s_new = _gated_delta(q, k, v, g, beta, s0)
    o = _rmsnorm(o, gnorm).reshape(B, T, GDN_WIDTH) * jax.nn.silu(z)
    return o, conv_new, s_new


def _mix_in(x, c, norm_g, w_ada, b_ada, w_in):
    ada = jax.nn.silu(c) @ w_ada + b_ada
    shift, scale, gate = jnp.split(ada, 3, axis=-1)
    h = _rmsnorm(x, norm_g) * (1.0 + scale[:, None]) + shift[:, None]
    split_pts = np.cumsum(np.array(IN_SIZES))[:-1].tolist()
    return jnp.split(h @ w_in, split_pts, axis=-1), gate


def _mix_out(x, gate, o_a, o_b, gm, w_o_nsa, w_o_gdn, w_out):
    gm_a, gm_b = jnp.split(jax.nn.sigmoid(gm), 2, axis=-1)
    m = gm_a * (o_a @ w_o_nsa) + gm_b * (o_b @ w_o_gdn)
    return x + gate[:, None] * (m @ w_out)


def _layer_prompt(x, c, lw):
    norm_g, w_ada, b_ada, w_in, cmpk, cmpv, conv_w, conv_b, a_log, dt_bias, gnorm, w_o_nsa, w_o_gdn, w_out = lw
    B, T, _ = x.shape
    (q_a, kv_a, g_a, z_a, qkv_b, a_b, b_b, z_b, gm), gate = _mix_in(x, c, norm_g, w_ada, b_ada, w_in)
    kv_a = kv_a.reshape(B, T, 3, 2, NSA_KV_HEADS, HEAD_DIM)
    o_a = _nsa_prompt(q_a, kv_a[:, :, 0], kv_a[:, :, 1], kv_a[:, :, 2], g_a, cmpk, cmpv) * jax.nn.silu(z_a)
    conv0 = jnp.zeros((B, CONV_W - 1, CONV_DIM), qkv_b.dtype)
    s0 = jnp.zeros((B, GDN_HEADS, GDN_DK, GDN_DV), jnp.float32)
    o_b, conv_new, s_new = _gdn(qkv_b, a_b, b_b, z_b, conv0, s0, conv_w, conv_b, a_log, dt_bias, gnorm)
    y = _mix_out(x, gate, o_a, o_b, gm, w_o_nsa, w_o_gdn, w_out)
    return y, (kv_a[:, :, 0], kv_a[:, :, 1], kv_a[:, T - min(WINDOW, T):, 2], conv_new, s_new)


def _layer_sample(x, c, cache_cmp, cache_sel, cache_win, conv_buf, s0, page_table, layer, lw):
    norm_g, w_ada, b_ada, w_in, cmpk, cmpv, conv_w, conv_b, a_log, dt_bias, gnorm, w_o_nsa, w_o_gdn, w_out = lw
    B, T, _ = x.shape
    (q_a, kv_a, g_a, z_a, qkv_b, a_b, b_b, z_b, gm), gate = _mix_in(x, c, norm_g, w_ada, b_ada, w_in)
    kv_a = kv_a.reshape(B, T, 3, 2, NSA_KV_HEADS, HEAD_DIM)
    o_a, win_new = _nsa_sample(q_a, kv_a[:, :, 0], kv_a[:, :, 1], kv_a[:, :, 2], g_a, cache_cmp, cache_sel, cache_win, page_table, layer, cmpk, cmpv)
    o_a = o_a * jax.nn.silu(z_a)
    o_b, conv_new, s_new = _gdn(qkv_b, a_b, b_b, z_b, conv_buf, s0, conv_w, conv_b, a_log, dt_bias, gnorm)
    y = _mix_out(x, gate, o_a, o_b, gm, w_o_nsa, w_o_gdn, w_out)
    return y, (kv_a[:, :, 0], kv_a[:, :, 1], win_new, conv_new, s_new)


def setup_inputs(seed: int = 0) -> dict:
    key = jax.random.key(seed)
    ks = iter(jax.random.split(key, 40))

    def nrm(shape, s):
        return jax.random.normal(next(ks), shape, jnp.float32) * s

    n_pages = PAST_LEN // PAGE_SIZE
    n_pool = (DEC_BATCH * n_pages * 5) // 4
    win_buf = min(WINDOW, PAST_LEN)
    kv_row = (2, NSA_KV_HEADS, HEAD_DIM)
    perm = jax.random.permutation(next(ks), n_pool)
    page_table = perm[:DEC_BATCH * n_pages].reshape(DEC_BATCH, n_pages).astype(jnp.int32)
    dt = jnp.exp(jax.random.uniform(next(ks), (DEPTH, GDN_HEADS), jnp.float32, math.log(1e-3), math.log(1e-1)))
    dt_bias = dt + jnp.log(-jnp.expm1(-dt))
    a_log = jnp.log(jax.random.uniform(next(ks), (DEPTH, GDN_HEADS), jnp.float32, 1.0, 16.0))
    return {
        'x_prompt': nrm((BATCH, SEQ, D_MODEL), 1.0),
        'x_sample': nrm((DEC_BATCH, DEC_SEQ, D_MODEL), 1.0),
        'cache_cmp_kv': nrm((DEPTH, n_pool, PAGE_SIZE) + kv_row, 1.0),
        'cache_sel_kv': nrm((DEPTH, n_pool, PAGE_SIZE) + kv_row, 1.0),
        'cache_win_kv': nrm((DEPTH, DEC_BATCH, win_buf) + kv_row, 1.0),
        'state_conv': nrm((DEPTH, DEC_BATCH, CONV_W - 1, CONV_DIM), 1.0),
        'state_gdn': nrm((DEPTH, DEC_BATCH, GDN_HEADS, GDN_DK, GDN_DV), 0.3),
        'page_table': page_table,
        'c_prompt': nrm((BATCH, D_MODEL), 1.0),
        'c_sample': nrm((DEC_BATCH, D_MODEL), 1.0),
        'norm_g': 1.0 + nrm((DEPTH, D_MODEL), 0.02),
        'w_ada': nrm((DEPTH, D_MODEL, 3 * D_MODEL), 0.5 * D_MODEL ** -0.5),
        'b_ada': nrm((DEPTH, 3 * D_MODEL), 0.02),
        'w_in': nrm((DEPTH, D_MODEL, IN_WIDTH), D_MODEL ** -0.5),
        'cmp_pe_k': nrm((DEPTH, CMP_BLOCK, HEAD_DIM), 0.1),
        'cmp_w1_k': nrm((DEPTH, CMP_BLOCK, HEAD_DIM, CMP_HIDDEN), (CMP_BLOCK * HEAD_DIM) ** -0.5),
        'cmp_w2_k': nrm((DEPTH, CMP_HIDDEN, HEAD_DIM), CMP_HIDDEN ** -0.5),
        'cmp_pe_v': nrm((DEPTH, CMP_BLOCK, HEAD_DIM), 0.1),
        'cmp_w1_v': nrm((DEPTH, CMP_BLOCK, HEAD_DIM, CMP_HIDDEN), (CMP_BLOCK * HEAD_DIM) ** -0.5),
        'cmp_w2_v': nrm((DEPTH, CMP_HIDDEN, HEAD_DIM), CMP_HIDDEN ** -0.5),
        'conv_w': nrm((DEPTH, CONV_W, CONV_DIM), CONV_W ** -0.5),
        'conv_b': nrm((DEPTH, CONV_DIM), 0.02),
        'gdn_a_log': a_log,
        'gdn_dt_bias': dt_bias,
        'gdn_norm_g': 1.0 + nrm((DEPTH, GDN_DV), 0.02),
        'w_o_nsa': nrm((DEPTH, NSA_WIDTH, D_MODEL), NSA_WIDTH ** -0.5),
        'w_o_gdn': nrm((DEPTH, GDN_WIDTH, D_MODEL), GDN_WIDTH ** -0.5),
        'w_out': nrm((DEPTH, D_MODEL, D_MODEL), D_MODEL ** -0.5),
        'final_g': 1.0 + nrm((D_MODEL,), 0.02),
    }


def reference(x_prompt, x_sample, cache_cmp_kv, cache_sel_kv, cache_win_kv, state_conv, state_gdn, page_table,
              c_prompt, c_sample, norm_g, w_ada, b_ada, w_in, cmp_pe_k, cmp_w1_k, cmp_w2_k, cmp_pe_v, cmp_w1_v,
              cmp_w2_v, conv_w, conv_b, gdn_a_log, gdn_dt_bias, gdn_norm_g, w_o_nsa, w_o_gdn, w_out, final_g):
    xp = x_prompt
    xs = x_sample
    sp = []
    ss = []
    for l in range(DEPTH):
        lw = (norm_g[l], w_ada[l], b_ada[l], w_in[l],
              (cmp_pe_k[l], cmp_w1_k[l], cmp_w2_k[l]), (cmp_pe_v[l], cmp_w1_v[l], cmp_w2_v[l]),
              conv_w[l], conv_b[l], gdn_a_log[l], gdn_dt_bias[l], gdn_norm_g[l],
              w_o_nsa[l], w_o_gdn[l], w_out[l])
        xp, st_p = _layer_prompt(xp, c_prompt, lw)
        sp.append(st_p)
        xs, st_s = _layer_sample(xs, c_sample, cache_cmp_kv, cache_sel_kv, cache_win_kv[l], state_conv[l],
                                 state_gdn[l], page_table, l, lw)
        ss.append(st_s)
    y_prompt = _rmsnorm(xp, final_g)
    y_sample = _rmsnorm(xs, final_g)
    new_cmp_kv_prompt = jnp.stack([s[0] for s in sp])
    new_sel_kv_prompt = jnp.stack([s[1] for s in sp])
    new_win_kv_prompt = jnp.stack([s[2] for s in sp])
    new_conv_prompt = jnp.stack([s[3] for s in sp])
    new_gdn_prompt = jnp.stack([s[4] for s in sp])
    new_cmp_kv_sample = jnp.stack([s[0] for s in ss])
    new_sel_kv_sample = jnp.stack([s[1] for s in ss])
    new_win_kv_sample = jnp.stack([s[2] for s in ss])
    new_conv_sample = jnp.stack([s[3] for s in ss])
    new_gdn_sample = jnp.stack([s[4] for s in ss])
    return (y_prompt, y_sample, new_cmp_kv_prompt, new_sel_kv_prompt, new_win_kv_prompt, new_conv_prompt,
            new_gdn_prompt, new_cmp_kv_sample, new_sel_kv_sample, new_win_kv_sample, new_conv_sample,
            new_gdn_sample)
```

```python
import functools
import math

import numpy as np
import jax
import jax.numpy as jnp
from jax import lax
from jax.experimental import pallas as pl
from jax.experimental.pallas import tpu as pltpu

F32 = jnp.float32
BF16 = jnp.bfloat16

NSA_HEADS = 8
NSA_KV_HEADS = 2
NSA_GROUP = NSA_HEADS // NSA_KV_HEADS
HEAD_DIM = 64
CMP_BLOCK = 32
CMP_STRIDE = 16
CMP_HIDDEN = 64
SEL_BLOCK = 64
N_SEL = 16
WINDOW = 512
Q_BLOCK = 128
GDN_HEADS = 8
GDN_DK = 64
GDN_DV = 64
CONV_W = 4
GDN_CHUNK = 64
PAGE_SIZE = 128
EPS = 1e-6

NSA_WIDTH = NSA_HEADS * HEAD_DIM
KV_ROW = 2 * NSA_KV_HEADS * HEAD_DIM
GDN_QK_WIDTH = GDN_HEADS * GDN_DK
GDN_WIDTH = GDN_HEADS * GDN_DV
CONV_DIM = 2 * GDN_QK_WIDTH + GDN_WIDTH
SMALL_W = 128
SMALL_A0 = 3 * NSA_HEADS
SMALL_B0 = SMALL_A0 + GDN_HEADS

NEG = -1e30
VMEM_LIMIT = 56 * 1024 * 1024

_NT = (((1,), (1,)), ((), ()))
_NN = (((1,), (0,)), ((), ()))


def _alibi_slopes():
    h = np.arange(1, NSA_HEADS + 1, dtype=np.float32)
    return [float(v) for v in np.power(np.float32(2.0), -np.float32(8.0) * h / np.float32(NSA_HEADS))]


_SLOPES = _alibi_slopes()


def _dot(a, b, dims=_NN):
    return lax.dot_general(a, b, dims, preferred_element_type=F32)


def _split(a):
    hi = a.astype(BF16)
    lo = (a - hi.astype(F32)).astype(BF16)
    return hi, lo


def _dot3(a, b, dims=_NN):
    ah, al = _split(a)
    bh, bl = _split(b)
    return _dot(ah, bh, dims) + _dot(ah, bl, dims) + _dot(al, bh, dims)


def _silu(x):
    return x * jax.nn.sigmoid(x)


def _cparams(sem):
    return pltpu.CompilerParams(dimension_semantics=sem, vmem_limit_bytes=VMEM_LIMIT)


def _ada_kernel(c_ref, w_ref, b_ref, o_ref):
    sc = _silu(c_ref[...]).astype(BF16)
    o_ref[...] = _dot(sc, w_ref[...]) + b_ref[...]


def _ada(c, w_bf, b):
    n, d = c.shape
    n_pad = -(-n // 8) * 8
    c = jnp.pad(c, ((0, n_pad - n), (0, 0)))
    out = pl.pallas_call(
        _ada_kernel,
        out_shape=jax.ShapeDtypeStruct((n_pad, w_bf.shape[1]), F32),
        name="ada",
    )(c, w_bf, b.reshape(1, -1))
    return out[:n]


_SEG = (("q", NSA_WIDTH), ("kvc", KV_ROW), ("kvs", KV_ROW), ("kvw", KV_ROW), ("za", NSA_WIDTH),
        ("qkvb", CONV_DIM), ("zb", GDN_WIDTH), ("gm", None), ("small", SMALL_W))


def _seg_offsets(d_model):
    offs, c = {}, 0
    for name, n in _SEG:
        n = 2 * d_model if n is None else n
        offs[name] = (c, n)
        c += n
    return offs, c


def _inproj_kernel(x_ref, scale_ref, shift_ref, ng_ref, w_ref,
                   q_ref, kvc_ref, kvs_ref, kvw_ref, kvcb_ref, kvsb_ref, kvwb_ref,
                   sza_ref, qkvb_ref, szb_ref, gms_ref, small_ref, *, offs):
    x = x_ref[...]
    y = x * lax.rsqrt(jnp.mean(x * x, axis=-1, keepdims=True) + EPS) * ng_ref[...]
    h = y * (1.0 + scale_ref[...]) + shift_ref[...]
    hb = h.astype(BF16)

    def seg(name):
        c0, n = offs[name]
        return _dot(hb, w_ref[:, c0:c0 + n])

    q_ref[...] = (seg("q") * (HEAD_DIM ** -0.5)).astype(BF16)
    for name, f_ref, b_ref in (("kvc", kvc_ref, kvcb_ref), ("kvs", kvs_ref, kvsb_ref), ("kvw", kvw_ref, kvwb_ref)):
        v = seg(name)
        f_ref[...] = v
        b_ref[...] = v.astype(BF16)
    sza_ref[...] = _silu(seg("za"))
    qkvb_ref[...] = seg("qkvb")
    szb_ref[...] = _silu(seg("zb"))
    gms_ref[...] = jax.nn.sigmoid(seg("gm"))
    small_ref[...] = seg("small")


def _inproj(x2, scale_arr, shift_arr, mod_specs, ng, w_bf, offs, tm):
    n, d = x2.shape
    wtot = w_bf.shape[1]
    row = lambda w: pl.BlockSpec((tm, w), lambda i: (i, 0))
    outs = [("q", BF16), ("kvc", F32), ("kvs", F32), ("kvw", F32), ("kvc", BF16), ("kvs", BF16), ("kvw", BF16),
            ("za", F32), ("qkvb", F32), ("zb", F32), ("gm", F32), ("small", F32)]
    out_shape = [jax.ShapeDtypeStruct((n, offs[nm][1]), dt) for nm, dt in outs]
    out_specs = [row(offs[nm][1]) for nm, _ in outs]
    return pl.pallas_call(
        functools.partial(_inproj_kernel, offs=offs),
        grid=(n // tm,),
        in_specs=[row(d), mod_specs[0], mod_specs[1],
                  pl.BlockSpec((1, d), lambda i: (0, 0)),
                  pl.BlockSpec((d, wtot), lambda i: (0, 0))],
        out_specs=out_specs,
        out_shape=out_shape,
        compiler_params=_cparams(("parallel",)),
        name="inproj",
    )(x2, scale_arr, shift_arr, ng, w_bf)


def _outproj_kernel(x_ref, oa_ref, ob_ref, gms_ref, gate_ref, wa_ref, wb_ref, wo_ref, fg_ref, y_ref, *, d):
    ma = _dot(oa_ref[...].astype(BF16), wa_ref[...])
    mb = _dot(ob_ref[...].astype(BF16), wb_ref[...])
    m = gms_ref[:, 0:d] * ma + gms_ref[:, d:2 * d] * mb
    y = x_ref[...] + gate_ref[...] * _dot(m.astype(BF16), wo_ref[...])
    y_ref[...] = y * lax.rsqrt(jnp.mean(y * y, axis=-1, keepdims=True) + EPS) * fg_ref[...]


def _outproj(x2, oa, ob, gms, gate_arr, gate_spec, wa, wb, wo, fg, tm):
    n, d = x2.shape
    row = lambda w: pl.BlockSpec((tm, w), lambda i: (i, 0))
    full = lambda a: pl.BlockSpec(a.shape, lambda i: (0, 0))
    return pl.pallas_call(
        functools.partial(_outproj_kernel, d=d),
        grid=(n // tm,),
        in_specs=[row(d), row(oa.shape[1]), row(ob.shape[1]), row(2 * d), gate_spec,
                  full(wa), full(wb), full(wo), full(fg)],
        out_specs=row(d),
        out_shape=jax.ShapeDtypeStruct((n, d), F32),
        compiler_params=_cparams(("parallel",)),
        name="outproj",
    )(x2, oa, ob, gms, gate_arr, wa, wb, wo, fg)


def _compress_math(x, w1_ref, pe_ref, w2_ref):
    n16 = x.shape[0]
    a0 = _dot(x, w1_ref[0])
    a1 = _dot(x, w1_ref[1])
    p0 = _dot(pe_ref[0], w1_ref[0]) + _dot(pe_ref[1], w1_ref[1])
    pre = a0 + pltpu.roll(a1, n16 - 1, 0) + p0[0:1]
    return _dot(_silu(pre).astype(BF16), w2_ref[...])


def _compress_kernel(x_ref, w1_ref, pe_ref, w2_ref, o_ref):
    o_ref[...] = _compress_math(x_ref[...], w1_ref, pe_ref, w2_ref).astype(BF16)


def _compress_prompt(kvcb, w1big, pebig, w2big, batch):
    n = kvcb.shape[0]
    n16 = n // batch // CMP_STRIDE
    x = kvcb.reshape(batch, n16, CMP_STRIDE * KV_ROW)
    return pl.pallas_call(
        _compress_kernel,
        grid=(batch,),
        in_specs=[pl.BlockSpec((None, n16, CMP_STRIDE * KV_ROW), lambda b: (b, 0, 0)),
                  pl.BlockSpec(w1big.shape, lambda b: (0, 0, 0)),
                  pl.BlockSpec(pebig.shape, lambda b: (0, 0, 0)),
                  pl.BlockSpec(w2big.shape, lambda b: (0, 0))],
        out_specs=pl.BlockSpec((None, n16, KV_ROW), lambda b: (b, 0, 0)),
        out_shape=jax.ShapeDtypeStruct((batch, n16, KV_ROW), BF16),
        compiler_params=_cparams(("parallel",)),
        name="compress_prompt",
    )(x, w1big, pebig, w2big)


def _compress_weights(pe_k, w1_k, w2_k, pe_v, w1_v, w2_v):
    r_cnt = CMP_BLOCK // CMP_STRIDE
    g = NSA_KV_HEADS
    eye2 = jnp.eye(2, dtype=F32)
    eyeg = jnp.eye(g, dtype=F32)
    w1 = jnp.stack([w1_k, w1_v]).reshape(2, r_cnt, CMP_STRIDE, HEAD_DIM, CMP_HIDDEN)
    w1big = jnp.einsum("krsde,kK,gG->rskgdKGe", w1, eye2, eyeg)
    w1big = w1big.reshape(r_cnt, CMP_STRIDE * KV_ROW, 2 * g * CMP_HIDDEN).astype(BF16)
    pe = jnp.stack([pe_k, pe_v]).reshape(2, r_cnt, CMP_STRIDE, HEAD_DIM)
    pebig = jnp.broadcast_to(pe.transpose(1, 2, 0, 3)[:, :, :, None, :], (r_cnt, CMP_STRIDE, 2, g, HEAD_DIM))
    pebig = pebig.reshape(r_cnt, 1, CMP_STRIDE * KV_ROW)
    pebig = jnp.pad(pebig, ((0, 0), (0, 7), (0, 0))).astype(BF16)
    w2 = jnp.stack([w2_k, w2_v])
    w2big = jnp.einsum("ked,kK,gG->kgeKGd", w2, eye2, eyeg).reshape(2 * g * CMP_HIDDEN, KV_ROW).astype(BF16)
    return w1big, pebig, w2big


def _sel_map(n_cmp_rows, n_blk, n_blk_pad):
    i = np.arange(n_cmp_rows)[:, None] * CMP_STRIDE
    j = np.arange(n_blk_pad)[None, :] * SEL_BLOCK
    ov = np.minimum(i + CMP_BLOCK, j + SEL_BLOCK) - np.maximum(i, j)
    m = np.clip(ov, 0, None).astype(np.float32) / np.float32(CMP_BLOCK)
    m[:, n_blk:] = 0.0
    return jnp.asarray(m, dtype=BF16)


def _softmax_block(s, mask):
    s = jnp.where(mask, s, NEG)
    m = jnp.max(s, axis=-1, keepdims=True)
    p = jnp.where(mask, jnp.exp(s - m), 0.0)
    return m, p


def _online_update(carry, s, mask, v_bf):
    m, l, acc = carry
    s = jnp.where(mask, s, NEG)
    m_new = jnp.maximum(m, jnp.max(s, axis=-1, keepdims=True))
    alpha = jnp.exp(m - m_new)
    p = jnp.where(mask, jnp.exp(s - m_new), 0.0)
    l = alpha * l + jnp.sum(p, axis=-1, keepdims=True)
    acc = alpha * acc + _dot(p.astype(BF16), v_bf)
    return m_new, l, acc


def _online_init(rows, hd):
    return (jnp.full((rows, 1), NEG, F32), jnp.zeros((rows, 1), F32), jnp.zeros((rows, hd), F32))


def _importance(p_sum, selmap):
    ph, pl_ = _split(p_sum)
    pl2 = (p_sum - ph.astype(F32) - pl_.astype(F32)).astype(BF16)
    return _dot(ph, selmap) + _dot(pl_, selmap) + _dot(pl2, selmap)


def _select_blocks(imp, t_col, n_blk_lanes):
    rows = imp.shape[0]
    jj = lax.broadcasted_iota(jnp.int32, (rows, n_blk_lanes), 1)
    cur = t_col // SEL_BLOCK
    valid = jj <= cur
    forced = (jj == 0) | (jj == cur) | (jj == cur - 1)
    score = jnp.where(valid, jnp.where(forced, -NEG, imp), NEG)
    jf = jj.astype(F32)

    def pick(_, carry):
        score, selm = carry
        m = jnp.max(score, axis=-1, keepdims=True)
        idx = jnp.min(jnp.where(score == m, jf, float(n_blk_lanes)), axis=-1, keepdims=True)
        hit = jf == idx
        return jnp.where(hit, 3.0 * NEG, score), jnp.where(hit, 1.0, selm)

    _, selm = lax.fori_loop(0, N_SEL, pick, (score, jnp.zeros((rows, n_blk_lanes), F32)))
    return selm


def _expand_blocks(selm_bf, k0, kt, reps):
    n_blk = selm_bf.shape[1]
    blk = lax.broadcasted_iota(jnp.int32, (n_blk, kt), 0)
    key = lax.broadcasted_iota(jnp.int32, (n_blk, kt), 1)
    e = jnp.where(blk == k0 // SEL_BLOCK + key // SEL_BLOCK, 1.0, 0.0).astype(BF16)
    mk = _dot(selm_bf, e)
    return jnp.concatenate([mk] * reps, axis=0) > 0.5


def _nsa_prompt_kernel(q_ref, kc_ref, ksel_ref, kwin_ref, small_ref, sza_ref, selmap_ref, o_ref, *, seq, kt):
    qb = Q_BLOCK
    hd = HEAD_DIM
    q0 = pl.program_id(1) * qb
    qall = q_ref[...]
    gates = jax.nn.sigmoid(small_ref[...])
    n_cmp = kc_ref.shape[0]
    n_blk = selmap_ref.shape[1]
    t_col = q0 + lax.broadcasted_iota(jnp.int32, (qb, 1), 0)
    t4 = jnp.concatenate([t_col] * NSA_GROUP, axis=0)
    wlen = WINDOW + qb
    outs = []
    for g in range(NSA_KV_HEADS):
        kcol = slice(g * hd, (g + 1) * hd)
        vcol = slice((NSA_KV_HEADS + g) * hd, (NSA_KV_HEADS + g + 1) * hd)
        qg = jnp.concatenate([qall[:, (g * NSA_GROUP + h) * hd:(g * NSA_GROUP + h + 1) * hd]
                              for h in range(NSA_GROUP)], axis=0)
        slope = jnp.concatenate([jnp.full((qb, 1), _SLOPES[g * NSA_GROUP + h], F32) for h in range(NSA_GROUP)], axis=0)

        s = _dot(qg, kc_ref[:, kcol], _NT)
        end = lax.broadcasted_iota(jnp.int32, (1, n_cmp), 1) * CMP_STRIDE + (CMP_BLOCK - 1)
        dist = t4 - end
        mask = dist >= 0
        _, p = _softmax_block(s - slope * dist.astype(F32), mask)
        p = p / jnp.maximum(jnp.sum(p, axis=-1, keepdims=True), 1e-30)
        o_c = _dot(p.astype(BF16), kc_ref[:, vcol])
        psum = p[0:qb]
        for h in range(1, NSA_GROUP):
            psum = psum + p[h * qb:(h + 1) * qb]
        imp = _importance(psum, selmap_ref[...])
        selm_bf = _select_blocks(imp, t_col, n_blk).astype(BF16)

        def sel_tile(j, carry):
            k0 = pl.multiple_of(j * kt, kt)
            s = _dot(qg, ksel_ref[pl.ds(k0, kt), kcol], _NT)
            dist = t4 - (k0 + lax.broadcasted_iota(jnp.int32, (1, kt), 1))
            mask = _expand_blocks(selm_bf, k0, kt, NSA_GROUP) & (dist >= 0)
            return _online_update(carry, s - slope * dist.astype(F32), mask, ksel_ref[pl.ds(k0, kt), vcol])

        _, l, acc = lax.fori_loop(0, (q0 + qb - 1) // kt + 1, sel_tile, _online_init(NSA_GROUP * qb, hd))
        o_s = acc / jnp.maximum(l, 1e-30)

        w0 = pl.multiple_of(jnp.maximum(q0 - WINDOW, 0), qb)
        s = _dot(qg, kwin_ref[pl.ds(w0, wlen), kcol], _NT)
        dist = t4 - (w0 + lax.broadcasted_iota(jnp.int32, (1, wlen), 1))
        mask = (dist >= 0) & (dist < WINDOW)
        _, p = _softmax_block(s - slope * dist.astype(F32), mask)
        o_w = _dot(p.astype(BF16), kwin_ref[pl.ds(w0, wlen), vcol])
        o_w = o_w / jnp.maximum(jnp.sum(p, axis=-1, keepdims=True), 1e-30)

        for h in range(NSA_GROUP):
            c = 3 * (g * NSA_GROUP + h)
            r = slice(h * qb, (h + 1) * qb)
            outs.append(gates[:, c:c + 1] * o_c[r] + gates[:, c + 1:c + 2] * o_s[r] + gates[:, c + 2:c + 3] * o_w[r])
    o_ref[...] = jnp.concatenate(outs, axis=1) * sza_ref[...]


def _nsa_prompt(qs, kc, kvsb, kvwb, small, sza, batch):
    n = qs.shape[0]
    seq = n // batch
    nq = seq // Q_BLOCK
    n_cmp = kc.shape[1]
    n_blk = -(-seq // SEL_BLOCK)
    kt = min(512, seq)
    selmap = _sel_map(n_cmp, n_blk, n_blk)
    row = lambda w: pl.BlockSpec((Q_BLOCK, w), lambda b, i: (b * nq + i, 0))
    seqspec = pl.BlockSpec((None, seq, KV_ROW), lambda b, i: (b, 0, 0))
    return pl.pallas_call(
        functools.partial(_nsa_prompt_kernel, seq=seq, kt=kt),
        grid=(batch, nq),
        in_specs=[row(NSA_WIDTH),
                  pl.BlockSpec((None, n_cmp, KV_ROW), lambda b, i: (b, 0, 0)),
                  seqspec, seqspec,
                  row(SMALL_W), row(NSA_WIDTH),
                  pl.BlockSpec(selmap.shape, lambda b, i: (0, 0))],
        out_specs=row(NSA_WIDTH),
        out_shape=jax.ShapeDtypeStruct((n, NSA_WIDTH), F32),
        compiler_params=_cparams(("parallel", "arbitrary")),
        name="nsa_prompt",
    )(qs, kc, kvsb.reshape(batch, seq, KV_ROW), kvwb.reshape(batch, seq, KV_ROW), small, sza, selmap)


def _softplus(x):
    return jnp.maximum(x, 0.0) + jnp.log1p(jnp.exp(-jnp.abs(x)))


def _gdn_prompt_kernel(qkv_ref, small_ref, szb_ref, cw_ref, cb_ref, alog_ref, dtb_ref, gn_ref,
                       o_ref, sfin_ref, s_ref, prev_ref):
    c = pl.program_id(1)
    ck = GDN_CHUNK
    dk = GDN_DK

    @pl.when(c == 0)
    def _():
        s_ref[...] = jnp.zeros_like(s_ref)
        prev_ref[...] = jnp.zeros_like(prev_ref)

    u = qkv_ref[...]
    ext = jnp.concatenate([prev_ref[...], u], axis=0)
    y = cb_ref[...]
    for i in range(CONV_W):
        y = y + ext[8 - (CONV_W - 1) + i:8 - (CONV_W - 1) + i + ck] * cw_ref[i:i + 1, :]
    prev_ref[...] = u[ck - 8:ck]
    act = _silu(y)

    small = small_ref[...]
    g_all = -jnp.exp(alog_ref[...]) * _softplus(small + dtb_ref[...])
    beta_all = jax.nn.sigmoid(small)
    ii = lax.broadcasted_iota(jnp.int32, (ck, ck), 0)
    jj = lax.broadcasted_iota(jnp.int32, (ck, ck), 1)
    tri = jnp.where(ii >= jj, 1.0, 0.0).astype(F32)
    eye = jnp.where(ii == jj, 1.0, 0.0).astype(F32)
    gc_all = lax.dot_general(tri, g_all, _NN, precision=lax.Precision.HIGHEST, preferred_element_type=F32)
    gc_t = gc_all.T

    outs = []
    for h in range(GDN_HEADS):
        q = act[:, h * dk:(h + 1) * dk]
        k = act[:, GDN_QK_WIDTH + h * dk:GDN_QK_WIDTH + (h + 1) * dk]
        v = act[:, 2 * GDN_QK_WIDTH + h * GDN_DV:2 * GDN_QK_WIDTH + (h + 1) * GDN_DV]
        q = q * lax.rsqrt(jnp.sum(q * q, axis=-1, keepdims=True) + EPS) * (dk ** -0.5)
        k = k * lax.rsqrt(jnp.sum(k * k, axis=-1, keepdims=True) + EPS)
        beta = beta_all[:, SMALL_B0 + h:SMALL_B0 + h + 1]
        gc = gc_all[:, SMALL_A0 + h:SMALL_A0 + h + 1]
        gr = gc_t[SMALL_A0 + h:SMALL_A0 + h + 1, :]
        decay = jnp.exp(jnp.where(ii >= jj, gc - gr, NEG))
        kb = k * beta
        lmat = jnp.where(ii > jj, _dot3(kb, k, _NT) * decay, 0.0)
        a_in = _dot3(q, k, _NT) * decay
        x = eye - lmat
        pw = lmat
        for _ in range(int(math.log2(ck)) - 1):
            pw = _dot3(pw, pw)
            x = x + _dot3(x, pw)
        eg = jnp.exp(gc)
        uu = _dot3(x, v * beta)
        ww = _dot3(x, kb * eg)
        s = s_ref[h]
        v_new = uu - _dot3(ww, s)
        o = _dot3(q * eg, s) + _dot3(a_in, v_new)
        g_last = gc[ck - 1:ck, :]
        kd = k * jnp.exp(g_last - gc)
        s_ref[h] = s * jnp.exp(g_last) + _dot3(kd.T, v_new)
        o = o * lax.rsqrt(jnp.mean(o * o, axis=-1, keepdims=True) + EPS) * gn_ref[...]
        outs.append(o)
    o_ref[...] = jnp.concatenate(outs, axis=1) * szb_ref[...]

    @pl.when(c == pl.num_programs(1) - 1)
    def _():
        sfin_ref[...] = s_ref[...]


def _lane_params(a_log, dt_bias):
    alog_l = jnp.zeros((1, SMALL_W), F32).at[0, SMALL_A0:SMALL_A0 + GDN_HEADS].set(a_log)
    dtb_l = jnp.zeros((1, SMALL_W), F32).at[0, SMALL_A0:SMALL_A0 + GDN_HEADS].set(dt_bias)
    return alog_l, dtb_l


def _gdn_prompt(qkvb, small, szb, conv_w, conv_b, a_log, dt_bias, gnorm, batch):
    n = qkvb.shape[0]
    seq = n // batch
    nc = seq // GDN_CHUNK
    alog_l, dtb_l = _lane_params(a_log, dt_bias)
    row = lambda w: pl.BlockSpec((GDN_CHUNK, w), lambda b, c: (b * nc + c, 0))
    full = lambda a: pl.BlockSpec(a.shape, lambda b, c: (0,) * a.ndim)
    cb = conv_b.reshape(1, -1)
    gn = gnorm.reshape(1, -1)
    return pl.pallas_call(
        _gdn_prompt_kernel,
        grid=(batch, nc),
        in_specs=[row(CONV_DIM), row(SMALL_W), row(GDN_WIDTH), full(conv_w), full(cb), full(alog_l), full(dtb_l), full(gn)],
        out_specs=[row(GDN_WIDTH),
                   pl.BlockSpec((None, GDN_HEADS, GDN_DK, GDN_DV), lambda b, c: (b, 0, 0, 0))],
        out_shape=[jax.ShapeDtypeStruct((n, GDN_WIDTH), F32),
                   jax.ShapeDtypeStruct((batch, GDN_HEADS, GDN_DK, GDN_DV), F32)],
        scratch_shapes=[pltpu.VMEM((GDN_HEADS, GDN_DK, GDN_DV), F32), pltpu.VMEM((8, CONV_DIM), F32)],
        compiler_params=_cparams(("parallel", "arbitrary")),
        name="gdn_prompt",
    )(qkvb, small, szb, conv_w, cb, alog_l, dtb_l, gn)


def _prep_w_in(w_in, d_model):
    sizes = (NSA_WIDTH, 6 * NSA_KV_HEADS * HEAD_DIM, 3 * NSA_HEADS, NSA_WIDTH, CONV_DIM, GDN_HEADS, GDN_HEADS,
             GDN_WIDTH, 2 * d_model)
    pts = np.cumsum(np.array(sizes))[:-1].tolist()
    q_a, kv_a, g_a, z_a, qkv_b, a_b, b_b, z_b, gm = jnp.split(w_in, pts, axis=1)
    small = jnp.concatenate([g_a, a_b, b_b], axis=1)
    small = jnp.pad(small, ((0, 0), (0, SMALL_W - small.shape[1])))
    return jnp.concatenate([q_a, kv_a, z_a, qkv_b, z_b, gm, small], axis=1).astype(BF16)


def _prompt_path(x, ada, lw, cw):
    (norm_g, w_bf, offs, conv_w, conv_b, a_log, dt_bias, gnorm, wa, wb, wo, final_g) = lw
    batch, seq, d = x.shape
    n = batch * seq
    x2 = x.reshape(n, d)
    ada3 = ada.reshape(batch, 1, 3 * d)
    tm = 256
    mod = lambda k: pl.BlockSpec((None, 1, d), lambda i: (i * tm // seq, 0, k))
    (qs, kvc, kvs, kvw, kvcb, kvsb, kvwb, sza, qkvb, szb, gms, small) = _inproj(
        x2, ada3, ada3, (mod(1), mod(0)), norm_g.reshape(1, d), w_bf, offs, tm)
    kc = _compress_prompt(kvcb, *cw, batch)
    o_a = _nsa_prompt(qs, kc, kvsb, kvwb, small, sza, batch)
    o_b, s_new = _gdn_prompt(qkvb, small, szb, conv_w, conv_b, a_log, dt_bias, gnorm, batch)
    tmo = 512
    gate_spec = pl.BlockSpec((None, 1, d), lambda i: (i * tmo // seq, 0, 2))
    y = _outproj(x2, o_a, o_b, gms, ada3, gate_spec, wa, wb, wo, final_g.reshape(1, d), tmo)
    kvshape = (1, batch, seq, 2, NSA_KV_HEADS, HEAD_DIM)
    keep = min(WINDOW, seq)
    new_win = kvw.reshape(kvshape)[:, :, seq - keep:]
    new_conv = qkvb.reshape(batch, seq, CONV_DIM)[None, :, seq - (CONV_W - 1):]
    return (y.reshape(batch, seq, d), kvc.reshape(kvshape), kvs.reshape(kvshape), new_win, new_conv, s_new[None])


def _page_fetch(pt_ref, cache_hbm, buf, sem, npages, rows_per_page):
    def copy(seq, slot, j):
        r0 = pl.multiple_of(j * rows_per_page, rows_per_page)
        return pltpu.make_async_copy(cache_hbm.at[pt_ref[seq, j]], buf.at[slot, pl.ds(r0, rows_per_page), :],
                                     sem.at[slot])

    def start(seq, slot):
        def body(j, c):
            copy(seq, slot, j).start()
            return c
        lax.fori_loop(0, npages, body, 0)

    def wait(seq, slot):
        def body(j, c):
            copy(seq, slot, j).wait()
            return c
        lax.fori_loop(0, npages, body, 0)

    return start, wait


def _fetch_this_prefetch_next(start, wait):
    b = pl.program_id(0)
    nb = pl.num_programs(0)

    @pl.when(b == 0)
    def _():
        start(0, 0)

    @pl.when(b + 1 < nb)
    def _():
        start(b + 1, (b + 1) % 2)

    slot = b % 2
    wait(b, slot)
    return slot


def _compress_sample_kernel(pt_ref, cache_hbm, w1_ref, pe_ref, w2_ref, o_ref, buf, sem, *, npages):
    start, wait = _page_fetch(pt_ref, cache_hbm, buf, sem, npages, PAGE_SIZE // CMP_STRIDE)
    slot = _fetch_this_prefetch_next(start, wait)
    o_ref[...] = _compress_math(buf[slot].astype(BF16), w1_ref, pe_ref, w2_ref).astype(BF16)


def _compress_sample(cache_cmp, page_table, w1big, pebig, w2big):
    nseq, npages = page_table.shape
    n_pool = cache_cmp.shape[0]
    rpp = PAGE_SIZE // CMP_STRIDE
    width = CMP_STRIDE * KV_ROW
    n16 = npages * rpp
    cache = cache_cmp.reshape(n_pool, rpp, width)
    grid_spec = pltpu.PrefetchScalarGridSpec(
        num_scalar_prefetch=1,
        grid=(nseq,),
        in_specs=[pl.BlockSpec(memory_space=pl.ANY),
                  pl.BlockSpec(w1big.shape, lambda b, pt: (0, 0, 0)),
                  pl.BlockSpec(pebig.shape, lambda b, pt: (0, 0, 0)),
                  pl.BlockSpec(w2big.shape, lambda b, pt: (0, 0))],
        out_specs=pl.BlockSpec((None, n16, KV_ROW), lambda b, pt: (b, 0, 0)),
        scratch_shapes=[pltpu.VMEM((2, n16, width), F32), pltpu.SemaphoreType.DMA((2,))],
    )
    return pl.pallas_call(
        functools.partial(_compress_sample_kernel, npages=npages),
        grid_spec=grid_spec,
        out_shape=jax.ShapeDtypeStruct((nseq, n16, KV_ROW), BF16),
        compiler_params=_cparams(("arbitrary",)),
        name="compress_sample",
    )(page_table, cache, w1big, pebig, w2big)


def _nsa_sample_kernel(pt_ref, q_ref, kc_ref, cache_hbm, win_ref, nsel_ref, nwin_ref, small_ref, sza_ref, selmap_ref,
                       o_ref, wout_ref, buf, sem, *, npages, tn, kt):
    hd = HEAD_DIM
    rt = 8
    start, wait = _page_fetch(pt_ref, cache_hbm, buf, sem, npages, PAGE_SIZE)
    slot = _fetch_this_prefetch_next(start, wait)
    past = npages * PAGE_SIZE
    nbuf = win_ref.shape[0]
    n_cmp = kc_ref.shape[0]
    n_blk_lanes = selmap_ref.shape[1]
    q8 = q_ref[...].astype(F32)
    gates = jax.nn.sigmoid(small_ref[...])
    tau = lax.broadcasted_iota(jnp.int32, (rt, 1), 0) % tn
    t_col = past + tau
    t4 = jnp.concatenate([t_col] * NSA_GROUP, axis=0)
    rows = NSA_GROUP * rt
    newcol = lax.broadcasted_iota(jnp.int32, (1, SMALL_W), 1)
    new_dist = t4 - (past + newcol)
    new_ok = (newcol < tn) & (new_dist >= 0)
    zpad = jnp.zeros((SMALL_W - rt, hd), F32)
    outs = []
    for g in range(NSA_KV_HEADS):
        kcol = slice(g * hd, (g + 1) * hd)
        vcol = slice((NSA_KV_HEADS + g) * hd, (NSA_KV_HEADS + g + 1) * hd)
        qg = jnp.concatenate([q8[:, (g * NSA_GROUP + h) * hd:(g * NSA_GROUP + h + 1) * hd]
                              for h in range(NSA_GROUP)], axis=0).astype(BF16)
        slope = jnp.concatenate([jnp.full((rt, 1), _SLOPES[g * NSA_GROUP + h], F32) for h in range(NSA_GROUP)], axis=0)

        def new_tile(ref, col):
            return jnp.concatenate([ref[:, col], zpad], axis=0).astype(BF16)

        s = _dot(qg, kc_ref[:, kcol], _NT)
        end = lax.broadcasted_iota(jnp.int32, (1, n_cmp), 1) * CMP_STRIDE + (CMP_BLOCK - 1)
        dist = t4 - end
        mask = dist >= 0
        _, p = _softmax_block(s - slope * dist.astype(F32), mask)
        p = p / jnp.maximum(jnp.sum(p, axis=-1, keepdims=True), 1e-30)
        o_c = _dot(p.astype(BF16), kc_ref[:, vcol])
        psum = p[0:rt]
        for h in range(1, NSA_GROUP):
            psum = psum + p[h * rt:(h + 1) * rt]
        imp = _importance(psum, selmap_ref[...])
        selm = _select_blocks(imp, t_col, n_blk_lanes)
        selm_bf = selm.astype(BF16)

        def sel_tile(j, carry):
            k0 = pl.multiple_of(j * kt, kt)
            kv = buf[slot, pl.ds(k0, kt), :]
            s = _dot(qg, kv[:, kcol].astype(BF16), _NT)
            dist = t4 - (k0 + lax.broadcasted_iota(jnp.int32, (1, kt), 1))
            mask = _expand_blocks(selm_bf, k0, kt, NSA_GROUP)
            return _online_update(carry, s - slope * dist.astype(F32), mask, kv[:, vcol].astype(BF16))

        carry = lax.fori_loop(0, past // kt, sel_tile, _online_init(rows, hd))
        nb_new = past // SEL_BLOCK
        new_sel = jnp.concatenate([selm[:, nb_new:nb_new + 1]] * NSA_GROUP, axis=0) > 0.5
        s = _dot(qg, new_tile(nsel_ref, kcol), _NT)
        _, l, acc = _online_update(carry, s - slope * new_dist.astype(F32), new_ok & new_sel, new_tile(nsel_ref, vcol))
        o_s = acc / jnp.maximum(l, 1e-30)

        kw = win_ref[...]
        s = _dot(qg, kw[:, kcol].astype(BF16), _NT)
        dist = t4 - (past - nbuf + lax.broadcasted_iota(jnp.int32, (1, nbuf), 1))
        mask = (dist >= 0) & (dist < WINDOW)
        carry = _online_update(_online_init(rows, hd), s - slope * dist.astype(F32), mask, kw[:, vcol].astype(BF16))
        s = _dot(qg, new_tile(nwin_ref, kcol), _NT)
        _, l, acc = _online_update(carry, s - slope * new_dist.astype(F32), new_ok, new_tile(nwin_ref, vcol))
        o_w = acc / jnp.maximum(l, 1e-30)

        for h in range(NSA_GROUP):
            c = 3 * (g * NSA_GROUP + h)
            r = slice(h * rt, (h + 1) * rt)
            outs.append(gates[:, c:c + 1] * o_c[r] + gates[:, c + 1:c + 2] * o_s[r] + gates[:, c + 2:c + 3] * o_w[r])
    o_ref[...] = jnp.concatenate(outs, axis=1) * sza_ref[...]
    wout_ref[0:nbuf - tn, :] = win_ref[tn:nbuf, :]
    wout_ref[nbuf - tn:nbuf, :] = nwin_ref[0:tn, :]


def _rep8(a, nseq, tn):
    a = a.reshape(nseq, tn, a.shape[-1])
    return jnp.concatenate([a] * (8 // tn), axis=1)


def _nsa_sample(qs, kc, cache_sel, cache_win, kvs, kvw, small, sza, page_table, tn):
    nseq, npages = page_table.shape
    past = npages * PAGE_SIZE
    n_pool = cache_sel.shape[0]
    nbuf = cache_win.shape[1]
    assert nbuf == WINDOW and 8 % tn == 0 and tn <= SEL_BLOCK
    n_cmp = kc.shape[1]
    n_blk = -(-(past + tn) // SEL_BLOCK)
    n_blk_lanes = -(-n_blk // 128) * 128
    selmap = _sel_map(n_cmp, n_blk, n_blk_lanes)
    kt = min(1024, past)
    seq3 = lambda r, w: pl.BlockSpec((None, r, w), lambda b, pt: (b, 0, 0))
    grid_spec = pltpu.PrefetchScalarGridSpec(
        num_scalar_prefetch=1,
        grid=(nseq,),
        in_specs=[seq3(8, NSA_WIDTH), seq3(n_cmp, KV_ROW), pl.BlockSpec(memory_space=pl.ANY), seq3(nbuf, KV_ROW),
                  seq3(8, KV_ROW), seq3(8, KV_ROW), seq3(8, SMALL_W), seq3(8, NSA_WIDTH),
                  pl.BlockSpec(selmap.shape, lambda b, pt: (0, 0))],
        out_specs=[seq3(8, NSA_WIDTH), seq3(nbuf, KV_ROW)],
        scratch_shapes=[pltpu.VMEM((2, past, KV_ROW), F32), pltpu.SemaphoreType.DMA((2,))],
    )
    o8, win_new = pl.pallas_call(
        functools.partial(_nsa_sample_kernel, npages=npages, tn=tn, kt=kt),
        grid_spec=grid_spec,
        out_shape=[jax.ShapeDtypeStruct((nseq, 8, NSA_WIDTH), F32), jax.ShapeDtypeStruct((nseq, nbuf, KV_ROW), F32)],
        compiler_params=_cparams(("arbitrary",)),
        name="nsa_sample",
    )(page_table, _rep8(qs, nseq, tn), kc, cache_sel.reshape(n_pool, PAGE_SIZE, KV_ROW),
      cache_win.reshape(nseq, nbuf, KV_ROW), _rep8(kvs, nseq, tn), _rep8(kvw, nseq, tn), _rep8(small, nseq, tn),
      _rep8(sza, nseq, tn), selmap)
    return o8[:, :tn].reshape(nseq * tn, NSA_WIDTH), win_new


def _gdn_sample_kernel(eq_ref, ek_ref, ev_ref, cwq_ref, cwk_ref, cwv_ref, cbq_ref, cbk_ref, cbv_ref, small_ref,
                       alog_ref, dtb_ref, szb_ref, gn_ref, s_ref, o_ref, so_ref, qs_ref, ks_ref):
    h = pl.program_id(0)
    tn = o_ref.shape[0]
    dk, dv, nseq = so_ref.shape

    def conv(e_ref, cw_ref, cb_ref, t):
        y = cb_ref[...]
        for i in range(CONV_W):
            y = y + e_ref[t + i] * cw_ref[:, i:i + 1]
        return _silu(y)

    so_ref[...] = s_ref[...]
    neg_rate = -jnp.exp(alog_ref[...])
    for t in range(tn):
        q = conv(eq_ref, cwq_ref, cbq_ref, t)
        k = conv(ek_ref, cwk_ref, cbk_ref, t)
        v = conv(ev_ref, cwv_ref, cbv_ref, t)
        qs_ref[...] = q * lax.rsqrt(jnp.sum(q * q, axis=0, keepdims=True) + EPS) * (dk ** -0.5)
        ks_ref[...] = k * lax.rsqrt(jnp.sum(k * k, axis=0, keepdims=True) + EPS)
        a_in = small_ref[t, pl.ds(SMALL_A0 + h, 1), :]
        b_in = small_ref[t, pl.ds(SMALL_B0 + h, 1), :]
        decay = jnp.exp(neg_rate * _softplus(a_in + dtb_ref[...]))
        beta = jax.nn.sigmoid(b_in)

        def ks_step(i, acc):
            return acc + ks_ref[pl.ds(i, 1), :] * so_ref[i]

        k_s = lax.fori_loop(0, dk, ks_step, jnp.zeros((dv, nseq), F32), unroll=8)
        delta = beta * (v - decay * k_s)

        def upd_step(i, acc):
            s_new = decay * so_ref[i] + ks_ref[pl.ds(i, 1), :] * delta
            so_ref[i] = s_new
            return acc + qs_ref[pl.ds(i, 1), :] * s_new

        o = lax.fori_loop(0, dk, upd_step, jnp.zeros((dv, nseq), F32), unroll=8)
        o = o * lax.rsqrt(jnp.mean(o * o, axis=0, keepdims=True) + EPS) * gn_ref[...]
        o_ref[t] = o * szb_ref[t]


def _gdn_sample(qkvb, small, szb, state_conv, state_gdn, conv_w, conv_b, a_log, dt_bias, gnorm, tn):
    nseq = state_gdn.shape[0]
    ext = jnp.concatenate([state_conv, qkvb.reshape(nseq, tn, CONV_DIM)], axis=1)
    ext_t = ext.transpose(1, 2, 0)
    small_t = small.reshape(nseq, tn, SMALL_W).transpose(1, 2, 0)
    szb_t = szb.reshape(nseq, tn, GDN_WIDTH).transpose(1, 2, 0)
    s_t = state_gdn.transpose(1, 2, 3, 0)
    cw_t = conv_w.T
    cb_t = conv_b.reshape(-1, 1)
    alog_b = jnp.broadcast_to(a_log[:, None, None], (GDN_HEADS, 1, nseq))
    dtb_b = jnp.broadcast_to(dt_bias[:, None, None], (GDN_HEADS, 1, nseq))
    gn = gnorm.reshape(-1, 1)
    nqk = GDN_QK_WIDTH // GDN_DK
    chan = lambda off: pl.BlockSpec((CONV_W - 1 + tn, GDN_DK, nseq), lambda h: (0, off + h, 0))
    cwb = lambda off: pl.BlockSpec((GDN_DK, CONV_W), lambda h: (off + h, 0))
    cbb = lambda off: pl.BlockSpec((GDN_DK, 1), lambda h: (off + h, 0))
    perhead = pl.BlockSpec((None, 1, nseq), lambda h: (h, 0, 0))
    o_t, s_new = pl.pallas_call(
        _gdn_sample_kernel,
        grid=(GDN_HEADS,),
        in_specs=[chan(0), chan(nqk), chan(2 * nqk), cwb(0), cwb(nqk), cwb(2 * nqk), cbb(0), cbb(nqk), cbb(2 * nqk),
                  pl.BlockSpec((tn, SMALL_W, nseq), lambda h: (0, 0, 0)), perhead, perhead,
                  pl.BlockSpec((tn, GDN_DV, nseq), lambda h: (0, h, 0)),
                  pl.BlockSpec((GDN_DV, 1), lambda h: (0, 0)),
                  pl.BlockSpec((None, GDN_DK, GDN_DV, nseq), lambda h: (h, 0, 0, 0))],
        out_specs=[pl.BlockSpec((tn, GDN_DV, nseq), lambda h: (0, h, 0)),
                   pl.BlockSpec((None, GDN_DK, GDN_DV, nseq), lambda h: (h, 0, 0, 0))],
        out_shape=[jax.ShapeDtypeStruct((tn, GDN_WIDTH, nseq), F32),
                   jax.ShapeDtypeStruct((GDN_HEADS, GDN_DK, GDN_DV, nseq), F32)],
        scratch_shapes=[pltpu.VMEM((GDN_DK, nseq), F32), pltpu.VMEM((GDN_DK, nseq), F32)],
        compiler_params=_cparams(("parallel",)),
        name="gdn_sample",
    )(ext_t, ext_t, ext_t, cw_t, cw_t, cw_t, cb_t, cb_t, cb_t, small_t, alog_b, dtb_b, szb_t, gn, s_t)
    o_b = o_t.transpose(2, 0, 1).reshape(nseq * tn, GDN_WIDTH)
    return o_b, ext[:, tn:], s_new.transpose(3, 0, 1, 2)


def _sample_path(x, ada, lw, cw, cache_cmp, cache_sel, cache_win, state_conv, state_gdn, page_table):
    (norm_g, w_bf, offs, conv_w, conv_b, a_log, dt_bias, gnorm, wa, wb, wo, final_g) = lw
    nseq, tn, d = x.shape
    n = nseq * tn
    x2 = x.reshape(n, d)
    ada_rows = jnp.repeat(ada, tn, axis=0)
    tm = min(256, n)
    mod = lambda k: pl.BlockSpec((tm, d), lambda i: (i, k))
    (qs, kvc, kvs, kvw, _, _, _, sza, qkvb, szb, gms, small) = _inproj(
        x2, ada_rows, ada_rows, (mod(1), mod(0)), norm_g.reshape(1, d), w_bf, offs, tm)
    kc = _compress_sample(cache_cmp, page_table, *cw)
    o_a, win_new = _nsa_sample(qs, kc, cache_sel, cache_win, kvs, kvw, small, sza, page_table, tn)
    o_b, conv_new, s_new = _gdn_sample(qkvb, small, szb, state_conv, state_gdn, conv_w, conv_b, a_log, dt_bias, gnorm, tn)
    y = _outproj(x2, o_a, o_b, gms, ada_rows, mod(2), wa, wb, wo, final_g.reshape(1, d), tm)
    kvshape = (1, nseq, tn, 2, NSA_KV_HEADS, HEAD_DIM)
    win_shape = (1, nseq, win_new.shape[1], 2, NSA_KV_HEADS, HEAD_DIM)
    return (y.reshape(nseq, tn, d), kvc.reshape(kvshape), kvs.reshape(kvshape), win_new.reshape(win_shape),
            conv_new[None], s_new[None])


def kernel(x_prompt, x_sample, cache_cmp_kv, cache_sel_kv, cache_win_kv, state_conv, state_gdn, page_table, c_prompt, c_sample, norm_g, w_ada, b_ada, w_in, cmp_pe_k, cmp_w1_k, cmp_w2_k, cmp_pe_v, cmp_w1_v, cmp_w2_v, conv_w, conv_b, gdn_a_log, gdn_dt_bias, gdn_norm_g, w_o_nsa, w_o_gdn, w_out, final_g):
    assert norm_g.shape[0] == 1, "single trunk layer"
    d = x_prompt.shape[-1]
    l = 0
    offs, _ = _seg_offsets(d)
    w_bf = _prep_w_in(w_in[l], d)
    ada = _ada(jnp.concatenate([c_prompt, c_sample], axis=0), w_ada[l].astype(BF16), b_ada[l])
    cw = _compress_weights(cmp_pe_k[l], cmp_w1_k[l], cmp_w2_k[l], cmp_pe_v[l], cmp_w1_v[l], cmp_w2_v[l])
    lw = (norm_g[l], w_bf, offs, conv_w[l], conv_b[l], gdn_a_log[l], gdn_dt_bias[l], gdn_norm_g[l],
          w_o_nsa[l].astype(BF16), w_o_gdn[l].astype(BF16), w_out[l].astype(BF16), final_g)
    nb = c_prompt.shape[0]
    yp, cmp_p, sel_p, win_p, conv_p, gdn_p = _prompt_path(x_prompt, ada[:nb], lw, cw)
    ys, cmp_s, sel_s, win_s, conv_s, gdn_s = _sample_path(
        x_sample, ada[nb:], lw, cw, cache_cmp_kv[l], cache_sel_kv[l], cache_win_kv[l], state_conv[l], state_gdn[l],
        page_table)
    return (yp, ys, cmp_p, sel_p, win_p, conv_p, gdn_p, cmp_s, sel_s, win_s, conv_s, gdn_s)
```

```python
import functools
import math

import numpy as np
import jax
import jax.numpy as jnp
from jax import lax
from jax.experimental import pallas as pl
from jax.experimental.pallas import tpu as pltpu

F32 = jnp.float32
BF16 = jnp.bfloat16

NSA_HEADS = 8
NSA_KV_HEADS = 2
NSA_GROUP = NSA_HEADS // NSA_KV_HEADS
HEAD_DIM = 64
CMP_BLOCK = 32
CMP_STRIDE = 16
CMP_HIDDEN = 64
SEL_BLOCK = 64
N_SEL = 16
WINDOW = 512
Q_BLOCK = 128
GDN_HEADS = 8
GDN_DK = 64
GDN_DV = 64
CONV_W = 4
GDN_CHUNK = 64
PAGE_SIZE = 128
EPS = 1e-6

NSA_WIDTH = NSA_HEADS * HEAD_DIM
KV_ROW = 2 * NSA_KV_HEADS * HEAD_DIM
GDN_QK_WIDTH = GDN_HEADS * GDN_DK
GDN_WIDTH = GDN_HEADS * GDN_DV
CONV_DIM = 2 * GDN_QK_WIDTH + GDN_WIDTH
SMALL_W = 128
SMALL_A0 = 3 * NSA_HEADS
SMALL_B0 = SMALL_A0 + GDN_HEADS

NEG = -1e30
MASKV = -(2.0 ** 100)
AUG = 128
POS_SPLIT = 128
VMEM_LIMIT = 56 * 1024 * 1024

_NT = (((1,), (1,)), ((), ()))
_NN = (((1,), (0,)), ((), ()))


def _alibi_slopes():
    h = np.arange(1, NSA_HEADS + 1, dtype=np.float32)
    return [float(v) for v in np.power(np.float32(2.0), -np.float32(8.0) * h / np.float32(NSA_HEADS))]


_SLOPES = _alibi_slopes()


def _dot(a, b, dims=_NN):
    return lax.dot_general(a, b, dims, preferred_element_type=F32)


def _split(a):
    hi = a.astype(BF16)
    lo = (a - hi.astype(F32)).astype(BF16)
    return hi, lo


def _dot3s(a_split, b_split, dims=_NN):
    (ah, al), (bh, bl) = a_split, b_split
    return _dot(ah, bh, dims) + _dot(ah, bl, dims) + _dot(al, bh, dims)


def _silu(x):
    return x * jax.nn.sigmoid(x)


def _cparams(sem):
    return pltpu.CompilerParams(dimension_semantics=sem, vmem_limit_bytes=VMEM_LIMIT)


def _ada_kernel(c_ref, w_ref, b_ref, o_ref):
    sc = _silu(c_ref[...]).astype(BF16)
    o_ref[...] = _dot(sc, w_ref[...]) + b_ref[...]


def _ada(c, w_bf, b):
    n, d = c.shape
    n_pad = -(-n // 8) * 8
    c = jnp.pad(c, ((0, n_pad - n), (0, 0)))
    out = pl.pallas_call(
        _ada_kernel,
        out_shape=jax.ShapeDtypeStruct((n_pad, w_bf.shape[1]), F32),
        name="ada",
    )(c, w_bf, b.reshape(1, -1))
    return out[:n]


_SEG = (("q", NSA_WIDTH), ("kvc", KV_ROW), ("kvs", KV_ROW), ("kvw", KV_ROW), ("za", NSA_WIDTH),
        ("qkvb", CONV_DIM), ("zb", GDN_WIDTH), ("gm", None), ("small", SMALL_W))


def _seg_offsets(d_model):
    offs, c = {}, 0
    for name, n in _SEG:
        n = 2 * d_model if n is None else n
        offs[name] = (c, n)
        c += n
    return offs, c


def _aug_cols(rows, c0, c1):
    lane = lax.broadcasted_iota(jnp.int32, (rows, AUG - HEAD_DIM), 1)
    return jnp.where(lane == 0, c0, jnp.where(lane == 1, c1, 0.0)).astype(F32)


def _aug_keys(kv, pos):
    hi = jnp.floor(pos * (1.0 / POS_SPLIT))
    cols = _aug_cols(kv.shape[0], hi, pos - hi * POS_SPLIT)
    parts = []
    for g in range(NSA_KV_HEADS):
        parts += [kv[:, g * HEAD_DIM:(g + 1) * HEAD_DIM], cols]
    k_aug = jnp.concatenate(parts, axis=1).astype(BF16)
    v_t = kv[:, NSA_KV_HEADS * HEAD_DIM:].T.astype(BF16)
    return k_aug, v_t


def _inproj_kernel(x_ref, scale_ref, shift_ref, ng_ref, w_ref,
                   q_ref, kvc_ref, kvs_ref, kvw_ref, kvcb_ref, ksel_ref, vselt_ref, kwin_ref, vwint_ref,
                   sza_ref, qkvb_ref, szb_ref, gms_ref, small_ref, *, offs, seq):
    x = x_ref[...]
    tm = x.shape[0]
    y = x * lax.rsqrt(jnp.mean(x * x, axis=-1, keepdims=True) + EPS) * ng_ref[...]
    h = y * (1.0 + scale_ref[...]) + shift_ref[...]
    hb = h.astype(BF16)

    def seg(name):
        c0, n = offs[name]
        return _dot(hb, w_ref[:, c0:c0 + n])

    qv = seg("q") * (HEAD_DIM ** -0.5)
    parts = []
    for hh in range(NSA_HEADS):
        parts += [qv[:, hh * HEAD_DIM:(hh + 1) * HEAD_DIM], _aug_cols(tm, _SLOPES[hh] * POS_SPLIT, _SLOPES[hh])]
    q_ref[...] = jnp.concatenate(parts, axis=1).astype(BF16)

    pos = ((pl.program_id(0) * tm) % seq + lax.broadcasted_iota(jnp.int32, (tm, 1), 0)).astype(F32)
    v = seg("kvc")
    kvc_ref[...] = v
    kvcb_ref[...] = v.astype(BF16)
    for name, f_ref, k_ref, vt_ref in (("kvs", kvs_ref, ksel_ref, vselt_ref), ("kvw", kvw_ref, kwin_ref, vwint_ref)):
        v = seg(name)
        f_ref[...] = v
        k_ref[...], vt_ref[...] = _aug_keys(v, pos)
    sza_ref[...] = _silu(seg("za"))
    qkvb_ref[...] = seg("qkvb")
    szb_ref[...] = _silu(seg("zb"))
    gms_ref[...] = jax.nn.sigmoid(seg("gm"))
    small_ref[...] = seg("small")


def _inproj(x2, scale_arr, shift_arr, mod_specs, ng, w_bf, offs, tm, seq):
    n, d = x2.shape
    wtot = w_bf.shape[1]
    nbatch = n // seq
    spt = seq // tm
    vw = NSA_KV_HEADS * HEAD_DIM
    row = lambda w: pl.BlockSpec((tm, w), lambda i: (i, 0))
    rows = lambda w, dt: (jax.ShapeDtypeStruct((n, w), dt), row(w))
    vt = (jax.ShapeDtypeStruct((nbatch, vw, seq), BF16), pl.BlockSpec((None, vw, tm), lambda i: (i // spt, 0, i % spt)))
    outs = [rows(NSA_HEADS * AUG, BF16), rows(KV_ROW, F32), rows(KV_ROW, F32), rows(KV_ROW, F32), rows(KV_ROW, BF16),
            rows(NSA_KV_HEADS * AUG, BF16), vt, rows(NSA_KV_HEADS * AUG, BF16), vt,
            rows(offs["za"][1], F32), rows(offs["qkvb"][1], F32), rows(offs["zb"][1], F32), rows(offs["gm"][1], F32),
            rows(offs["small"][1], F32)]
    out_shape = [o[0] for o in outs]
    out_specs = [o[1] for o in outs]
    return pl.pallas_call(
        functools.partial(_inproj_kernel, offs=offs, seq=seq),
        grid=(n // tm,),
        in_specs=[row(d), mod_specs[0], mod_specs[1],
                  pl.BlockSpec((1, d), lambda i: (0, 0)),
                  pl.BlockSpec((d, wtot), lambda i: (0, 0))],
        out_specs=out_specs,
        out_shape=out_shape,
        compiler_params=_cparams(("parallel",)),
        name="inproj",
    )(x2, scale_arr, shift_arr, ng, w_bf)


def _outproj_kernel(x_ref, oa_ref, ob_ref, gms_ref, gate_ref, wa_ref, wb_ref, wo_ref, fg_ref, y_ref, *, d):
    ma = _dot(oa_ref[...].astype(BF16), wa_ref[...])
    mb = _dot(ob_ref[...].astype(BF16), wb_ref[...])
    m = gms_ref[:, 0:d] * ma + gms_ref[:, d:2 * d] * mb
    y = x_ref[...] + gate_ref[...] * _dot(m.astype(BF16), wo_ref[...])
    y_ref[...] = y * lax.rsqrt(jnp.mean(y * y, axis=-1, keepdims=True) + EPS) * fg_ref[...]


def _outproj(x2, oa, ob, gms, gate_arr, gate_spec, wa, wb, wo, fg, tm):
    n, d = x2.shape
    row = lambda w: pl.BlockSpec((tm, w), lambda i: (i, 0))
    full = lambda a: pl.BlockSpec(a.shape, lambda i: (0, 0))
    return pl.pallas_call(
        functools.partial(_outproj_kernel, d=d),
        grid=(n // tm,),
        in_specs=[row(d), row(oa.shape[1]), row(ob.shape[1]), row(2 * d), gate_spec,
                  full(wa), full(wb), full(wo), full(fg)],
        out_specs=row(d),
        out_shape=jax.ShapeDtypeStruct((n, d), F32),
        compiler_params=_cparams(("parallel",)),
        name="outproj",
    )(x2, oa, ob, gms, gate_arr, wa, wb, wo, fg)


def _compress_math(x, w1_ref, pe_ref, w2_ref):
    n16 = x.shape[0]
    a0 = _dot(x, w1_ref[0])
    a1 = _dot(x, w1_ref[1])
    p0 = _dot(pe_ref[0], w1_ref[0]) + _dot(pe_ref[1], w1_ref[1])
    pre = a0 + pltpu.roll(a1, n16 - 1, 0) + p0[0:1]
    return _dot(_silu(pre).astype(BF16), w2_ref[...])


def _compress_kernel(x_ref, w1_ref, pe_ref, w2_ref, k_ref, vt_ref):
    kv = _compress_math(x_ref[...], w1_ref, pe_ref, w2_ref)
    n16 = kv.shape[0]
    end = (lax.broadcasted_iota(jnp.int32, (n16, 1), 0) * CMP_STRIDE + (CMP_BLOCK - 1)).astype(F32)
    k_ref[...], vt_ref[...] = _aug_keys(kv, end)


def _compress_prompt(kvcb, w1big, pebig, w2big, batch):
    n = kvcb.shape[0]
    n16 = n // batch // CMP_STRIDE
    x = kvcb.reshape(batch, n16, CMP_STRIDE * KV_ROW)
    vw = NSA_KV_HEADS * HEAD_DIM
    return pl.pallas_call(
        _compress_kernel,
        grid=(batch,),
        in_specs=[pl.BlockSpec((None, n16, CMP_STRIDE * KV_ROW), lambda b: (b, 0, 0)),
                  pl.BlockSpec(w1big.shape, lambda b: (0, 0, 0)),
                  pl.BlockSpec(pebig.shape, lambda b: (0, 0, 0)),
                  pl.BlockSpec(w2big.shape, lambda b: (0, 0))],
        out_specs=[pl.BlockSpec((None, n16, NSA_KV_HEADS * AUG), lambda b: (b, 0, 0)),
                   pl.BlockSpec((None, vw, n16), lambda b: (b, 0, 0))],
        out_shape=[jax.ShapeDtypeStruct((batch, n16, NSA_KV_HEADS * AUG), BF16),
                   jax.ShapeDtypeStruct((batch, vw, n16), BF16)],
        compiler_params=_cparams(("parallel",)),
        name="compress_prompt",
    )(x, w1big, pebig, w2big)


def _compress_weights(pe_k, w1_k, w2_k, pe_v, w1_v, w2_v):
    r_cnt = CMP_BLOCK // CMP_STRIDE
    g = NSA_KV_HEADS
    eye2 = jnp.eye(2, dtype=F32)
    eyeg = jnp.eye(g, dtype=F32)
    w1 = jnp.stack([w1_k, w1_v]).reshape(2, r_cnt, CMP_STRIDE, HEAD_DIM, CMP_HIDDEN)
    w1big = jnp.einsum("krsde,kK,gG->rskgdKGe", w1, eye2, eyeg)
    w1big = w1big.reshape(r_cnt, CMP_STRIDE * KV_ROW, 2 * g * CMP_HIDDEN).astype(BF16)
    pe = jnp.stack([pe_k, pe_v]).reshape(2, r_cnt, CMP_STRIDE, HEAD_DIM)
    pebig = jnp.broadcast_to(pe.transpose(1, 2, 0, 3)[:, :, :, None, :], (r_cnt, CMP_STRIDE, 2, g, HEAD_DIM))
    pebig = pebig.reshape(r_cnt, 1, CMP_STRIDE * KV_ROW)
    pebig = jnp.pad(pebig, ((0, 0), (0, 7), (0, 0))).astype(BF16)
    w2 = jnp.stack([w2_k, w2_v])
    w2big = jnp.einsum("ked,kK,gG->kgeKGd", w2, eye2, eyeg).reshape(2 * g * CMP_HIDDEN, KV_ROW).astype(BF16)
    return w1big, pebig, w2big


def _sel_map(n_cmp_rows, n_blk, n_blk_pad):
    i = np.arange(n_cmp_rows)[:, None] * CMP_STRIDE
    j = np.arange(n_blk_pad)[None, :] * SEL_BLOCK
    ov = np.minimum(i + CMP_BLOCK, j + SEL_BLOCK) - np.maximum(i, j)
    m = np.clip(ov, 0, None).astype(np.float32) / np.float32(CMP_BLOCK)
    m[:, n_blk:] = 0.0
    return jnp.asarray(m, dtype=BF16)


def _softmax_block(s, mask):
    s = jnp.where(mask, s, NEG)
    m = jnp.max(s, axis=-1, keepdims=True)
    p = jnp.where(mask, jnp.exp(s - m), 0.0)
    return m, p


def _online_update(carry, s, mask, v_bf):
    m, l, acc = carry
    s = jnp.where(mask, s, NEG)
    m_new = jnp.maximum(m, jnp.max(s, axis=-1, keepdims=True))
    alpha = jnp.exp(m - m_new)
    p = jnp.where(mask, jnp.exp(s - m_new), 0.0)
    l = alpha * l + jnp.sum(p, axis=-1, keepdims=True)
    acc = alpha * acc + _dot(p.astype(BF16), v_bf)
    return m_new, l, acc


def _online_init(rows, hd):
    return (jnp.full((rows, 1), NEG, F32), jnp.zeros((rows, 1), F32), jnp.zeros((rows, hd), F32))


def _importance(p_sum, selmap):
    ph, pl_ = _split(p_sum)
    pl2 = (p_sum - ph.astype(F32) - pl_.astype(F32)).astype(BF16)
    return _dot(ph, selmap) + _dot(pl_, selmap) + _dot(pl2, selmap)


def _select_blocks(imp, t_col, n_blk_lanes):
    rows = imp.shape[0]
    jj = lax.broadcasted_iota(jnp.int32, (rows, n_blk_lanes), 1)
    cur = t_col // SEL_BLOCK
    valid = jj <= cur
    forced = (jj == 0) | (jj == cur) | (jj == cur - 1)
    score = jnp.where(valid, jnp.where(forced, -NEG, imp), NEG)
    jf = jj.astype(F32)

    def pick(_, carry):
        score, selm = carry
        m = jnp.max(score, axis=-1, keepdims=True)
        idx = jnp.min(jnp.where(score == m, jf, float(n_blk_lanes)), axis=-1, keepdims=True)
        hit = jf == idx
        return jnp.where(hit, 3.0 * NEG, score), jnp.where(hit, 1.0, selm)

    _, selm = lax.fori_loop(0, N_SEL, pick, (score, jnp.zeros((rows, n_blk_lanes), F32)))
    return selm


def _expand_blocks(selm_bf, k0, kt, reps):
    n_blk = selm_bf.shape[1]
    blk = lax.broadcasted_iota(jnp.int32, (n_blk, kt), 0)
    key = lax.broadcasted_iota(jnp.int32, (n_blk, kt), 1)
    e = jnp.where(blk == k0 // SEL_BLOCK + key // SEL_BLOCK, 1.0, 0.0).astype(BF16)
    mk = _dot(selm_bf, e)
    return jnp.concatenate([mk] * reps, axis=0) > 0.5


def _flash_step_t(carry, s_t, v_t):
    m, l, acc = carry
    m_new = jnp.maximum(m, jnp.max(s_t, axis=0, keepdims=True))
    alpha = jnp.exp(m - m_new)
    p = jnp.exp(s_t - m_new)
    l = alpha * l + jnp.sum(p, axis=0, keepdims=True)
    acc = alpha * acc + _dot(v_t, p.astype(BF16))
    return m_new, l, acc


def _flash_init_t(cols, dv):
    return (jnp.full((1, cols), NEG, F32), jnp.zeros((1, cols), F32), jnp.zeros((dv, cols), F32))


def _select_blocks_t(imp_t, tq):
    nb, nq = imp_t.shape
    jj = lax.broadcasted_iota(jnp.int32, (nb, nq), 0)
    cur = tq // SEL_BLOCK
    forced = (jj == 0) | (jj == cur) | (jj == cur - 1)
    jf = jj.astype(F32)
    score = jnp.where(forced, NEG, jnp.where(jj <= cur, imp_t, NEG))
    selm = jnp.where(forced, 1.0, 0.0).astype(F32)
    for _ in range(N_SEL - 3):
        m = jnp.max(score, axis=0, keepdims=True)
        idx = jnp.min(jnp.where(score == m, jf, float(nb)), axis=0, keepdims=True)
        hit = jnp.where(m > 0.5 * NEG, idx, -1.0) == jf
        selm = jnp.where(hit, 1.0, selm)
        score = jnp.where(hit, NEG, score)
    return selm


def _nsa_prompt_kernel(q_ref, kc_ref, vct_ref, ksel_ref, vselt_ref, kwin_ref, vwint_ref, onehot_ref, small_ref, sza_ref,
                       selmapt_ref, o_ref, *, kt):
    qb = Q_BLOCK
    hd = HEAD_DIM
    q0 = pl.program_id(1) * qb
    n_cmp = kc_ref.shape[0]
    cols = NSA_GROUP * qb
    tq = q0 + lax.broadcasted_iota(jnp.int32, (1, qb), 1)
    gates_t = jax.nn.sigmoid(small_ref[...]).T
    wlen = WINDOW + qb
    w0 = pl.multiple_of(jnp.maximum(q0 - WINDOW, 0), qb)
    tile4 = lambda a: jnp.concatenate([a] * NSA_GROUP, axis=1)

    def keymask(pos_col, lo):
        dist = tq - pos_col
        ok = (dist >= 0) if lo is None else ((dist >= 0) & (dist < lo))
        return tile4(jnp.where(ok, 0.0, MASKV).astype(F32))

    cmp_bias = keymask(lax.broadcasted_iota(jnp.int32, (n_cmp, 1), 0) * CMP_STRIDE + (CMP_BLOCK - 1), None)
    win_bias = keymask(w0 + lax.broadcasted_iota(jnp.int32, (wlen, 1), 0), WINDOW)
    jlast = q0 // kt
    klast = pl.multiple_of(jlast * kt, kt)
    diag_bias = keymask(klast + lax.broadcasted_iota(jnp.int32, (kt, 1), 0), None)

    groups = range(NSA_KV_HEADS)
    kcols = [slice(g * AUG, (g + 1) * AUG) for g in groups]
    vrows = [slice(g * hd, (g + 1) * hd) for g in groups]
    q_heads = [[q_ref[:, (g * NSA_GROUP + h) * AUG:(g * NSA_GROUP + h + 1) * AUG] for h in range(NSA_GROUP)]
               for g in groups]
    q_cat = [jnp.concatenate(q_heads[g], axis=0) for g in groups]
    finish = lambda carry: carry[2] * (1.0 / jnp.maximum(carry[1], 1e-30))

    o_w = [finish(_flash_step_t(_flash_init_t(cols, hd), _dot(kwin_ref[pl.ds(w0, wlen), kcols[g]], q_cat[g], _NT) + win_bias,
                                vwint_ref[vrows[g], pl.ds(w0, wlen)])) for g in groups]

    o_c, q_full = [], []
    sm_t = selmapt_ref[...]
    for g in groups:
        s_t = _dot(kc_ref[:, kcols[g]], q_cat[g], _NT) + cmp_bias
        m = jnp.maximum(jnp.max(s_t, axis=0, keepdims=True), NEG)
        p = jnp.exp(s_t - m)
        p = p * (1.0 / jnp.maximum(jnp.sum(p, axis=0, keepdims=True), 1e-30))
        o_c.append(_dot(vct_ref[vrows[g], :], p.astype(BF16)))
        psum = p[:, 0:qb]
        for h in range(1, NSA_GROUP):
            psum = psum + p[:, h * qb:(h + 1) * qb]
        ph, pl_ = _split(psum)
        pl2 = (psum - ph.astype(F32) - pl_.astype(F32)).astype(BF16)
        imp_t = _dot(sm_t, ph) + _dot(sm_t, pl_) + _dot(sm_t, pl2)
        selm = _select_blocks_t(imp_t, tq).T
        blk_bias = jnp.where(selm > 0.5, 0.0, MASKV).astype(BF16)
        q_full.append(jnp.concatenate([jnp.concatenate([qh, blk_bias], axis=1) for qh in q_heads[g]], axis=0))

    def sel_step(k0, carries, bias):
        onehot = onehot_ref[pl.ds(k0, kt), :]
        out = []
        for g in groups:
            k_op = jnp.concatenate([ksel_ref[pl.ds(k0, kt), kcols[g]], onehot], axis=1)
            s_t = _dot(k_op, q_full[g], _NT)
            s_t = s_t if bias is None else s_t + bias
            out.append(_flash_step_t(carries[g], s_t, vselt_ref[vrows[g], pl.ds(k0, kt)]))
        return tuple(out)

    carries = lax.fori_loop(0, jlast, lambda j, c: sel_step(pl.multiple_of(j * kt, kt), c, None),
                            tuple(_flash_init_t(cols, hd) for _ in groups))
    o_s = [finish(c) for c in sel_step(klast, carries, diag_bias)]

    outs = []
    for g in groups:
        for h in range(NSA_GROUP):
            c = 3 * (g * NSA_GROUP + h)
            r = slice(h * qb, (h + 1) * qb)
            outs.append(gates_t[c:c + 1, :] * o_c[g][:, r] + gates_t[c + 1:c + 2, :] * o_s[g][:, r]
                        + gates_t[c + 2:c + 3, :] * o_w[g][:, r])
    o_ref[...] = jnp.concatenate(outs, axis=0).T * sza_ref[...]


def _nsa_prompt(q_aug, kc_aug, vc_t, ksel, vsel_t, kwin, vwin_t, small, sza, batch):
    n = q_aug.shape[0]
    seq = n // batch
    nq = seq // Q_BLOCK
    n_cmp = kc_aug.shape[1]
    n_blk = -(-seq // SEL_BLOCK)
    assert n_blk <= AUG and seq % Q_BLOCK == 0 and seq >= WINDOW + Q_BLOCK
    kt = min(512, seq)
    selmap_t = _sel_map(n_cmp, n_blk, AUG).T
    pos = np.arange(seq)[:, None] // SEL_BLOCK
    onehot = jnp.asarray(pos == np.arange(AUG)[None, :], dtype=BF16)
    vw = NSA_KV_HEADS * HEAD_DIM
    kw = NSA_KV_HEADS * AUG
    row = lambda w: pl.BlockSpec((Q_BLOCK, w), lambda b, i: (b * nq + i, 0))
    kspec = pl.BlockSpec((None, seq, kw), lambda b, i: (b, 0, 0))
    vspec = pl.BlockSpec((None, vw, seq), lambda b, i: (b, 0, 0))
    return pl.pallas_call(
        functools.partial(_nsa_prompt_kernel, kt=kt),
        grid=(batch, nq),
        in_specs=[row(NSA_HEADS * AUG),
                  pl.BlockSpec((None, n_cmp, kw), lambda b, i: (b, 0, 0)),
                  pl.BlockSpec((None, vw, n_cmp), lambda b, i: (b, 0, 0)),
                  kspec, vspec, kspec, vspec,
                  pl.BlockSpec(onehot.shape, lambda b, i: (0, 0)),
                  row(SMALL_W), row(NSA_WIDTH),
                  pl.BlockSpec(selmap_t.shape, lambda b, i: (0, 0))],
        out_specs=row(NSA_WIDTH),
        out_shape=jax.ShapeDtypeStruct((n, NSA_WIDTH), F32),
        compiler_params=_cparams(("parallel", "arbitrary")),
        name="nsa_prompt",
    )(q_aug, kc_aug, vc_t, ksel.reshape(batch, seq, kw), vsel_t, kwin.reshape(batch, seq, kw), vwin_t, onehot, small, sza,
      selmap_t)


def _softplus(x):
    return jnp.maximum(x, 0.0) + jnp.log1p(jnp.exp(-jnp.abs(x)))


def _gdn_prompt_kernel(qkv_ref, small_ref, szb_ref, cw_ref, cb_ref, alog_ref, dtb_ref, gn_ref,
                       o_ref, sfin_ref, s_ref, prev_ref):
    c = pl.program_id(1)
    ck = GDN_CHUNK
    dk = GDN_DK

    @pl.when(c == 0)
    def _():
        s_ref[...] = jnp.zeros_like(s_ref)
        prev_ref[...] = jnp.zeros_like(prev_ref)

    u = qkv_ref[...]
    ext = jnp.concatenate([prev_ref[...], u], axis=0)
    y = cb_ref[...]
    for i in range(CONV_W):
        y = y + ext[8 - (CONV_W - 1) + i:8 - (CONV_W - 1) + i + ck] * cw_ref[i:i + 1, :]
    prev_ref[...] = u[ck - 8:ck]
    act = _silu(y)

    small = small_ref[...]
    g_all = -jnp.exp(alog_ref[...]) * _softplus(small + dtb_ref[...])
    beta_all = jax.nn.sigmoid(small)
    ii = lax.broadcasted_iota(jnp.int32, (ck, ck), 0)
    jj = lax.broadcasted_iota(jnp.int32, (ck, ck), 1)
    tri = jnp.where(ii >= jj, 1.0, 0.0).astype(F32)
    eye = jnp.where(ii == jj, 1.0, 0.0).astype(F32)
    gc_all = lax.dot_general(tri, g_all, _NN, precision=lax.Precision.HIGHEST, preferred_element_type=F32)
    gc_t = gc_all.T

    heads = range(GDN_HEADS)
    dv = GDN_DV
    q, k, v, beta, gc, decay = [], [], [], [], [], []
    for h in heads:
        qh = act[:, h * dk:(h + 1) * dk]
        kh = act[:, GDN_QK_WIDTH + h * dk:GDN_QK_WIDTH + (h + 1) * dk]
        q.append(qh * lax.rsqrt(jnp.sum(qh * qh, axis=-1, keepdims=True) + EPS) * (dk ** -0.5))
        k.append(kh * lax.rsqrt(jnp.sum(kh * kh, axis=-1, keepdims=True) + EPS))
        v.append(act[:, 2 * GDN_QK_WIDTH + h * dv:2 * GDN_QK_WIDTH + (h + 1) * dv])
        beta.append(beta_all[:, SMALL_B0 + h:SMALL_B0 + h + 1])
        gc.append(gc_all[:, SMALL_A0 + h:SMALL_A0 + h + 1])
        gr = gc_t[SMALL_A0 + h:SMALL_A0 + h + 1, :]
        decay.append(jnp.exp(jnp.where(ii >= jj, gc[h] - gr, NEG)))
    kb = [k[h] * beta[h] for h in heads]
    ks = [_split(k[h]) for h in heads]
    qk = [_dot3s(_split(jnp.concatenate([q[h], kb[h]], axis=0)), ks[h], _NT) for h in heads]
    a_in = [qk[h][0:ck] * decay[h] for h in heads]
    lmat = [jnp.where(ii > jj, qk[h][ck:2 * ck] * decay[h], 0.0) for h in heads]
    x = [eye - lmat[h] for h in heads]
    pw = [_split(lmat[h]) for h in heads]
    for _ in range(int(math.log2(ck)) - 1):
        pw = [_split(_dot3s(pw[h], pw[h])) for h in heads]
        x = [x[h] + _dot3s(_split(x[h]), pw[h]) for h in heads]
    eg = [jnp.exp(gc[h]) for h in heads]
    uw = [_dot3s(_split(x[h]), _split(jnp.concatenate([v[h] * beta[h], kb[h] * eg[h]], axis=1))) for h in heads]
    s_old = [s_ref[h] for h in heads]
    qw_s = [_dot3s(_split(jnp.concatenate([q[h] * eg[h], uw[h][:, dv:dv + dk]], axis=0)), _split(s_old[h])) for h in heads]
    v_new = [uw[h][:, 0:dv] - qw_s[h][ck:2 * ck] for h in heads]
    vns = [_split(v_new[h]) for h in heads]
    o = [qw_s[h][0:ck] + _dot3s(_split(a_in[h]), vns[h]) for h in heads]
    outs = []
    for h in heads:
        g_last = gc[h][ck - 1:ck, :]
        kd = k[h] * jnp.exp(g_last - gc[h])
        s_ref[h] = s_old[h] * jnp.exp(g_last) + _dot3s(_split(kd.T), vns[h])
        outs.append(o[h] * lax.rsqrt(jnp.mean(o[h] * o[h], axis=-1, keepdims=True) + EPS) * gn_ref[...])
    o_ref[...] = jnp.concatenate(outs, axis=1) * szb_ref[...]

    @pl.when(c == pl.num_programs(1) - 1)
    def _():
        sfin_ref[...] = s_ref[...]


def _lane_params(a_log, dt_bias):
    alog_l = jnp.zeros((1, SMALL_W), F32).at[0, SMALL_A0:SMALL_A0 + GDN_HEADS].set(a_log)
    dtb_l = jnp.zeros((1, SMALL_W), F32).at[0, SMALL_A0:SMALL_A0 + GDN_HEADS].set(dt_bias)
    return alog_l, dtb_l


def _gdn_prompt(qkvb, small, szb, conv_w, conv_b, a_log, dt_bias, gnorm, batch):
    n = qkvb.shape[0]
    seq = n // batch
    nc = seq // GDN_CHUNK
    alog_l, dtb_l = _lane_params(a_log, dt_bias)
    row = lambda w: pl.BlockSpec((GDN_CHUNK, w), lambda b, c: (b * nc + c, 0))
    full = lambda a: pl.BlockSpec(a.shape, lambda b, c: (0,) * a.ndim)
    cb = conv_b.reshape(1, -1)
    gn = gnorm.reshape(1, -1)
    return pl.pallas_call(
        _gdn_prompt_kernel,
        grid=(batch, nc),
        in_specs=[row(CONV_DIM), row(SMALL_W), row(GDN_WIDTH), full(conv_w), full(cb), full(alog_l), full(dtb_l), full(gn)],
        out_specs=[row(GDN_WIDTH),
                   pl.BlockSpec((None, GDN_HEADS, GDN_DK, GDN_DV), lambda b, c: (b, 0, 0, 0))],
        out_shape=[jax.ShapeDtypeStruct((n, GDN_WIDTH), F32),
                   jax.ShapeDtypeStruct((batch, GDN_HEADS, GDN_DK, GDN_DV), F32)],
        scratch_shapes=[pltpu.VMEM((GDN_HEADS, GDN_DK, GDN_DV), F32), pltpu.VMEM((8, CONV_DIM), F32)],
        compiler_params=_cparams(("parallel", "arbitrary")),
        name="gdn_prompt",
    )(qkvb, small, szb, conv_w, cb, alog_l, dtb_l, gn)


def _prep_w_in(w_in, d_model):
    sizes = (NSA_WIDTH, 6 * NSA_KV_HEADS * HEAD_DIM, 3 * NSA_HEADS, NSA_WIDTH, CONV_DIM, GDN_HEADS, GDN_HEADS,
             GDN_WIDTH, 2 * d_model)
    pts = np.cumsum(np.array(sizes))[:-1].tolist()
    q_a, kv_a, g_a, z_a, qkv_b, a_b, b_b, z_b, gm = jnp.split(w_in, pts, axis=1)
    small = jnp.concatenate([g_a, a_b, b_b], axis=1)
    small = jnp.pad(small, ((0, 0), (0, SMALL_W - small.shape[1])))
    return jnp.concatenate([q_a, kv_a, z_a, qkv_b, z_b, gm, small], axis=1).astype(BF16)


def _prompt_path(x, ada, lw, cw):
    (norm_g, w_bf, offs, conv_w, conv_b, a_log, dt_bias, gnorm, wa, wb, wo, final_g) = lw
    batch, seq, d = x.shape
    n = batch * seq
    x2 = x.reshape(n, d)
    ada3 = ada.reshape(batch, 1, 3 * d)
    tm = 256
    mod = lambda k: pl.BlockSpec((None, 1, d), lambda i: (i * tm // seq, 0, k))
    (q_aug, kvc, kvs, kvw, kvcb, ksel, vsel_t, kwin, vwin_t, sza, qkvb, szb, gms, small) = _inproj(
        x2, ada3, ada3, (mod(1), mod(0)), norm_g.reshape(1, d), w_bf, offs, tm, seq)
    kc_aug, vc_t = _compress_prompt(kvcb, *cw, batch)
    o_a = _nsa_prompt(q_aug, kc_aug, vc_t, ksel, vsel_t, kwin, vwin_t, small, sza, batch)
    o_b, s_new = _gdn_prompt(qkvb, small, szb, conv_w, conv_b, a_log, dt_bias, gnorm, batch)
    tmo = 512
    gate_spec = pl.BlockSpec((None, 1, d), lambda i: (i * tmo // seq, 0, 2))
    y = _outproj(x2, o_a, o_b, gms, ada3, gate_spec, wa, wb, wo, final_g.reshape(1, d), tmo)
    kvshape = (1, batch, seq, 2, NSA_KV_HEADS, HEAD_DIM)
    keep = min(WINDOW, seq)
    new_win = kvw.reshape(kvshape)[:, :, seq - keep:]
    new_conv = qkvb.reshape(batch, seq, CONV_DIM)[None, :, seq - (CONV_W - 1):]
    return (y.reshape(batch, seq, d), kvc.reshape(kvshape), kvs.reshape(kvshape), new_win, new_conv, s_new[None])


def _page_fetch(pt_ref, cache_hbm, buf, sem, npages, rows_per_page):
    def copy(seq, slot, j):
        r0 = pl.multiple_of(j * rows_per_page, rows_per_page)
        return pltpu.make_async_copy(cache_hbm.at[pt_ref[seq, j]], buf.at[slot, pl.ds(r0, rows_per_page), :],
                                     sem.at[slot])

    def start(seq, slot):
        def body(j, c):
            copy(seq, slot, j).start()
            return c
        lax.fori_loop(0, npages, body, 0)

    def wait(seq, slot):
        def body(j, c):
            copy(seq, slot, j).wait()
            return c
        lax.fori_loop(0, npages, body, 0)

    return start, wait


def _fetch_this_prefetch_next(start, wait):
    b = pl.program_id(0)
    nb = pl.num_programs(0)

    @pl.when(b == 0)
    def _():
        start(0, 0)

    @pl.when(b + 1 < nb)
    def _():
        start(b + 1, (b + 1) % 2)

    slot = b % 2
    wait(b, slot)
    return slot


def _compress_sample_kernel(pt_ref, cache_hbm, w1_ref, pe_ref, w2_ref, o_ref, buf, sem, *, npages):
    start, wait = _page_fetch(pt_ref, cache_hbm, buf, sem, npages, PAGE_SIZE // CMP_STRIDE)
    slot = _fetch_this_prefetch_next(start, wait)
    o_ref[...] = _compress_math(buf[slot].astype(BF16), w1_ref, pe_ref, w2_ref).astype(BF16)


def _compress_sample(cache_cmp, page_table, w1big, pebig, w2big):
    nseq, npages = page_table.shape
    n_pool = cache_cmp.shape[0]
    rpp = PAGE_SIZE // CMP_STRIDE
    width = CMP_STRIDE * KV_ROW
    n16 = npages * rpp
    cache = cache_cmp.reshape(n_pool, rpp, width)
    grid_spec = pltpu.PrefetchScalarGridSpec(
        num_scalar_prefetch=1,
        grid=(nseq,),
        in_specs=[pl.BlockSpec(memory_space=pl.ANY),
                  pl.BlockSpec(w1big.shape, lambda b, pt: (0, 0, 0)),
                  pl.BlockSpec(pebig.shape, lambda b, pt: (0, 0, 0)),
                  pl.BlockSpec(w2big.shape, lambda b, pt: (0, 0))],
        out_specs=pl.BlockSpec((None, n16, KV_ROW), lambda b, pt: (b, 0, 0)),
        scratch_shapes=[pltpu.VMEM((2, n16, width), F32), pltpu.SemaphoreType.DMA((2,))],
    )
    return pl.pallas_call(
        functools.partial(_compress_sample_kernel, npages=npages),
        grid_spec=grid_spec,
        out_shape=jax.ShapeDtypeStruct((nseq, n16, KV_ROW), BF16),
        compiler_params=_cparams(("arbitrary",)),
        name="compress_sample",
    )(page_table, cache, w1big, pebig, w2big)


def _nsa_sample_kernel(pt_ref, q_ref, kc_ref, cache_hbm, win_ref, nsel_ref, nwin_ref, small_ref, sza_ref, selmap_ref,
                       o_ref, wout_ref, buf, sem, *, npages, tn, kt):
    hd = HEAD_DIM
    rt = 8
    start, wait = _page_fetch(pt_ref, cache_hbm, buf, sem, npages, PAGE_SIZE)
    slot = _fetch_this_prefetch_next(start, wait)
    past = npages * PAGE_SIZE
    nbuf = win_ref.shape[0]
    n_cmp = kc_ref.shape[0]
    n_blk_lanes = selmap_ref.shape[1]
    q8 = q_ref[...].astype(F32)
    gates = jax.nn.sigmoid(small_ref[...])
    tau = lax.broadcasted_iota(jnp.int32, (rt, 1), 0) % tn
    t_col = past + tau
    t4 = jnp.concatenate([t_col] * NSA_GROUP, axis=0)
    rows = NSA_GROUP * rt
    newcol = lax.broadcasted_iota(jnp.int32, (1, SMALL_W), 1)
    new_dist = t4 - (past + newcol)
    new_ok = (newcol < tn) & (new_dist >= 0)
    zpad = jnp.zeros((SMALL_W - rt, hd), F32)
    outs = []
    for g in range(NSA_KV_HEADS):
        kcol = slice(g * hd, (g + 1) * hd)
        vcol = slice((NSA_KV_HEADS + g) * hd, (NSA_KV_HEADS + g + 1) * hd)
        qg = jnp.concatenate([q8[:, (g * NSA_GROUP + h) * AUG:(g * NSA_GROUP + h) * AUG + hd]
                              for h in range(NSA_GROUP)], axis=0).astype(BF16)
        slope = jnp.concatenate([jnp.full((rt, 1), _SLOPES[g * NSA_GROUP + h], F32) for h in range(NSA_GROUP)], axis=0)

        def new_tile(ref, col):
            return jnp.concatenate([ref[:, col], zpad], axis=0).astype(BF16)

        s = _dot(qg, kc_ref[:, kcol], _NT)
        end = lax.broadcasted_iota(jnp.int32, (1, n_cmp), 1) * CMP_STRIDE + (CMP_BLOCK - 1)
        dist = t4 - end
        mask = dist >= 0
        _, p = _softmax_block(s - slope * dist.astype(F32), mask)
        p = p / jnp.maximum(jnp.sum(p, axis=-1, keepdims=True), 1e-30)
        o_c = _dot(p.astype(BF16), kc_ref[:, vcol])
        psum = p[0:rt]
        for h in range(1, NSA_GROUP):
            psum = psum + p[h * rt:(h + 1) * rt]
        imp = _importance(psum, selmap_ref[...])
        selm = _select_blocks(imp, t_col, n_blk_lanes)
        selm_bf = selm.astype(BF16)

        def sel_tile(j, carry):
            k0 = pl.multiple_of(j * kt, kt)
            kv = buf[slot, pl.ds(k0, kt), :]
            s = _dot(qg, kv[:, kcol].astype(BF16), _NT)
            dist = t4 - (k0 + lax.broadcasted_iota(jnp.int32, (1, kt), 1))
            mask = _expand_blocks(selm_bf, k0, kt, NSA_GROUP)
            return _online_update(carry, s - slope * dist.astype(F32), mask, kv[:, vcol].astype(BF16))

        carry = lax.fori_loop(0, past // kt, sel_tile, _online_init(rows, hd))
        nb_new = past // SEL_BLOCK
        new_sel = jnp.concatenate([selm[:, nb_new:nb_new + 1]] * NSA_GROUP, axis=0) > 0.5
        s = _dot(qg, new_tile(nsel_ref, kcol), _NT)
        _, l, acc = _online_update(carry, s - slope * new_dist.astype(F32), new_ok & new_sel, new_tile(nsel_ref, vcol))
        o_s = acc / jnp.maximum(l, 1e-30)

        kw = win_ref[...]
        s = _dot(qg, kw[:, kcol].astype(BF16), _NT)
        dist = t4 - (past - nbuf + lax.broadcasted_iota(jnp.int32, (1, nbuf), 1))
        mask = (dist >= 0) & (dist < WINDOW)
        carry = _online_update(_online_init(rows, hd), s - slope * dist.astype(F32), mask, kw[:, vcol].astype(BF16))
        s = _dot(qg, new_tile(nwin_ref, kcol), _NT)
        _, l, acc = _online_update(carry, s - slope * new_dist.astype(F32), new_ok, new_tile(nwin_ref, vcol))
        o_w = acc / jnp.maximum(l, 1e-30)

        for h in range(NSA_GROUP):
            c = 3 * (g * NSA_GROUP + h)
            r = slice(h * rt, (h + 1) * rt)
            outs.append(gates[:, c:c + 1] * o_c[r] + gates[:, c + 1:c + 2] * o_s[r] + gates[:, c + 2:c + 3] * o_w[r])
    o_ref[...] = jnp.concatenate(outs, axis=1) * sza_ref[...]
    wout_ref[0:nbuf - tn, :] = win_ref[tn:nbuf, :]
    wout_ref[nbuf - tn:nbuf, :] = nwin_ref[0:tn, :]


def _rep8(a, nseq, tn):
    a = a.reshape(nseq, tn, a.shape[-1])
    return jnp.concatenate([a] * (8 // tn), axis=1)


def _nsa_sample(qs, kc, cache_sel, cache_win, kvs, kvw, small, sza, page_table, tn):
    nseq, npages = page_table.shape
    past = npages * PAGE_SIZE
    n_pool = cache_sel.shape[0]
    nbuf = cache_win.shape[1]
    assert nbuf == WINDOW and 8 % tn == 0 and tn <= SEL_BLOCK
    n_cmp = kc.shape[1]
    n_blk = -(-(past + tn) // SEL_BLOCK)
    n_blk_lanes = -(-n_blk // 128) * 128
    selmap = _sel_map(n_cmp, n_blk, n_blk_lanes)
    kt = min(1024, past)
    seq3 = lambda r, w: pl.BlockSpec((None, r, w), lambda b, pt: (b, 0, 0))
    grid_spec = pltpu.PrefetchScalarGridSpec(
        num_scalar_prefetch=1,
        grid=(nseq,),
        in_specs=[seq3(8, NSA_HEADS * AUG), seq3(n_cmp, KV_ROW), pl.BlockSpec(memory_space=pl.ANY), seq3(nbuf, KV_ROW),
                  seq3(8, KV_ROW), seq3(8, KV_ROW), seq3(8, SMALL_W), seq3(8, NSA_WIDTH),
                  pl.BlockSpec(selmap.shape, lambda b, pt: (0, 0))],
        out_specs=[seq3(8, NSA_WIDTH), seq3(nbuf, KV_ROW)],
        scratch_shapes=[pltpu.VMEM((2, past, KV_ROW), F32), pltpu.SemaphoreType.DMA((2,))],
    )
    o8, win_new = pl.pallas_call(
        functools.partial(_nsa_sample_kernel, npages=npages, tn=tn, kt=kt),
        grid_spec=grid_spec,
        out_shape=[jax.ShapeDtypeStruct((nseq, 8, NSA_WIDTH), F32), jax.ShapeDtypeStruct((nseq, nbuf, KV_ROW), F32)],
        compiler_params=_cparams(("arbitrary",)),
        name="nsa_sample",
    )(page_table, _rep8(qs, nseq, tn), kc, cache_sel.reshape(n_pool, PAGE_SIZE, KV_ROW),
      cache_win.reshape(nseq, nbuf, KV_ROW), _rep8(kvs, nseq, tn), _rep8(kvw, nseq, tn), _rep8(small, nseq, tn),
      _rep8(sza, nseq, tn), selmap)
    return o8[:, :tn].reshape(nseq * tn, NSA_WIDTH), win_new


def _gdn_sample_kernel(eq_ref, ek_ref, ev_ref, cwq_ref, cwk_ref, cwv_ref, cbq_ref, cbk_ref, cbv_ref, small_ref,
                       alog_ref, dtb_ref, szb_ref, gn_ref, s_ref, o_ref, so_ref, qs_ref, ks_ref):
    h = pl.program_id(0)
    tn = o_ref.shape[0]
    dk, dv, nseq = so_ref.shape

    def conv(e_ref, cw_ref, cb_ref, t):
        y = cb_ref[...]
        for i in range(CONV_W):
            y = y + e_ref[t + i] * cw_ref[:, i:i + 1]
        return _silu(y)

    so_ref[...] = s_ref[...]
    neg_rate = -jnp.exp(alog_ref[...])
    for t in range(tn):
        q = conv(eq_ref, cwq_ref, cbq_ref, t)
        k = conv(ek_ref, cwk_ref, cbk_ref, t)
        v = conv(ev_ref, cwv_ref, cbv_ref, t)
        qs_ref[...] = q * lax.rsqrt(jnp.sum(q * q, axis=0, keepdims=True) + EPS) * (dk ** -0.5)
        ks_ref[...] = k * lax.rsqrt(jnp.sum(k * k, axis=0, keepdims=True) + EPS)
        a_in = small_ref[t, pl.ds(SMALL_A0 + h, 1), :]
        b_in = small_ref[t, pl.ds(SMALL_B0 + h, 1), :]
        decay = jnp.exp(neg_rate * _softplus(a_in + dtb_ref[...]))
        beta = jax.nn.sigmoid(b_in)

        def ks_step(i, acc):
            return acc + ks_ref[pl.ds(i, 1), :] * so_ref[i]

        k_s = lax.fori_loop(0, dk, ks_step, jnp.zeros((dv, nseq), F32), unroll=8)
        delta = beta * (v - decay * k_s)

        def upd_step(i, acc):
            s_new = decay * so_ref[i] + ks_ref[pl.ds(i, 1), :] * delta
            so_ref[i] = s_new
            return acc + qs_ref[pl.ds(i, 1), :] * s_new

        o = lax.fori_loop(0, dk, upd_step, jnp.zeros((dv, nseq), F32), unroll=8)
        o = o * lax.rsqrt(jnp.mean(o * o, axis=0, keepdims=True) + EPS) * gn_ref[...]
        o_ref[t] = o * szb_ref[t]


def _gdn_sample(qkvb, small, szb, state_conv, state_gdn, conv_w, conv_b, a_log, dt_bias, gnorm, tn):
    nseq = state_gdn.shape[0]
    ext = jnp.concatenate([state_conv, qkvb.reshape(nseq, tn, CONV_DIM)], axis=1)
    ext_t = ext.transpose(1, 2, 0)
    small_t = small.reshape(nseq, tn, SMALL_W).transpose(1, 2, 0)
    szb_t = szb.reshape(nseq, tn, GDN_WIDTH).transpose(1, 2, 0)
    s_t = state_gdn.transpose(1, 2, 3, 0)
    cw_t = conv_w.T
    cb_t = conv_b.reshape(-1, 1)
    alog_b = jnp.broadcast_to(a_log[:, None, None], (GDN_HEADS, 1, nseq))
    dtb_b = jnp.broadcast_to(dt_bias[:, None, None], (GDN_HEADS, 1, nseq))
    gn = gnorm.reshape(-1, 1)
    nqk = GDN_QK_WIDTH // GDN_DK
    chan = lambda off: pl.BlockSpec((CONV_W - 1 + tn, GDN_DK, nseq), lambda h: (0, off + h, 0))
    cwb = lambda off: pl.BlockSpec((GDN_DK, CONV_W), lambda h: (off + h, 0))
    cbb = lambda off: pl.BlockSpec((GDN_DK, 1), lambda h: (off + h, 0))
    perhead = pl.BlockSpec((None, 1, nseq), lambda h: (h, 0, 0))
    o_t, s_new = pl.pallas_call(
        _gdn_sample_kernel,
        grid=(GDN_HEADS,),
        in_specs=[chan(0), chan(nqk), chan(2 * nqk), cwb(0), cwb(nqk), cwb(2 * nqk), cbb(0), cbb(nqk), cbb(2 * nqk),
                  pl.BlockSpec((tn, SMALL_W, nseq), lambda h: (0, 0, 0)), perhead, perhead,
                  pl.BlockSpec((tn, GDN_DV, nseq), lambda h: (0, h, 0)),
                  pl.BlockSpec((GDN_DV, 1), lambda h: (0, 0)),
                  pl.BlockSpec((None, GDN_DK, GDN_DV, nseq), lambda h: (h, 0, 0, 0))],
        out_specs=[pl.BlockSpec((tn, GDN_DV, nseq), lambda h: (0, h, 0)),
                   pl.BlockSpec((None, GDN_DK, GDN_DV, nseq), lambda h: (h, 0, 0, 0))],
        out_shape=[jax.ShapeDtypeStruct((tn, GDN_WIDTH, nseq), F32),
                   jax.ShapeDtypeStruct((GDN_HEADS, GDN_DK, GDN_DV, nseq), F32)],
        scratch_shapes=[pltpu.VMEM((GDN_DK, nseq), F32), pltpu.VMEM((GDN_DK, nseq), F32)],
        compiler_params=_cparams(("parallel",)),
        name="gdn_sample",
    )(ext_t, ext_t, ext_t, cw_t, cw_t, cw_t, cb_t, cb_t, cb_t, small_t, alog_b, dtb_b, szb_t, gn, s_t)
    o_b = o_t.transpose(2, 0, 1).reshape(nseq * tn, GDN_WIDTH)
    return o_b, ext[:, tn:], s_new.transpose(3, 0, 1, 2)


def _sample_path(x, ada, lw, cw, cache_cmp, cache_sel, cache_win, state_conv, state_gdn, page_table):
    (norm_g, w_bf, offs, conv_w, conv_b, a_log, dt_bias, gnorm, wa, wb, wo, final_g) = lw
    nseq, tn, d = x.shape
    n = nseq * tn
    x2 = x.reshape(n, d)
    ada_rows = jnp.repeat(ada, tn, axis=0)
    tm = min(256, n)
    mod = lambda k: pl.BlockSpec((tm, d), lambda i: (i, k))
    (qs, kvc, kvs, kvw, _, _, _, _, _, sza, qkvb, szb, gms, small) = _inproj(
        x2, ada_rows, ada_rows, (mod(1), mod(0)), norm_g.reshape(1, d), w_bf, offs, tm, n)
    kc = _compress_sample(cache_cmp, page_table, *cw)
    o_a, win_new = _nsa_sample(qs, kc, cache_sel, cache_win, kvs, kvw, small, sza, page_table, tn)
    o_b, conv_new, s_new = _gdn_sample(qkvb, small, szb, state_conv, state_gdn, conv_w, conv_b, a_log, dt_bias, gnorm, tn)
    y = _outproj(x2, o_a, o_b, gms, ada_rows, mod(2), wa, wb, wo, final_g.reshape(1, d), tm)
    kvshape = (1, nseq, tn, 2, NSA_KV_HEADS, HEAD_DIM)
    win_shape = (1, nseq, win_new.shape[1], 2, NSA_KV_HEADS, HEAD_DIM)
    return (y.reshape(nseq, tn, d), kvc.reshape(kvshape), kvs.reshape(kvshape), win_new.reshape(win_shape),
            conv_new[None], s_new[None])


def kernel(x_prompt, x_sample, cache_cmp_kv, cache_sel_kv, cache_win_kv, state_conv, state_gdn, page_table, c_prompt, c_sample, norm_g, w_ada, b_ada, w_in, cmp_pe_k, cmp_w1_k, cmp_w2_k, cmp_pe_v, cmp_w1_v, cmp_w2_v, conv_w, conv_b, gdn_a_log, gdn_dt_bias, gdn_norm_g, w_o_nsa, w_o_gdn, w_out, final_g):
    assert norm_g.shape[0] == 1, "single trunk layer"
    d = x_prompt.shape[-1]
    l = 0
    offs, _ = _seg_offsets(d)
    w_bf = _prep_w_in(w_in[l], d)
    ada = _ada(jnp.concatenate([c_prompt, c_sample], axis=0), w_ada[l].astype(BF16), b_ada[l])
    cw = _compress_weights(cmp_pe_k[l], cmp_w1_k[l], cmp_w2_k[l], cmp_pe_v[l], cmp_w1_v[l], cmp_w2_v[l])
    lw = (norm_g[l], w_bf, offs, conv_w[l], conv_b[l], gdn_a_log[l], gdn_dt_bias[l], gdn_norm_g[l],
          w_o_nsa[l].astype(BF16), w_o_gdn[l].astype(BF16), w_out[l].astype(BF16), final_g)
    nb = c_prompt.shape[0]
    yp, cmp_p, sel_p, win_p, conv_p, gdn_p = _prompt_path(x_prompt, ada[:nb], lw, cw)
    ys, cmp_s, sel_s, win_s, conv_s, gdn_s = _sample_path(
        x_sample, ada[nb:], lw, cw, cache_cmp_kv[l], cache_sel_kv[l], cache_win_kv[l], state_conv[l], state_gdn[l],
        page_table)
    return (yp, ys, cmp_p, sel_p, win_p, conv_p, gdn_p, cmp_s, sel_s, win_s, conv_s, gdn_s)
```

```python
import functools
import math

import numpy as np
import jax
import jax.numpy as jnp
from jax import lax
from jax.experimental import pallas as pl
from jax.experimental.pallas import tpu as pltpu

F32 = jnp.float32
BF16 = jnp.bfloat16

NSA_HEADS = 8
NSA_KV_HEADS = 2
NSA_GROUP = NSA_HEADS // NSA_KV_HEADS
HEAD_DIM = 64
CMP_BLOCK = 32
CMP_STRIDE = 16
CMP_HIDDEN = 64
SEL_BLOCK = 64
N_SEL = 16
WINDOW = 512
Q_BLOCK = 128
GDN_HEADS = 8
GDN_DK = 64
GDN_DV = 64
CONV_W = 4
GDN_CHUNK = 64
PAGE_SIZE = 128
EPS = 1e-6

NSA_WIDTH = NSA_HEADS * HEAD_DIM
KV_ROW = 2 * NSA_KV_HEADS * HEAD_DIM
GDN_QK_WIDTH = GDN_HEADS * GDN_DK
GDN_WIDTH = GDN_HEADS * GDN_DV
CONV_DIM = 2 * GDN_QK_WIDTH + GDN_WIDTH
SMALL_W = 128
SMALL_A0 = 3 * NSA_HEADS
SMALL_B0 = SMALL_A0 + GDN_HEADS

NEG = -1e30
MASKV = -(2.0 ** 100)
AUG = 128
POS_SPLIT = 128
VMEM_LIMIT = 56 * 1024 * 1024

_NT = (((1,), (1,)), ((), ()))
_NN = (((1,), (0,)), ((), ()))


def _alibi_slopes():
    h = np.arange(1, NSA_HEADS + 1, dtype=np.float32)
    return [float(v) for v in np.power(np.float32(2.0), -np.float32(8.0) * h / np.float32(NSA_HEADS))]


_SLOPES = _alibi_slopes()


def _dot(a, b, dims=_NN):
    return lax.dot_general(a, b, dims, preferred_element_type=F32)


def _split(a):
    hi = a.astype(BF16)
    lo = (a - hi.astype(F32)).astype(BF16)
    return hi, lo


def _dot3s(a_split, b_split, dims=_NN):
    (ah, al), (bh, bl) = a_split, b_split
    return _dot(ah, bh, dims) + _dot(ah, bl, dims) + _dot(al, bh, dims)


def _silu(x):
    return x * jax.nn.sigmoid(x)


def _cparams(sem):
    return pltpu.CompilerParams(dimension_semantics=sem, vmem_limit_bytes=VMEM_LIMIT)


def _ada_kernel(c_ref, w_ref, b_ref, o_ref):
    sc = _silu(c_ref[...]).astype(BF16)
    o_ref[...] = _dot(sc, w_ref[...]) + b_ref[...]


def _ada(c, w_bf, b):
    n, d = c.shape
    n_pad = -(-n // 8) * 8
    c = jnp.pad(c, ((0, n_pad - n), (0, 0)))
    out = pl.pallas_call(
        _ada_kernel,
        out_shape=jax.ShapeDtypeStruct((n_pad, w_bf.shape[1]), F32),
        name="ada",
    )(c, w_bf, b.reshape(1, -1))
    return out[:n]


_SEG = (("q", NSA_WIDTH), ("kvc", KV_ROW), ("kvs", KV_ROW), ("kvw", KV_ROW), ("za", NSA_WIDTH),
        ("qkvb", CONV_DIM), ("zb", GDN_WIDTH), ("gm", None), ("small", SMALL_W))


def _seg_offsets(d_model):
    offs, c = {}, 0
    for name, n in _SEG:
        n = 2 * d_model if n is None else n
        offs[name] = (c, n)
        c += n
    return offs, c


def _aug_cols(rows, c0, c1):
    lane = lax.broadcasted_iota(jnp.int32, (rows, AUG - HEAD_DIM), 1)
    return jnp.where(lane == 0, c0, jnp.where(lane == 1, c1, 0.0)).astype(F32)


def _aug_keys(kv, pos):
    hi = jnp.floor(pos * (1.0 / POS_SPLIT))
    cols = _aug_cols(kv.shape[0], hi, pos - hi * POS_SPLIT)
    parts = []
    for g in range(NSA_KV_HEADS):
        parts += [kv[:, g * HEAD_DIM:(g + 1) * HEAD_DIM], cols]
    k_aug = jnp.concatenate(parts, axis=1).astype(BF16)
    kv_t = kv.T
    return k_aug, kv_t[NSA_KV_HEADS * HEAD_DIM:].astype(BF16), kv_t


def _inproj_kernel(x_ref, scale_ref, shift_ref, ng_ref, w_ref,
                   q_ref, kvct_ref, kvst_ref, kvwt_ref, kvcb_ref, ksel_ref, vselt_ref, kwin_ref, vwint_ref,
                   sza_ref, qkvb_ref, szb_ref, gms_ref, small_ref, *, offs, seq):
    x = x_ref[...]
    tm = x.shape[0]
    y = x * lax.rsqrt(jnp.mean(x * x, axis=-1, keepdims=True) + EPS) * ng_ref[...]
    h = y * (1.0 + scale_ref[...]) + shift_ref[...]
    hb = h.astype(BF16)

    def seg(name):
        c0, n = offs[name]
        return _dot(hb, w_ref[:, c0:c0 + n])

    qv = seg("q") * (HEAD_DIM ** -0.5)
    parts = []
    for hh in range(NSA_HEADS):
        parts += [qv[:, hh * HEAD_DIM:(hh + 1) * HEAD_DIM], _aug_cols(tm, _SLOPES[hh] * POS_SPLIT, _SLOPES[hh])]
    q_ref[...] = jnp.concatenate(parts, axis=1).astype(BF16)

    pos = ((pl.program_id(0) * tm) % seq + lax.broadcasted_iota(jnp.int32, (tm, 1), 0)).astype(F32)
    v = seg("kvc")
    kvct_ref[...] = v.T
    kvcb_ref[...] = v.astype(BF16)
    for name, f_ref, k_ref, vt_ref in (("kvs", kvst_ref, ksel_ref, vselt_ref), ("kvw", kvwt_ref, kwin_ref, vwint_ref)):
        k_ref[...], vt_ref[...], f_ref[...] = _aug_keys(seg(name), pos)
    sza_ref[...] = _silu(seg("za"))
    qkvb_ref[...] = seg("qkvb")
    szb_ref[...] = _silu(seg("zb"))
    gms_ref[...] = jax.nn.sigmoid(seg("gm"))
    small_ref[...] = seg("small")


def _inproj(x2, scale_arr, shift_arr, mod_specs, ng, w_bf, offs, tm, seq):
    n, d = x2.shape
    wtot = w_bf.shape[1]
    nbatch = n // seq
    spt = seq // tm
    vw = NSA_KV_HEADS * HEAD_DIM
    row = lambda w: pl.BlockSpec((tm, w), lambda i: (i, 0))
    rows = lambda w, dt: (jax.ShapeDtypeStruct((n, w), dt), row(w))
    tr = lambda w, dt: (jax.ShapeDtypeStruct((nbatch, w, seq), dt),
                        pl.BlockSpec((None, w, tm), lambda i: (i // spt, 0, i % spt)))
    vt = tr(vw, BF16)
    kvt = tr(KV_ROW, F32)
    outs = [rows(NSA_HEADS * AUG, BF16), kvt, kvt, kvt, rows(KV_ROW, BF16),
            rows(NSA_KV_HEADS * AUG, BF16), vt, rows(NSA_KV_HEADS * AUG, BF16), vt,
            rows(offs["za"][1], F32), rows(offs["qkvb"][1], F32), rows(offs["zb"][1], F32), rows(offs["gm"][1], F32),
            rows(offs["small"][1], F32)]
    out_shape = [o[0] for o in outs]
    out_specs = [o[1] for o in outs]
    return pl.pallas_call(
        functools.partial(_inproj_kernel, offs=offs, seq=seq),
        grid=(n // tm,),
        in_specs=[row(d), mod_specs[0], mod_specs[1],
                  pl.BlockSpec((1, d), lambda i: (0, 0)),
                  pl.BlockSpec((d, wtot), lambda i: (0, 0))],
        out_specs=out_specs,
        out_shape=out_shape,
        compiler_params=_cparams(("parallel",)),
        name="inproj",
    )(x2, scale_arr, shift_arr, ng, w_bf)


def _outproj_kernel(x_ref, oa_ref, ob_ref, gms_ref, gate_ref, wa_ref, wb_ref, wo_ref, fg_ref, y_ref, *, d):
    ma = _dot(oa_ref[...].astype(BF16), wa_ref[...])
    mb = _dot(ob_ref[...].astype(BF16), wb_ref[...])
    m = gms_ref[:, 0:d] * ma + gms_ref[:, d:2 * d] * mb
    y = x_ref[...] + gate_ref[...] * _dot(m.astype(BF16), wo_ref[...])
    y_ref[...] = y * lax.rsqrt(jnp.mean(y * y, axis=-1, keepdims=True) + EPS) * fg_ref[...]


def _outproj(x2, oa, ob, gms, gate_arr, gate_spec, wa, wb, wo, fg, tm):
    n, d = x2.shape
    row = lambda w: pl.BlockSpec((tm, w), lambda i: (i, 0))
    full = lambda a: pl.BlockSpec(a.shape, lambda i: (0, 0))
    return pl.pallas_call(
        functools.partial(_outproj_kernel, d=d),
        grid=(n // tm,),
        in_specs=[row(d), row(oa.shape[1]), row(ob.shape[1]), row(2 * d), gate_spec,
                  full(wa), full(wb), full(wo), full(fg)],
        out_specs=row(d),
        out_shape=jax.ShapeDtypeStruct((n, d), F32),
        compiler_params=_cparams(("parallel",)),
        name="outproj",
    )(x2, oa, ob, gms, gate_arr, wa, wb, wo, fg)


def _compress_math(x, w1_ref, pe_ref, w2_ref):
    n16 = x.shape[0]
    a0 = _dot(x, w1_ref[0])
    a1 = _dot(x, w1_ref[1])
    p0 = _dot(pe_ref[0], w1_ref[0]) + _dot(pe_ref[1], w1_ref[1])
    pre = a0 + pltpu.roll(a1, n16 - 1, 0) + p0[0:1]
    return _dot(_silu(pre).astype(BF16), w2_ref[...])


def _compress_kernel(x_ref, w1_ref, pe_ref, w2_ref, k_ref, vt_ref):
    kv = _compress_math(x_ref[...], w1_ref, pe_ref, w2_ref)
    n16 = kv.shape[0]
    end = (lax.broadcasted_iota(jnp.int32, (n16, 1), 0) * CMP_STRIDE + (CMP_BLOCK - 1)).astype(F32)
    k_ref[...], vt_ref[...], _ = _aug_keys(kv, end)


def _compress_prompt(kvcb, w1big, pebig, w2big, batch):
    n = kvcb.shape[0]
    n16 = n // batch // CMP_STRIDE
    x = kvcb.reshape(batch, n16, CMP_STRIDE * KV_ROW)
    vw = NSA_KV_HEADS * HEAD_DIM
    return pl.pallas_call(
        _compress_kernel,
        grid=(batch,),
        in_specs=[pl.BlockSpec((None, n16, CMP_STRIDE * KV_ROW), lambda b: (b, 0, 0)),
                  pl.BlockSpec(w1big.shape, lambda b: (0, 0, 0)),
                  pl.BlockSpec(pebig.shape, lambda b: (0, 0, 0)),
                  pl.BlockSpec(w2big.shape, lambda b: (0, 0))],
        out_specs=[pl.BlockSpec((None, n16, NSA_KV_HEADS * AUG), lambda b: (b, 0, 0)),
                   pl.BlockSpec((None, vw, n16), lambda b: (b, 0, 0))],
        out_shape=[jax.ShapeDtypeStruct((batch, n16, NSA_KV_HEADS * AUG), BF16),
                   jax.ShapeDtypeStruct((batch, vw, n16), BF16)],
        compiler_params=_cparams(("parallel",)),
        name="compress_prompt",
    )(x, w1big, pebig, w2big)


def _compress_weights(pe_k, w1_k, w2_k, pe_v, w1_v, w2_v):
    r_cnt = CMP_BLOCK // CMP_STRIDE
    g = NSA_KV_HEADS
    eye2 = jnp.eye(2, dtype=F32)
    eyeg = jnp.eye(g, dtype=F32)
    w1 = jnp.stack([w1_k, w1_v]).reshape(2, r_cnt, CMP_STRIDE, HEAD_DIM, CMP_HIDDEN)
    w1big = jnp.einsum("krsde,kK,gG->rskgdKGe", w1, eye2, eyeg)
    w1big = w1big.reshape(r_cnt, CMP_STRIDE * KV_ROW, 2 * g * CMP_HIDDEN).astype(BF16)
    pe = jnp.stack([pe_k, pe_v]).reshape(2, r_cnt, CMP_STRIDE, HEAD_DIM)
    pebig = jnp.broadcast_to(pe.transpose(1, 2, 0, 3)[:, :, :, None, :], (r_cnt, CMP_STRIDE, 2, g, HEAD_DIM))
    pebig = pebig.reshape(r_cnt, 1, CMP_STRIDE * KV_ROW)
    pebig = jnp.pad(pebig, ((0, 0), (0, 7), (0, 0))).astype(BF16)
    w2 = jnp.stack([w2_k, w2_v])
    w2big = jnp.einsum("ked,kK,gG->kgeKGd", w2, eye2, eyeg).reshape(2 * g * CMP_HIDDEN, KV_ROW).astype(BF16)
    return w1big, pebig, w2big


def _sel_map(n_cmp_rows, n_blk, n_blk_pad):
    i = np.arange(n_cmp_rows)[:, None] * CMP_STRIDE
    j = np.arange(n_blk_pad)[None, :] * SEL_BLOCK
    ov = np.minimum(i + CMP_BLOCK, j + SEL_BLOCK) - np.maximum(i, j)
    m = np.clip(ov, 0, None).astype(np.float32) / np.float32(CMP_BLOCK)
    m[:, n_blk:] = 0.0
    return jnp.asarray(m, dtype=BF16)


def _softmax_block(s, mask):
    s = jnp.where(mask, s, NEG)
    m = jnp.max(s, axis=-1, keepdims=True)
    p = jnp.where(mask, jnp.exp(s - m), 0.0)
    return m, p


def _online_update(carry, s, mask, v_bf, v_dims=_NN):
    m, l, acc = carry
    s = jnp.where(mask, s, NEG)
    m_new = jnp.maximum(m, jnp.max(s, axis=-1, keepdims=True))
    alpha = jnp.exp(m - m_new)
    p = jnp.where(mask, jnp.exp(s - m_new), 0.0)
    l = alpha * l + jnp.sum(p, axis=-1, keepdims=True)
    acc = alpha * acc + _dot(p.astype(BF16), v_bf, v_dims)
    return m_new, l, acc


def _flash_step(carry, s, v_bf, v_dims=_NN):
    m, l, acc = carry
    m_new = jnp.maximum(m, jnp.max(s, axis=-1, keepdims=True))
    alpha = jnp.exp(m - m_new)
    p = jnp.exp(s - m_new)
    return (m_new, alpha * l + jnp.sum(p, axis=-1, keepdims=True),
            alpha * acc + _dot(p.astype(BF16), v_bf, v_dims))


def _online_init(rows, hd):
    return (jnp.full((rows, 1), NEG, F32), jnp.zeros((rows, 1), F32), jnp.zeros((rows, hd), F32))


def _importance(p_sum, selmap):
    ph, pl_ = _split(p_sum)
    pl2 = (p_sum - ph.astype(F32) - pl_.astype(F32)).astype(BF16)
    return _dot(ph, selmap) + _dot(pl_, selmap) + _dot(pl2, selmap)


def _select_blocks(imp, t_col, n_blk_lanes):
    rows = imp.shape[0]
    jj = lax.broadcasted_iota(jnp.int32, (rows, n_blk_lanes), 1)
    cur = t_col // SEL_BLOCK
    forced = (jj == 0) | (jj == cur) | (jj == cur - 1)
    jf = jj.astype(F32)
    score = jnp.where(forced, NEG, jnp.where(jj <= cur, imp, NEG))
    selm = jnp.where(forced, 1.0, 0.0).astype(F32)
    for _ in range(N_SEL - 3):
        m = jnp.max(score, axis=-1, keepdims=True)
        idx = jnp.min(jnp.where(score == m, jf, float(n_blk_lanes)), axis=-1, keepdims=True)
        hit = jnp.where(m > 0.5 * NEG, idx, -1.0) == jf
        selm = jnp.where(hit, 1.0, selm)
        score = jnp.where(hit, NEG, score)
    return selm


def _expand_blocks(selm_bf, k0, kt, reps):
    n_blk = selm_bf.shape[1]
    blk = lax.broadcasted_iota(jnp.int32, (n_blk, kt), 0)
    key = lax.broadcasted_iota(jnp.int32, (n_blk, kt), 1)
    e = jnp.where(blk == k0 // SEL_BLOCK + key // SEL_BLOCK, 1.0, 0.0).astype(BF16)
    mk = _dot(selm_bf, e)
    return jnp.concatenate([mk] * reps, axis=0) > 0.5


def _flash_step_t(carry, s_t, v_t):
    m, l, acc = carry
    m_new = jnp.maximum(m, jnp.max(s_t, axis=0, keepdims=True))
    alpha = jnp.exp(m - m_new)
    p = jnp.exp(s_t - m_new)
    l = alpha * l + jnp.sum(p, axis=0, keepdims=True)
    acc = alpha * acc + _dot(v_t, p.astype(BF16))
    return m_new, l, acc


def _flash_init_t(cols, dv):
    return (jnp.full((1, cols), NEG, F32), jnp.zeros((1, cols), F32), jnp.zeros((dv, cols), F32))


def _select_blocks_t(imp_t, tq):
    nb, nq = imp_t.shape
    jj = lax.broadcasted_iota(jnp.int32, (nb, nq), 0)
    cur = tq // SEL_BLOCK
    forced = (jj == 0) | (jj == cur) | (jj == cur - 1)
    jf = jj.astype(F32)
    score = jnp.where(forced, NEG, jnp.where(jj <= cur, imp_t, NEG))
    selm = jnp.where(forced, 1.0, 0.0).astype(F32)
    for _ in range(N_SEL - 3):
        m = jnp.max(score, axis=0, keepdims=True)
        idx = jnp.min(jnp.where(score == m, jf, float(nb)), axis=0, keepdims=True)
        hit = jnp.where(m > 0.5 * NEG, idx, -1.0) == jf
        selm = jnp.where(hit, 1.0, selm)
        score = jnp.where(hit, NEG, score)
    return selm


def _nsa_prompt_kernel(q_ref, kc_ref, vct_ref, ksel_ref, vselt_ref, kwin_ref, vwint_ref, onehot_ref, small_ref, sza_ref,
                       selmapt_ref, o_ref, *, kt):
    qb = Q_BLOCK
    hd = HEAD_DIM
    q0 = pl.program_id(1) * qb
    n_cmp = kc_ref.shape[0]
    cols = NSA_GROUP * qb
    tq = q0 + lax.broadcasted_iota(jnp.int32, (1, qb), 1)
    gates_t = jax.nn.sigmoid(small_ref[...]).T
    wlen = WINDOW + qb
    w0 = pl.multiple_of(jnp.maximum(q0 - WINDOW, 0), qb)
    tile4 = lambda a: jnp.concatenate([a] * NSA_GROUP, axis=1)

    def keymask(pos_col, lo):
        dist = tq - pos_col
        ok = (dist >= 0) if lo is None else ((dist >= 0) & (dist < lo))
        return tile4(jnp.where(ok, 0.0, MASKV).astype(F32))

    cmp_bias = keymask(lax.broadcasted_iota(jnp.int32, (n_cmp, 1), 0) * CMP_STRIDE + (CMP_BLOCK - 1), None)
    win_bias = keymask(w0 + lax.broadcasted_iota(jnp.int32, (wlen, 1), 0), WINDOW)
    jlast = q0 // kt
    klast = pl.multiple_of(jlast * kt, kt)
    diag_bias = keymask(klast + lax.broadcasted_iota(jnp.int32, (kt, 1), 0), None)

    groups = range(NSA_KV_HEADS)
    kcols = [slice(g * AUG, (g + 1) * AUG) for g in groups]
    vrows = [slice(g * hd, (g + 1) * hd) for g in groups]
    q_heads = [[q_ref[:, (g * NSA_GROUP + h) * AUG:(g * NSA_GROUP + h + 1) * AUG] for h in range(NSA_GROUP)]
               for g in groups]
    q_cat = [jnp.concatenate(q_heads[g], axis=0) for g in groups]
    finish = lambda carry: carry[2] * (1.0 / jnp.maximum(carry[1], 1e-30))

    o_w = [finish(_flash_step_t(_flash_init_t(cols, hd), _dot(kwin_ref[pl.ds(w0, wlen), kcols[g]], q_cat[g], _NT) + win_bias,
                                vwint_ref[vrows[g], pl.ds(w0, wlen)])) for g in groups]

    o_c, q_full = [], []
    sm_t = selmapt_ref[...]
    for g in groups:
        s_t = _dot(kc_ref[:, kcols[g]], q_cat[g], _NT) + cmp_bias
        m = jnp.maximum(jnp.max(s_t, axis=0, keepdims=True), NEG)
        p = jnp.exp(s_t - m)
        p = p * (1.0 / jnp.maximum(jnp.sum(p, axis=0, keepdims=True), 1e-30))
        o_c.append(_dot(vct_ref[vrows[g], :], p.astype(BF16)))
        psum = p[:, 0:qb]
        for h in range(1, NSA_GROUP):
            psum = psum + p[:, h * qb:(h + 1) * qb]
        ph, pl_ = _split(psum)
        pl2 = (psum - ph.astype(F32) - pl_.astype(F32)).astype(BF16)
        imp_t = _dot(sm_t, ph) + _dot(sm_t, pl_) + _dot(sm_t, pl2)
        selm = _select_blocks_t(imp_t, tq).T
        blk_bias = jnp.where(selm > 0.5, 0.0, MASKV).astype(BF16)
        q_full.append(jnp.concatenate([jnp.concatenate([qh, blk_bias], axis=1) for qh in q_heads[g]], axis=0))

    def sel_step(k0, carries, bias):
        onehot = onehot_ref[pl.ds(k0, kt), :]
        out = []
        for g in groups:
            k_op = jnp.concatenate([ksel_ref[pl.ds(k0, kt), kcols[g]], onehot], axis=1)
            s_t = _dot(k_op, q_full[g], _NT)
            s_t = s_t if bias is None else s_t + bias
            out.append(_flash_step_t(carries[g], s_t, vselt_ref[vrows[g], pl.ds(k0, kt)]))
        return tuple(out)

    carries = lax.fori_loop(0, jlast, lambda j, c: sel_step(pl.multiple_of(j * kt, kt), c, None),
                            tuple(_flash_init_t(cols, hd) for _ in groups))
    o_s = [finish(c) for c in sel_step(klast, carries, diag_bias)]

    outs = []
    for g in groups:
        for h in range(NSA_GROUP):
            c = 3 * (g * NSA_GROUP + h)
            r = slice(h * qb, (h + 1) * qb)
            outs.append(gates_t[c:c + 1, :] * o_c[g][:, r] + gates_t[c + 1:c + 2, :] * o_s[g][:, r]
                        + gates_t[c + 2:c + 3, :] * o_w[g][:, r])
    o_ref[...] = jnp.concatenate(outs, axis=0).T * sza_ref[...]


def _nsa_prompt(q_aug, kc_aug, vc_t, ksel, vsel_t, kwin, vwin_t, small, sza, batch):
    n = q_aug.shape[0]
    seq = n // batch
    nq = seq // Q_BLOCK
    n_cmp = kc_aug.shape[1]
    n_blk = -(-seq // SEL_BLOCK)
    assert n_blk <= AUG and seq % Q_BLOCK == 0 and seq >= WINDOW + Q_BLOCK
    kt = min(512, seq)
    selmap_t = _sel_map(n_cmp, n_blk, AUG).T
    pos = np.arange(seq)[:, None] // SEL_BLOCK
    onehot = jnp.asarray(pos == np.arange(AUG)[None, :], dtype=BF16)
    vw = NSA_KV_HEADS * HEAD_DIM
    kw = NSA_KV_HEADS * AUG
    row = lambda w: pl.BlockSpec((Q_BLOCK, w), lambda b, i: (b * nq + i, 0))
    kspec = pl.BlockSpec((None, seq, kw), lambda b, i: (b, 0, 0))
    vspec = pl.BlockSpec((None, vw, seq), lambda b, i: (b, 0, 0))
    return pl.pallas_call(
        functools.partial(_nsa_prompt_kernel, kt=kt),
        grid=(batch, nq),
        in_specs=[row(NSA_HEADS * AUG),
                  pl.BlockSpec((None, n_cmp, kw), lambda b, i: (b, 0, 0)),
                  pl.BlockSpec((None, vw, n_cmp), lambda b, i: (b, 0, 0)),
                  kspec, vspec, kspec, vspec,
                  pl.BlockSpec(onehot.shape, lambda b, i: (0, 0)),
                  row(SMALL_W), row(NSA_WIDTH),
                  pl.BlockSpec(selmap_t.shape, lambda b, i: (0, 0))],
        out_specs=row(NSA_WIDTH),
        out_shape=jax.ShapeDtypeStruct((n, NSA_WIDTH), F32),
        compiler_params=_cparams(("parallel", "arbitrary")),
        name="nsa_prompt",
    )(q_aug, kc_aug, vc_t, ksel.reshape(batch, seq, kw), vsel_t, kwin.reshape(batch, seq, kw), vwin_t, onehot, small, sza,
      selmap_t)


def _softplus(x):
    return jnp.maximum(x, 0.0) + jnp.log1p(jnp.exp(-jnp.abs(x)))


def _gdn_prompt_kernel(qkv_ref, small_ref, szb_ref, cw_ref, cb_ref, alog_ref, dtb_ref, gn_ref,
                       o_ref, sfin_ref, s_ref, prev_ref):
    c = pl.program_id(1)
    ck = GDN_CHUNK
    dk = GDN_DK

    @pl.when(c == 0)
    def _():
        s_ref[...] = jnp.zeros_like(s_ref)
        prev_ref[...] = jnp.zeros_like(prev_ref)

    u = qkv_ref[...]
    ext = jnp.concatenate([prev_ref[...], u], axis=0)
    y = cb_ref[...]
    for i in range(CONV_W):
        y = y + ext[8 - (CONV_W - 1) + i:8 - (CONV_W - 1) + i + ck] * cw_ref[i:i + 1, :]
    prev_ref[...] = u[ck - 8:ck]
    act = _silu(y)

    small = small_ref[...]
    g_all = -jnp.exp(alog_ref[...]) * _softplus(small + dtb_ref[...])
    beta_all = jax.nn.sigmoid(small)
    ii = lax.broadcasted_iota(jnp.int32, (ck, ck), 0)
    jj = lax.broadcasted_iota(jnp.int32, (ck, ck), 1)
    tri = jnp.where(ii >= jj, 1.0, 0.0).astype(F32)
    eye = jnp.where(ii == jj, 1.0, 0.0).astype(F32)
    gc_all = lax.dot_general(tri, g_all, _NN, precision=lax.Precision.HIGHEST, preferred_element_type=F32)
    gc_t = gc_all.T

    heads = range(GDN_HEADS)
    dv = GDN_DV
    q, k, v, beta, gc, decay = [], [], [], [], [], []
    for h in heads:
        qh = act[:, h * dk:(h + 1) * dk]
        kh = act[:, GDN_QK_WIDTH + h * dk:GDN_QK_WIDTH + (h + 1) * dk]
        q.append(qh * lax.rsqrt(jnp.sum(qh * qh, axis=-1, keepdims=True) + EPS) * (dk ** -0.5))
        k.append(kh * lax.rsqrt(jnp.sum(kh * kh, axis=-1, keepdims=True) + EPS))
        v.append(act[:, 2 * GDN_QK_WIDTH + h * dv:2 * GDN_QK_WIDTH + (h + 1) * dv])
        beta.append(beta_all[:, SMALL_B0 + h:SMALL_B0 + h + 1])
        gc.append(gc_all[:, SMALL_A0 + h:SMALL_A0 + h + 1])
        gr = gc_t[SMALL_A0 + h:SMALL_A0 + h + 1, :]
        decay.append(jnp.exp(jnp.where(ii >= jj, gc[h] - gr, NEG)))
    kb = [k[h] * beta[h] for h in heads]
    ks = [_split(k[h]) for h in heads]
    qk = [_dot3s(_split(jnp.concatenate([q[h], kb[h]], axis=0)), ks[h], _NT) for h in heads]
    a_in = [qk[h][0:ck] * decay[h] for h in heads]
    lmat = [jnp.where(ii > jj, qk[h][ck:2 * ck] * decay[h], 0.0) for h in heads]
    x = [eye - lmat[h] for h in heads]
    pw = [_split(lmat[h]) for h in heads]
    for _ in range(int(math.log2(ck)) - 1):
        pw = [_split(_dot3s(pw[h], pw[h])) for h in heads]
        x = [x[h] + _dot3s(_split(x[h]), pw[h]) for h in heads]
    eg = [jnp.exp(gc[h]) for h in heads]
    uw = [_dot3s(_split(x[h]), _split(jnp.concatenate([v[h] * beta[h], kb[h] * eg[h]], axis=1))) for h in heads]
    s_old = [s_ref[h] for h in heads]
    qw_s = [_dot3s(_split(jnp.concatenate([q[h] * eg[h], uw[h][:, dv:dv + dk]], axis=0)), _split(s_old[h])) for h in heads]
    v_new = [uw[h][:, 0:dv] - qw_s[h][ck:2 * ck] for h in heads]
    vns = [_split(v_new[h]) for h in heads]
    o = [qw_s[h][0:ck] + _dot3s(_split(a_in[h]), vns[h]) for h in heads]
    outs = []
    for h in heads:
        g_last = gc[h][ck - 1:ck, :]
        kd = k[h] * jnp.exp(g_last - gc[h])
        s_ref[h] = s_old[h] * jnp.exp(g_last) + _dot3s(_split(kd.T), vns[h])
        outs.append(o[h] * lax.rsqrt(jnp.mean(o[h] * o[h], axis=-1, keepdims=True) + EPS) * gn_ref[...])
    o_ref[...] = jnp.concatenate(outs, axis=1) * szb_ref[...]

    @pl.when(c == pl.num_programs(1) - 1)
    def _():
        sfin_ref[...] = s_ref[...]


def _lane_params(a_log, dt_bias):
    alog_l = jnp.zeros((1, SMALL_W), F32).at[0, SMALL_A0:SMALL_A0 + GDN_HEADS].set(a_log)
    dtb_l = jnp.zeros((1, SMALL_W), F32).at[0, SMALL_A0:SMALL_A0 + GDN_HEADS].set(dt_bias)
    return alog_l, dtb_l


def _gdn_prompt(qkvb, small, szb, conv_w, conv_b, a_log, dt_bias, gnorm, batch):
    n = qkvb.shape[0]
    seq = n // batch
    nc = seq // GDN_CHUNK
    alog_l, dtb_l = _lane_params(a_log, dt_bias)
    row = lambda w: pl.BlockSpec((GDN_CHUNK, w), lambda b, c: (b * nc + c, 0))
    full = lambda a: pl.BlockSpec(a.shape, lambda b, c: (0,) * a.ndim)
    cb = conv_b.reshape(1, -1)
    gn = gnorm.reshape(1, -1)
    return pl.pallas_call(
        _gdn_prompt_kernel,
        grid=(batch, nc),
        in_specs=[row(CONV_DIM), row(SMALL_W), row(GDN_WIDTH), full(conv_w), full(cb), full(alog_l), full(dtb_l), full(gn)],
        out_specs=[row(GDN_WIDTH),
                   pl.BlockSpec((None, GDN_HEADS, GDN_DK, GDN_DV), lambda b, c: (b, 0, 0, 0))],
        out_shape=[jax.ShapeDtypeStruct((n, GDN_WIDTH), F32),
                   jax.ShapeDtypeStruct((batch, GDN_HEADS, GDN_DK, GDN_DV), F32)],
        scratch_shapes=[pltpu.VMEM((GDN_HEADS, GDN_DK, GDN_DV), F32), pltpu.VMEM((8, CONV_DIM), F32)],
        compiler_params=_cparams(("parallel", "arbitrary")),
        name="gdn_prompt",
    )(qkvb, small, szb, conv_w, cb, alog_l, dtb_l, gn)


def _prep_w_in(w_in, d_model):
    sizes = (NSA_WIDTH, 6 * NSA_KV_HEADS * HEAD_DIM, 3 * NSA_HEADS, NSA_WIDTH, CONV_DIM, GDN_HEADS, GDN_HEADS,
             GDN_WIDTH, 2 * d_model)
    pts = np.cumsum(np.array(sizes))[:-1].tolist()
    q_a, kv_a, g_a, z_a, qkv_b, a_b, b_b, z_b, gm = jnp.split(w_in, pts, axis=1)
    small = jnp.concatenate([g_a, a_b, b_b], axis=1)
    small = jnp.pad(small, ((0, 0), (0, SMALL_W - small.shape[1])))
    return jnp.concatenate([q_a, kv_a, z_a, qkv_b, z_b, gm, small], axis=1).astype(BF16)


def _kv_leaf(kv_t):
    b, _, t = kv_t.shape
    return kv_t.reshape(1, b, 2, NSA_KV_HEADS, HEAD_DIM, t).transpose(0, 1, 5, 2, 3, 4)


def _kv_rows_t(cache):
    n, r = cache.shape[:2]
    return cache.transpose(0, 2, 3, 4, 1).reshape(n, KV_ROW, r)


def _prompt_path(x, ada, lw, cw):
    (norm_g, w_bf, offs, conv_w, conv_b, a_log, dt_bias, gnorm, wa, wb, wo, final_g) = lw
    batch, seq, d = x.shape
    n = batch * seq
    x2 = x.reshape(n, d)
    ada3 = ada.reshape(batch, 1, 3 * d)
    tm = 256
    mod = lambda k: pl.BlockSpec((None, 1, d), lambda i: (i * tm // seq, 0, k))
    (q_aug, kvc_t, kvs_t, kvw_t, kvcb, ksel, vsel_t, kwin, vwin_t, sza, qkvb, szb, gms, small) = _inproj(
        x2, ada3, ada3, (mod(1), mod(0)), norm_g.reshape(1, d), w_bf, offs, tm, seq)
    kc_aug, vc_t = _compress_prompt(kvcb, *cw, batch)
    o_a = _nsa_prompt(q_aug, kc_aug, vc_t, ksel, vsel_t, kwin, vwin_t, small, sza, batch)
    o_b, s_new = _gdn_prompt(qkvb, small, szb, conv_w, conv_b, a_log, dt_bias, gnorm, batch)
    tmo = 512
    gate_spec = pl.BlockSpec((None, 1, d), lambda i: (i * tmo // seq, 0, 2))
    y = _outproj(x2, o_a, o_b, gms, ada3, gate_spec, wa, wb, wo, final_g.reshape(1, d), tmo)
    keep = min(WINDOW, seq)
    new_conv = qkvb.reshape(batch, seq, CONV_DIM)[None, :, seq - (CONV_W - 1):]
    return (y.reshape(batch, seq, d), _kv_leaf(kvc_t), _kv_leaf(kvs_t), _kv_leaf(kvw_t[:, :, seq - keep:]), new_conv,
            s_new[None])


def _page_fetch(pt_ref, cache_hbm, buf, sem, npages):
    def copy(seq, slot, j):
        c0 = pl.multiple_of(j * PAGE_SIZE, PAGE_SIZE)
        return pltpu.make_async_copy(cache_hbm.at[pt_ref[seq, j]], buf.at[slot, :, pl.ds(c0, PAGE_SIZE)], sem.at[slot])

    def start(seq, slot):
        def body(j, c):
            copy(seq, slot, j).start()
            return c
        lax.fori_loop(0, npages, body, 0)

    def wait(seq, slot):
        def body(j, c):
            copy(seq, slot, j).wait()
            return c
        lax.fori_loop(0, npages, body, 0)

    return start, wait


def _fetch_this_prefetch_next(start, wait):
    b = pl.program_id(0)
    nb = pl.num_programs(0)

    @pl.when(b == 0)
    def _():
        start(0, 0)

    @pl.when(b + 1 < nb)
    def _():
        start(b + 1, (b + 1) % 2)

    slot = b % 2
    wait(b, slot)
    return slot


def _compress_sample_kernel(pt_ref, cache_hbm, w1_ref, pe_ref, w2_ref, o_ref, buf, sem, x_ref, xc_ref, *, npages, tt):
    start, wait = _page_fetch(pt_ref, cache_hbm, buf, sem, npages)
    slot = _fetch_this_prefetch_next(start, wait)
    past = npages * PAGE_SIZE
    n16 = past // CMP_STRIDE

    nlt = x_ref.shape[0]

    def untranspose(j, c):
        t0 = pl.multiple_of(j * tt, tt)
        x = buf[slot, :, pl.ds(t0, tt)].T
        for i in range(nlt):
            x_ref[i, pl.ds(t0, tt), :] = x[:, i * 128:(i + 1) * 128]
        return c

    lax.fori_loop(0, past // tt, untranspose, 0)

    for s in range(CMP_STRIDE):
        for i in range(nlt):
            c0 = s * KV_ROW + i * 128
            xc_ref[:, c0:c0 + 128] = x_ref[i, pl.ds(s, n16, stride=CMP_STRIDE), :].astype(BF16)
    o_ref[...] = _compress_math(xc_ref[...], w1_ref, pe_ref, w2_ref).astype(BF16)


def _compress_sample(cache_t, page_table, w1big, pebig, w2big):
    nseq, npages = page_table.shape
    past = npages * PAGE_SIZE
    n16 = past // CMP_STRIDE
    tt = min(1024, past)
    grid_spec = pltpu.PrefetchScalarGridSpec(
        num_scalar_prefetch=1,
        grid=(nseq,),
        in_specs=[pl.BlockSpec(memory_space=pl.ANY),
                  pl.BlockSpec(w1big.shape, lambda b, pt: (0, 0, 0)),
                  pl.BlockSpec(pebig.shape, lambda b, pt: (0, 0, 0)),
                  pl.BlockSpec(w2big.shape, lambda b, pt: (0, 0))],
        out_specs=pl.BlockSpec((None, n16, KV_ROW), lambda b, pt: (b, 0, 0)),
        scratch_shapes=[pltpu.VMEM((2, KV_ROW, past), F32), pltpu.SemaphoreType.DMA((2,)),
                        pltpu.VMEM((KV_ROW // 128, past, 128), F32),
                        pltpu.VMEM((n16, CMP_STRIDE * KV_ROW), BF16)],
    )
    return pl.pallas_call(
        functools.partial(_compress_sample_kernel, npages=npages, tt=tt),
        grid_spec=grid_spec,
        out_shape=jax.ShapeDtypeStruct((nseq, n16, KV_ROW), BF16),
        compiler_params=_cparams(("arbitrary",)),
        name="compress_sample",
    )(page_table, cache_t, w1big, pebig, w2big)


def _nsa_sample_kernel(pt_ref, q_ref, kc_ref, cache_hbm, win_ref, nsel_ref, nwin_ref, nwint_ref, small_ref, sza_ref,
                       selmap_ref, onehot_ref, o_ref, wout_ref, buf, sem, *, npages, tn, kt):
    hd = HEAD_DIM
    rt = 8
    start, wait = _page_fetch(pt_ref, cache_hbm, buf, sem, npages)
    slot = _fetch_this_prefetch_next(start, wait)
    past = npages * PAGE_SIZE
    nbuf = win_ref.shape[1]
    n_cmp = kc_ref.shape[0]
    n_blk_lanes = selmap_ref.shape[1]
    q8 = q_ref[...].astype(F32)
    gates = jax.nn.sigmoid(small_ref[...])
    tau = lax.broadcasted_iota(jnp.int32, (rt, 1), 0) % tn
    t_col = past + tau
    t4 = jnp.concatenate([t_col] * NSA_GROUP, axis=0)
    rows = NSA_GROUP * rt
    newcol = lax.broadcasted_iota(jnp.int32, (1, SMALL_W), 1)
    new_dist = t4 - (past + newcol)
    new_ok = (newcol < tn) & (new_dist >= 0)
    zpad = jnp.zeros((SMALL_W - rt, hd), F32)
    groups = range(NSA_KV_HEADS)
    kcols = [slice(g * hd, (g + 1) * hd) for g in groups]
    vcols = [slice((NSA_KV_HEADS + g) * hd, (NSA_KV_HEADS + g + 1) * hd) for g in groups]

    def new_tile(ref, col):
        return jnp.concatenate([ref[:, col], zpad], axis=0).astype(BF16)

    def pos_cols(p0, n):
        pos = (p0 + lax.broadcasted_iota(jnp.int32, (n, 1), 0)).astype(F32)
        hi = jnp.floor(pos * (1.0 / POS_SPLIT))
        return _aug_cols(n, hi, pos - hi * POS_SPLIT)

    qgs, slopes, o_cs, imps = [], [], [], []
    for g in groups:
        kcol, vcol = kcols[g], vcols[g]
        qg = jnp.concatenate([q8[:, (g * NSA_GROUP + h) * AUG:(g * NSA_GROUP + h) * AUG + hd]
                              for h in range(NSA_GROUP)], axis=0).astype(BF16)
        slope = jnp.concatenate([jnp.full((rt, 1), _SLOPES[g * NSA_GROUP + h], F32) for h in range(NSA_GROUP)], axis=0)

        s = _dot(qg, kc_ref[:, kcol], _NT)
        end = lax.broadcasted_iota(jnp.int32, (1, n_cmp), 1) * CMP_STRIDE + (CMP_BLOCK - 1)
        dist = t4 - end
        mask = dist >= 0
        _, p = _softmax_block(s - slope * dist.astype(F32), mask)
        p = p / jnp.maximum(jnp.sum(p, axis=-1, keepdims=True), 1e-30)
        o_c = _dot(p.astype(BF16), kc_ref[:, vcol])
        psum = p[0:rt]
        for h in range(1, NSA_GROUP):
            psum = psum + p[h * rt:(h + 1) * rt]
        for lst, val in ((qgs, qg), (slopes, slope), (o_cs, o_c), (imps, _importance(psum, selmap_ref[...]))):
            lst.append(val)

    selm_all = _select_blocks(jnp.concatenate(imps, axis=0), jnp.concatenate([t_col] * NSA_KV_HEADS, axis=0), n_blk_lanes)

    selms, q_augs, q_fulls = [], [], []
    for g in groups:
        selm = selm_all[g * rt:(g + 1) * rt]
        q_aug = jnp.concatenate([q8[:, (g * NSA_GROUP + h) * AUG:(g * NSA_GROUP + h + 1) * AUG]
                                 for h in range(NSA_GROUP)], axis=0)
        blk_bias = jnp.where(selm > 0.5, 0.0, MASKV)
        q_full = jnp.concatenate([q_aug, jnp.concatenate([blk_bias] * NSA_GROUP, axis=0)], axis=1).astype(BF16)
        for lst, val in ((selms, selm), (q_augs, q_aug), (q_fulls, q_full)):
            lst.append(val)

    def sel_tile(j, carries):
        k0 = pl.multiple_of(j * kt, kt)
        pos = (k0 + lax.broadcasted_iota(jnp.int32, (1, kt), 1)).astype(F32)
        hi = jnp.floor(pos * (1.0 / POS_SPLIT))
        sub = lax.broadcasted_iota(jnp.int32, (AUG - hd, kt), 0)
        pos_rows = jnp.where(sub == 0, hi, jnp.where(sub == 1, pos - hi * POS_SPLIT, 0.0)).astype(BF16)
        aug = jnp.concatenate([pos_rows, onehot_ref[:, pl.ds(k0, kt)]], axis=0)
        out = []
        for g in groups:
            k_t = buf[slot, kcols[g], pl.ds(k0, kt)].astype(BF16)
            v_t = buf[slot, vcols[g], pl.ds(k0, kt)].astype(BF16)
            k_op = jnp.concatenate([k_t, aug], axis=0)
            out.append(_flash_step(carries[g], _dot(q_fulls[g], k_op), v_t, _NT))
        return tuple(out)

    carries = lax.fori_loop(0, past // kt, sel_tile, tuple(_online_init(rows, hd) for _ in groups))

    outs = []
    for g in groups:
        kcol, vcol = kcols[g], vcols[g]
        qg, slope, o_c, selm, q_aug = qgs[g], slopes[g], o_cs[g], selms[g], q_augs[g]
        nb_new = past // SEL_BLOCK
        new_sel = jnp.concatenate([selm[:, nb_new:nb_new + 1]] * NSA_GROUP, axis=0) > 0.5
        kn = jnp.concatenate([jnp.concatenate([nsel_ref[:, kcol], zpad], axis=0), pos_cols(past, SMALL_W)], axis=1)
        s = _dot(q_aug.astype(BF16), kn.astype(BF16), _NT) + jnp.where(new_ok & new_sel, 0.0, MASKV)
        _, l, acc = _flash_step(carries[g], s, new_tile(nsel_ref, vcol))
        o_s = acc / jnp.maximum(l, 1e-30)

        s = _dot(qg, win_ref[kcol, :].astype(BF16))
        dist = t4 - (past - nbuf + lax.broadcasted_iota(jnp.int32, (1, nbuf), 1))
        mask = (dist >= 0) & (dist < WINDOW)
        carry = _online_update(_online_init(rows, hd), s - slope * dist.astype(F32), mask,
                               win_ref[vcol, :].astype(BF16), _NT)
        s = _dot(qg, new_tile(nwin_ref, kcol), _NT)
        _, l, acc = _online_update(carry, s - slope * new_dist.astype(F32), new_ok, new_tile(nwin_ref, vcol))
        o_w = acc / jnp.maximum(l, 1e-30)

        for h in range(NSA_GROUP):
            c = 3 * (g * NSA_GROUP + h)
            r = slice(h * rt, (h + 1) * rt)
            outs.append(gates[:, c:c + 1] * o_c[r] + gates[:, c + 1:c + 2] * o_s[r] + gates[:, c + 2:c + 3] * o_w[r])
    o_ref[...] = jnp.concatenate(outs, axis=1) * sza_ref[...]
    wout_ref[...] = jnp.concatenate([win_ref[:, tn:nbuf], nwint_ref[...]], axis=1)


def _rep8(a, nseq, tn):
    a = a.reshape(nseq, tn, a.shape[-1])
    return jnp.concatenate([a] * (8 // tn), axis=1)


def _nsa_sample(qs, kc, cache_t, win_t, kvs, kvw, small, sza, page_table, tn):
    nseq, npages = page_table.shape
    past = npages * PAGE_SIZE
    nbuf = win_t.shape[2]
    assert nbuf == WINDOW and 8 % tn == 0 and tn <= SEL_BLOCK
    n_cmp = kc.shape[1]
    n_blk = -(-(past + tn) // SEL_BLOCK)
    n_blk_lanes = -(-n_blk // 128) * 128
    selmap = _sel_map(n_cmp, n_blk, n_blk_lanes)
    kt = min(4096, past)
    key_blk = np.arange(past)[None, :] // SEL_BLOCK
    onehot_t = jnp.asarray(key_blk == np.arange(n_blk_lanes)[:, None], dtype=BF16)
    seq3 = lambda r, w: pl.BlockSpec((None, r, w), lambda b, pt: (b, 0, 0))
    grid_spec = pltpu.PrefetchScalarGridSpec(
        num_scalar_prefetch=1,
        grid=(nseq,),
        in_specs=[seq3(8, NSA_HEADS * AUG), seq3(n_cmp, KV_ROW), pl.BlockSpec(memory_space=pl.ANY), seq3(KV_ROW, nbuf),
                  seq3(8, KV_ROW), seq3(8, KV_ROW), seq3(KV_ROW, tn), seq3(8, SMALL_W), seq3(8, NSA_WIDTH),
                  pl.BlockSpec(selmap.shape, lambda b, pt: (0, 0)),
                  pl.BlockSpec(onehot_t.shape, lambda b, pt: (0, 0))],
        out_specs=[seq3(8, NSA_WIDTH), seq3(KV_ROW, nbuf)],
        scratch_shapes=[pltpu.VMEM((2, KV_ROW, past), F32), pltpu.SemaphoreType.DMA((2,))],
    )
    rep8 = lambda a: jnp.concatenate([a.reshape(nseq, tn, a.shape[-1])] * (8 // tn), axis=1)
    o8, win_new_t = pl.pallas_call(
        functools.partial(_nsa_sample_kernel, npages=npages, tn=tn, kt=kt),
        grid_spec=grid_spec,
        out_shape=[jax.ShapeDtypeStruct((nseq, 8, NSA_WIDTH), F32), jax.ShapeDtypeStruct((nseq, KV_ROW, nbuf), F32)],
        compiler_params=_cparams(("arbitrary",)),
        name="nsa_sample",
    )(page_table, rep8(qs), kc, cache_t, win_t, rep8(kvs), rep8(kvw), kvw.transpose(0, 2, 1), rep8(small), rep8(sza),
      selmap, onehot_t)
    return o8[:, :tn].reshape(nseq * tn, NSA_WIDTH), win_new_t


def _gdn_sample_kernel(eq_ref, ek_ref, ev_ref, cwq_ref, cwk_ref, cwv_ref, cbq_ref, cbk_ref, cbv_ref, small_ref,
                       alog_ref, dtb_ref, szb_ref, gn_ref, s_ref, o_ref, so_ref, qs_ref, ks_ref):
    h = pl.program_id(0)
    tn = o_ref.shape[0]
    dk, dv, nseq = so_ref.shape

    def conv(e_ref, cw_ref, cb_ref, t):
        y = cb_ref[...]
        for i in range(CONV_W):
            y = y + e_ref[t + i] * cw_ref[:, i:i + 1]
        return _silu(y)

    so_ref[...] = s_ref[...]
    neg_rate = -jnp.exp(alog_ref[...])
    for t in range(tn):
        q = conv(eq_ref, cwq_ref, cbq_ref, t)
        k = conv(ek_ref, cwk_ref, cbk_ref, t)
        v = conv(ev_ref, cwv_ref, cbv_ref, t)
        qs_ref[...] = q * lax.rsqrt(jnp.sum(q * q, axis=0, keepdims=True) + EPS) * (dk ** -0.5)
        ks_ref[...] = k * lax.rsqrt(jnp.sum(k * k, axis=0, keepdims=True) + EPS)
        a_in = small_ref[t, pl.ds(SMALL_A0 + h, 1), :]
        b_in = small_ref[t, pl.ds(SMALL_B0 + h, 1), :]
        decay = jnp.exp(neg_rate * _softplus(a_in + dtb_ref[...]))
        beta = jax.nn.sigmoid(b_in)

        def ks_step(i, acc):
            return acc + ks_ref[pl.ds(i, 1), :] * so_ref[i]

        k_s = lax.fori_loop(0, dk, ks_step, jnp.zeros((dv, nseq), F32), unroll=8)
        delta = beta * (v - decay * k_s)

        def upd_step(i, acc):
            s_new = decay * so_ref[i] + ks_ref[pl.ds(i, 1), :] * delta
            so_ref[i] = s_new
            return acc + qs_ref[pl.ds(i, 1), :] * s_new

        o = lax.fori_loop(0, dk, upd_step, jnp.zeros((dv, nseq), F32), unroll=8)
        o = o * lax.rsqrt(jnp.mean(o * o, axis=0, keepdims=True) + EPS) * gn_ref[...]
        o_ref[t] = o * szb_ref[t]


def _gdn_sample(qkvb, small, szb, state_conv, state_gdn, conv_w, conv_b, a_log, dt_bias, gnorm, tn):
    nseq = state_gdn.shape[0]
    ext = jnp.concatenate([state_conv, qkvb.reshape(nseq, tn, CONV_DIM)], axis=1)
    ext_t = ext.transpose(1, 2, 0)
    small_t = small.reshape(nseq, tn, SMALL_W).transpose(1, 2, 0)
    szb_t = szb.reshape(nseq, tn, GDN_WIDTH).transpose(1, 2, 0)
    s_t = state_gdn.transpose(1, 2, 3, 0)
    cw_t = conv_w.T
    cb_t = conv_b.reshape(-1, 1)
    alog_b = jnp.broadcast_to(a_log[:, None, None], (GDN_HEADS, 1, nseq))
    dtb_b = jnp.broadcast_to(dt_bias[:, None, None], (GDN_HEADS, 1, nseq))
    gn = gnorm.reshape(-1, 1)
    nqk = GDN_QK_WIDTH // GDN_DK
    chan = lambda off: pl.BlockSpec((CONV_W - 1 + tn, GDN_DK, nseq), lambda h: (0, off + h, 0))
    cwb = lambda off: pl.BlockSpec((GDN_DK, CONV_W), lambda h: (off + h, 0))
    cbb = lambda off: pl.BlockSpec((GDN_DK, 1), lambda h: (off + h, 0))
    perhead = pl.BlockSpec((None, 1, nseq), lambda h: (h, 0, 0))
    o_t, s_new = pl.pallas_call(
        _gdn_sample_kernel,
        grid=(GDN_HEADS,),
        in_specs=[chan(0), chan(nqk), chan(2 * nqk), cwb(0), cwb(nqk), cwb(2 * nqk), cbb(0), cbb(nqk), cbb(2 * nqk),
                  pl.BlockSpec((tn, SMALL_W, nseq), lambda h: (0, 0, 0)), perhead, perhead,
                  pl.BlockSpec((tn, GDN_DV, nseq), lambda h: (0, h, 0)),
                  pl.BlockSpec((GDN_DV, 1), lambda h: (0, 0)),
                  pl.BlockSpec((None, GDN_DK, GDN_DV, nseq), lambda h: (h, 0, 0, 0))],
        out_specs=[pl.BlockSpec((tn, GDN_DV, nseq), lambda h: (0, h, 0)),
                   pl.BlockSpec((None, GDN_DK, GDN_DV, nseq), lambda h: (h, 0, 0, 0))],
        out_shape=[jax.ShapeDtypeStruct((tn, GDN_WIDTH, nseq), F32),
                   jax.ShapeDtypeStruct((GDN_HEADS, GDN_DK, GDN_DV, nseq), F32)],
        scratch_shapes=[pltpu.VMEM((GDN_DK, nseq), F32), pltpu.VMEM((GDN_DK, nseq), F32)],
        compiler_params=_cparams(("parallel",)),
        name="gdn_sample",
    )(ext_t, ext_t, ext_t, cw_t, cw_t, cw_t, cb_t, cb_t, cb_t, small_t, alog_b, dtb_b, szb_t, gn, s_t)
    o_b = o_t.transpose(2, 0, 1).reshape(nseq * tn, GDN_WIDTH)
    return o_b, ext[:, tn:], s_new.transpose(3, 0, 1, 2)


def _sample_path(x, ada, lw, cw, cache_cmp, cache_sel, cache_win, state_conv, state_gdn, page_table):
    (norm_g, w_bf, offs, conv_w, conv_b, a_log, dt_bias, gnorm, wa, wb, wo, final_g) = lw
    nseq, tn, d = x.shape
    n = nseq * tn
    x2 = x.reshape(n, d)
    ada_rows = jnp.repeat(ada, tn, axis=0)
    tm = min(256, n)
    mod = lambda k: pl.BlockSpec((tm, d), lambda i: (i, k))
    (qs, kvc_t, kvs_t, kvw_t, _, _, _, _, _, sza, qkvb, szb, gms, small) = _inproj(
        x2, ada_rows, ada_rows, (mod(1), mod(0)), norm_g.reshape(1, d), w_bf, offs, tm, n)
    rows = lambda a_t: a_t.reshape(KV_ROW, nseq, tn).transpose(1, 2, 0)
    kvc, kvs, kvw = rows(kvc_t), rows(kvs_t), rows(kvw_t)
    kc = _compress_sample(_kv_rows_t(cache_cmp), page_table, *cw)
    o_a, win_new_t = _nsa_sample(qs, kc, _kv_rows_t(cache_sel), _kv_rows_t(cache_win), kvs, kvw, small, sza, page_table, tn)
    o_b, conv_new, s_new = _gdn_sample(qkvb, small, szb, state_conv, state_gdn, conv_w, conv_b, a_log, dt_bias, gnorm, tn)
    y = _outproj(x2, o_a, o_b, gms, ada_rows, mod(2), wa, wb, wo, final_g.reshape(1, d), tm)
    kvshape = (1, nseq, tn, 2, NSA_KV_HEADS, HEAD_DIM)
    return (y.reshape(nseq, tn, d), kvc.reshape(kvshape), kvs.reshape(kvshape), _kv_leaf(win_new_t),
            conv_new[None], s_new[None])


def kernel(x_prompt, x_sample, cache_cmp_kv, cache_sel_kv, cache_win_kv, state_conv, state_gdn, page_table, c_prompt, c_sample, norm_g, w_ada, b_ada, w_in, cmp_pe_k, cmp_w1_k, cmp_w2_k, cmp_pe_v, cmp_w1_v, cmp_w2_v, conv_w, conv_b, gdn_a_log, gdn_dt_bias, gdn_norm_g, w_o_nsa, w_o_gdn, w_out, final_g):
    assert norm_g.shape[0] == 1, "single trunk layer"
    d = x_prompt.shape[-1]
    l = 0
    offs, _ = _seg_offsets(d)
    w_bf = _prep_w_in(w_in[l], d)
    ada = _ada(jnp.concatenate([c_prompt, c_sample], axis=0), w_ada[l].astype(BF16), b_ada[l])
    cw = _compress_weights(cmp_pe_k[l], cmp_w1_k[l], cmp_w2_k[l], cmp_pe_v[l], cmp_w1_v[l], cmp_w2_v[l])
    lw = (norm_g[l], w_bf, offs, conv_w[l], conv_b[l], gdn_a_log[l], gdn_dt_bias[l], gdn_norm_g[l],
          w_o_nsa[l].astype(BF16), w_o_gdn[l].astype(BF16), w_out[l].astype(BF16), final_g)
    nb = c_prompt.shape[0]
    yp, cmp_p, sel_p, win_p, conv_p, gdn_p = _prompt_path(x_prompt, ada[:nb], lw, cw)
    ys, cmp_s, sel_s, win_s, conv_s, gdn_s = _sample_path(
        x_sample, ada[nb:], lw, cw, cache_cmp_kv[l], cache_sel_kv[l], cache_win_kv[l], state_conv[l], state_gdn[l],
        page_table)
    return (yp, ys, cmp_p, sel_p, win_p, conv_p, gdn_p, cmp_s, sel_s, win_s, conv_s, gdn_s)
```

```python
import functools
import math

import numpy as np
import jax
import jax.numpy as jnp
from jax import lax
from jax.experimental import pallas as pl
from jax.experimental.pallas import tpu as pltpu

F32 = jnp.float32
BF16 = jnp.bfloat16

NSA_HEADS = 8
NSA_KV_HEADS = 2
NSA_GROUP = NSA_HEADS // NSA_KV_HEADS
HEAD_DIM = 64
CMP_BLOCK = 32
CMP_STRIDE = 16
CMP_HIDDEN = 64
SEL_BLOCK = 64
N_SEL = 16
WINDOW = 512
Q_BLOCK = 128
GDN_HEADS = 8
GDN_DK = 64
GDN_DV = 64
CONV_W = 4
GDN_CHUNK = 64
PAGE_SIZE = 128
EPS = 1e-6

NSA_WIDTH = NSA_HEADS * HEAD_DIM
KV_ROW = 2 * NSA_KV_HEADS * HEAD_DIM
GDN_QK_WIDTH = GDN_HEADS * GDN_DK
GDN_WIDTH = GDN_HEADS * GDN_DV
CONV_DIM = 2 * GDN_QK_WIDTH + GDN_WIDTH
SMALL_W = 128
SMALL_A0 = 3 * NSA_HEADS
SMALL_B0 = SMALL_A0 + GDN_HEADS

NEG = -1e30
MASKV = -(2.0 ** 100)
AUG = 128
POS_SPLIT = 128
VMEM_LIMIT = 56 * 1024 * 1024

_NT = (((1,), (1,)), ((), ()))
_NN = (((1,), (0,)), ((), ()))


def _alibi_slopes():
    h = np.arange(1, NSA_HEADS + 1, dtype=np.float32)
    return [float(v) for v in np.power(np.float32(2.0), -np.float32(8.0) * h / np.float32(NSA_HEADS))]


_SLOPES = _alibi_slopes()


def _dot(a, b, dims=_NN):
    return lax.dot_general(a, b, dims, preferred_element_type=F32)


def _split(a):
    hi = a.astype(BF16)
    lo = (a - hi.astype(F32)).astype(BF16)
    return hi, lo


def _dot3s(a_split, b_split, dims=_NN):
    (ah, al), (bh, bl) = a_split, b_split
    return _dot(ah, bh, dims) + _dot(ah, bl, dims) + _dot(al, bh, dims)


def _silu(x):
    return x * jax.nn.sigmoid(x)


def _cparams(sem):
    return pltpu.CompilerParams(dimension_semantics=sem, vmem_limit_bytes=VMEM_LIMIT)


def _ada_kernel(c_ref, w_ref, b_ref, o_ref):
    sc = _silu(c_ref[...]).astype(BF16)
    o_ref[...] = _dot(sc, w_ref[...]) + b_ref[...]


def _ada(c, w_bf, b):
    n, d = c.shape
    n_pad = -(-n // 8) * 8
    c = jnp.pad(c, ((0, n_pad - n), (0, 0)))
    out = pl.pallas_call(
        _ada_kernel,
        out_shape=jax.ShapeDtypeStruct((n_pad, w_bf.shape[1]), F32),
        name="ada",
    )(c, w_bf, b.reshape(1, -1))
    return out[:n]


_SEG = (("q", NSA_WIDTH), ("kvc", KV_ROW), ("kvs", KV_ROW), ("kvw", KV_ROW), ("za", NSA_WIDTH),
        ("qkvb", CONV_DIM), ("zb", GDN_WIDTH), ("gm", None), ("small", SMALL_W))


def _seg_offsets(d_model):
    offs, c = {}, 0
    for name, n in _SEG:
        n = 2 * d_model if n is None else n
        offs[name] = (c, n)
        c += n
    return offs, c


def _aug_cols(rows, c0, c1):
    lane = lax.broadcasted_iota(jnp.int32, (rows, AUG - HEAD_DIM), 1)
    return jnp.where(lane == 0, c0, jnp.where(lane == 1, c1, 0.0)).astype(F32)


def _aug_keys(kv, pos):
    hi = jnp.floor(pos * (1.0 / POS_SPLIT))
    cols = _aug_cols(kv.shape[0], hi, pos - hi * POS_SPLIT)
    parts = []
    for g in range(NSA_KV_HEADS):
        parts += [kv[:, g * HEAD_DIM:(g + 1) * HEAD_DIM], cols]
    k_aug = jnp.concatenate(parts, axis=1).astype(BF16)
    kv_t = kv.T
    return k_aug, kv_t[NSA_KV_HEADS * HEAD_DIM:].astype(BF16), kv_t


def _inproj_kernel(x_ref, scale_ref, shift_ref, ng_ref, w_ref,
                   q_ref, kvct_ref, kvst_ref, kvwt_ref, kvcb_ref, ksel_ref, vselt_ref, kwin_ref, vwint_ref,
                   sza_ref, qkvb_ref, szb_ref, gms_ref, small_ref, *, offs, seq):
    x = x_ref[...]
    tm = x.shape[0]
    y = x * lax.rsqrt(jnp.mean(x * x, axis=-1, keepdims=True) + EPS) * ng_ref[...]
    h = y * (1.0 + scale_ref[...]) + shift_ref[...]
    hb = h.astype(BF16)

    def seg(name):
        c0, n = offs[name]
        return _dot(hb, w_ref[:, c0:c0 + n])

    qv = seg("q") * (HEAD_DIM ** -0.5)
    parts = []
    for hh in range(NSA_HEADS):
        parts += [qv[:, hh * HEAD_DIM:(hh + 1) * HEAD_DIM], _aug_cols(tm, _SLOPES[hh] * POS_SPLIT, _SLOPES[hh])]
    q_ref[...] = jnp.concatenate(parts, axis=1).astype(BF16)

    pos = ((pl.program_id(0) * tm) % seq + lax.broadcasted_iota(jnp.int32, (tm, 1), 0)).astype(F32)
    v = seg("kvc")
    kvct_ref[...] = v.T
    kvcb_ref[...] = v.astype(BF16)
    for name, f_ref, k_ref, vt_ref in (("kvs", kvst_ref, ksel_ref, vselt_ref), ("kvw", kvwt_ref, kwin_ref, vwint_ref)):
        k_ref[...], vt_ref[...], f_ref[...] = _aug_keys(seg(name), pos)
    sza_ref[...] = _silu(seg("za"))
    qkvb_ref[...] = seg("qkvb")
    szb_ref[...] = _silu(seg("zb"))
    gms_ref[...] = jax.nn.sigmoid(seg("gm"))
    small_ref[...] = seg("small")


def _inproj(x2, scale_arr, shift_arr, mod_specs, ng, w_bf, offs, tm, seq):
    n, d = x2.shape
    wtot = w_bf.shape[1]
    nbatch = n // seq
    spt = seq // tm
    vw = NSA_KV_HEADS * HEAD_DIM
    row = lambda w: pl.BlockSpec((tm, w), lambda i: (i, 0))
    rows = lambda w, dt: (jax.ShapeDtypeStruct((n, w), dt), row(w))
    tr = lambda w, dt: (jax.ShapeDtypeStruct((nbatch, w, seq), dt),
                        pl.BlockSpec((None, w, tm), lambda i: (i // spt, 0, i % spt)))
    vt = tr(vw, BF16)
    kvt = tr(KV_ROW, F32)
    outs = [rows(NSA_HEADS * AUG, BF16), kvt, kvt, kvt, rows(KV_ROW, BF16),
            rows(NSA_KV_HEADS * AUG, BF16), vt, rows(NSA_KV_HEADS * AUG, BF16), vt,
            rows(offs["za"][1], F32), rows(offs["qkvb"][1], F32), rows(offs["zb"][1], F32), rows(offs["gm"][1], F32),
            rows(offs["small"][1], F32)]
    out_shape = [o[0] for o in outs]
    out_specs = [o[1] for o in outs]
    return pl.pallas_call(
        functools.partial(_inproj_kernel, offs=offs, seq=seq),
        grid=(n // tm,),
        in_specs=[row(d), mod_specs[0], mod_specs[1],
                  pl.BlockSpec((1, d), lambda i: (0, 0)),
                  pl.BlockSpec((d, wtot), lambda i: (0, 0))],
        out_specs=out_specs,
        out_shape=out_shape,
        compiler_params=_cparams(("parallel",)),
        name="inproj",
    )(x2, scale_arr, shift_arr, ng, w_bf)


def _outproj_kernel(x_ref, oa_ref, ob_ref, gms_ref, gate_ref, wa_ref, wb_ref, wo_ref, fg_ref, y_ref, *, d):
    ma = _dot(oa_ref[...].astype(BF16), wa_ref[...])
    mb = _dot(ob_ref[...].astype(BF16), wb_ref[...])
    m = gms_ref[:, 0:d] * ma + gms_ref[:, d:2 * d] * mb
    y = x_ref[...] + gate_ref[...] * _dot(m.astype(BF16), wo_ref[...])
    y_ref[...] = y * lax.rsqrt(jnp.mean(y * y, axis=-1, keepdims=True) + EPS) * fg_ref[...]


def _outproj(x2, oa, ob, gms, gate_arr, gate_spec, wa, wb, wo, fg, tm):
    n, d = x2.shape
    row = lambda w: pl.BlockSpec((tm, w), lambda i: (i, 0))
    full = lambda a: pl.BlockSpec(a.shape, lambda i: (0, 0))
    return pl.pallas_call(
        functools.partial(_outproj_kernel, d=d),
        grid=(n // tm,),
        in_specs=[row(d), row(oa.shape[1]), row(ob.shape[1]), row(2 * d), gate_spec,
                  full(wa), full(wb), full(wo), full(fg)],
        out_specs=row(d),
        out_shape=jax.ShapeDtypeStruct((n, d), F32),
        compiler_params=_cparams(("parallel",)),
        name="outproj",
    )(x2, oa, ob, gms, gate_arr, wa, wb, wo, fg)


def _compress_math(x, w1_ref, pe_ref, w2_ref):
    n16 = x.shape[0]
    a0 = _dot(x, w1_ref[0])
    a1 = _dot(x, w1_ref[1])
    p0 = _dot(pe_ref[0], w1_ref[0]) + _dot(pe_ref[1], w1_ref[1])
    pre = a0 + pltpu.roll(a1, n16 - 1, 0) + p0[0:1]
    return _dot(_silu(pre).astype(BF16), w2_ref[...])


def _compress_kernel(x_ref, w1_ref, pe_ref, w2_ref, k_ref, vt_ref):
    kv = _compress_math(x_ref[...], w1_ref, pe_ref, w2_ref)
    n16 = kv.shape[0]
    end = (lax.broadcasted_iota(jnp.int32, (n16, 1), 0) * CMP_STRIDE + (CMP_BLOCK - 1)).astype(F32)
    k_ref[...], vt_ref[...], _ = _aug_keys(kv, end)


def _compress_prompt(kvcb, w1big, pebig, w2big, batch):
    n = kvcb.shape[0]
    n16 = n // batch // CMP_STRIDE
    x = kvcb.reshape(batch, n16, CMP_STRIDE * KV_ROW)
    vw = NSA_KV_HEADS * HEAD_DIM
    return pl.pallas_call(
        _compress_kernel,
        grid=(batch,),
        in_specs=[pl.BlockSpec((None, n16, CMP_STRIDE * KV_ROW), lambda b: (b, 0, 0)),
                  pl.BlockSpec(w1big.shape, lambda b: (0, 0, 0)),
                  pl.BlockSpec(pebig.shape, lambda b: (0, 0, 0)),
                  pl.BlockSpec(w2big.shape, lambda b: (0, 0))],
        out_specs=[pl.BlockSpec((None, n16, NSA_KV_HEADS * AUG), lambda b: (b, 0, 0)),
                   pl.BlockSpec((None, vw, n16), lambda b: (b, 0, 0))],
        out_shape=[jax.ShapeDtypeStruct((batch, n16, NSA_KV_HEADS * AUG), BF16),
                   jax.ShapeDtypeStruct((batch, vw, n16), BF16)],
        compiler_params=_cparams(("parallel",)),
        name="compress_prompt",
    )(x, w1big, pebig, w2big)


def _compress_weights(pe_k, w1_k, w2_k, pe_v, w1_v, w2_v):
    r_cnt = CMP_BLOCK // CMP_STRIDE
    g = NSA_KV_HEADS
    eye2 = jnp.eye(2, dtype=F32)
    eyeg = jnp.eye(g, dtype=F32)
    w1 = jnp.stack([w1_k, w1_v]).reshape(2, r_cnt, CMP_STRIDE, HEAD_DIM, CMP_HIDDEN)
    w1big = jnp.einsum("krsde,kK,gG->rskgdKGe", w1, eye2, eyeg)
    w1big = w1big.reshape(r_cnt, CMP_STRIDE * KV_ROW, 2 * g * CMP_HIDDEN).astype(BF16)
    pe = jnp.stack([pe_k, pe_v]).reshape(2, r_cnt, CMP_STRIDE, HEAD_DIM)
    pebig = jnp.broadcast_to(pe.transpose(1, 2, 0, 3)[:, :, :, None, :], (r_cnt, CMP_STRIDE, 2, g, HEAD_DIM))
    pebig = pebig.reshape(r_cnt, 1, CMP_STRIDE * KV_ROW)
    pebig = jnp.pad(pebig, ((0, 0), (0, 7), (0, 0))).astype(BF16)
    w2 = jnp.stack([w2_k, w2_v])
    w2big = jnp.einsum("ked,kK,gG->kgeKGd", w2, eye2, eyeg).reshape(2 * g * CMP_HIDDEN, KV_ROW).astype(BF16)
    return w1big, pebig, w2big


def _sel_map(n_cmp_rows, n_blk, n_blk_pad):
    i = np.arange(n_cmp_rows)[:, None] * CMP_STRIDE
    j = np.arange(n_blk_pad)[None, :] * SEL_BLOCK
    ov = np.minimum(i + CMP_BLOCK, j + SEL_BLOCK) - np.maximum(i, j)
    m = np.clip(ov, 0, None).astype(np.float32) / np.float32(CMP_BLOCK)
    m[:, n_blk:] = 0.0
    return jnp.asarray(m, dtype=BF16)


def _softmax_block(s, mask):
    s = jnp.where(mask, s, NEG)
    m = jnp.max(s, axis=-1, keepdims=True)
    p = jnp.where(mask, jnp.exp(s - m), 0.0)
    return m, p


def _online_update(carry, s, mask, v_bf, v_dims=_NN):
    m, l, acc = carry
    s = jnp.where(mask, s, NEG)
    m_new = jnp.maximum(m, jnp.max(s, axis=-1, keepdims=True))
    alpha = jnp.exp(m - m_new)
    p = jnp.where(mask, jnp.exp(s - m_new), 0.0)
    l = alpha * l + jnp.sum(p, axis=-1, keepdims=True)
    acc = alpha * acc + _dot(p.astype(BF16), v_bf, v_dims)
    return m_new, l, acc


def _flash_step(carry, s, v_bf, v_dims=_NN):
    m, l, acc = carry
    m_new = jnp.maximum(m, jnp.max(s, axis=-1, keepdims=True))
    alpha = jnp.exp(m - m_new)
    p = jnp.exp(s - m_new)
    return (m_new, alpha * l + jnp.sum(p, axis=-1, keepdims=True),
            alpha * acc + _dot(p.astype(BF16), v_bf, v_dims))


def _online_init(rows, hd):
    return (jnp.full((rows, 1), NEG, F32), jnp.zeros((rows, 1), F32), jnp.zeros((rows, hd), F32))


def _importance(p_sum, selmap):
    ph, pl_ = _split(p_sum)
    pl2 = (p_sum - ph.astype(F32) - pl_.astype(F32)).astype(BF16)
    return _dot(ph, selmap) + _dot(pl_, selmap) + _dot(pl2, selmap)


def _flash_step_t(carry, s_t, v_t):
    m, l, acc = carry
    m_new = jnp.maximum(m, jnp.max(s_t, axis=0, keepdims=True))
    alpha = jnp.exp(m - m_new)
    p = jnp.exp(s_t - m_new)
    l = alpha * l + jnp.sum(p, axis=0, keepdims=True)
    acc = alpha * acc + _dot(v_t, p.astype(BF16))
    return m_new, l, acc


def _flash_init_t(cols, dv):
    return (jnp.full((1, cols), NEG, F32), jnp.zeros((1, cols), F32), jnp.zeros((dv, cols), F32))


def _select_blocks_t(imp_t, tq):
    nb, nq = imp_t.shape
    jj = lax.broadcasted_iota(jnp.int32, (nb, nq), 0)
    cur = tq // SEL_BLOCK
    forced = (jj == 0) | (jj == cur) | (jj == cur - 1)
    jf = jj.astype(F32)
    score = jnp.where(forced, NEG, jnp.where(jj <= cur, imp_t, NEG))
    selm = jnp.where(forced, 1.0, 0.0).astype(F32)
    for _ in range(N_SEL - 3):
        m = jnp.max(score, axis=0, keepdims=True)
        idx = jnp.min(jnp.where(score == m, jf, float(nb)), axis=0, keepdims=True)
        hit = jnp.where(m > 0.5 * NEG, idx, -1.0) == jf
        selm = jnp.where(hit, 1.0, selm)
        score = jnp.where(hit, NEG, score)
    return selm


def _nsa_prompt_kernel(q_ref, kc_ref, vct_ref, ksel_ref, vselt_ref, kwin_ref, vwint_ref, onehot_ref, small_ref, sza_ref,
                       selmapt_ref, o_ref, flags_ref, m_ref, l_ref, acc_ref, *, kt):
    qb = Q_BLOCK
    hd = HEAD_DIM
    q0 = pl.program_id(1) * qb
    n_cmp = kc_ref.shape[0]
    cols = NSA_GROUP * qb
    tq = q0 + lax.broadcasted_iota(jnp.int32, (1, qb), 1)
    gates_t = jax.nn.sigmoid(small_ref[...]).T
    wlen = WINDOW + qb
    w0 = pl.multiple_of(jnp.maximum(q0 - WINDOW, 0), qb)
    tile4 = lambda a: jnp.concatenate([a] * NSA_GROUP, axis=1)

    def keymask(pos_col, lo):
        dist = tq - pos_col
        ok = (dist >= 0) if lo is None else ((dist >= 0) & (dist < lo))
        return tile4(jnp.where(ok, 0.0, MASKV).astype(F32))

    cmp_bias = keymask(lax.broadcasted_iota(jnp.int32, (n_cmp, 1), 0) * CMP_STRIDE + (CMP_BLOCK - 1), None)
    win_bias = keymask(w0 + lax.broadcasted_iota(jnp.int32, (wlen, 1), 0), WINDOW)
    jlast = q0 // kt
    klast = pl.multiple_of(jlast * kt, kt)
    diag_bias = keymask(klast + lax.broadcasted_iota(jnp.int32, (kt, 1), 0), None)

    groups = range(NSA_KV_HEADS)
    kcols = [slice(g * AUG, (g + 1) * AUG) for g in groups]
    vrows = [slice(g * hd, (g + 1) * hd) for g in groups]
    q_heads = [[q_ref[:, (g * NSA_GROUP + h) * AUG:(g * NSA_GROUP + h + 1) * AUG] for h in range(NSA_GROUP)]
               for g in groups]
    q_cat = [jnp.concatenate(q_heads[g], axis=0) for g in groups]
    finish = lambda carry: carry[2] * (1.0 / jnp.maximum(carry[1], 1e-30))

    o_w = [finish(_flash_step_t(_flash_init_t(cols, hd), _dot(kwin_ref[pl.ds(w0, wlen), kcols[g]], q_cat[g], _NT) + win_bias,
                                vwint_ref[vrows[g], pl.ds(w0, wlen)])) for g in groups]

    o_c, q_full = [], []
    sm_t = selmapt_ref[...]
    for g in groups:
        s_t = _dot(kc_ref[:, kcols[g]], q_cat[g], _NT) + cmp_bias
        m = jnp.maximum(jnp.max(s_t, axis=0, keepdims=True), NEG)
        p = jnp.exp(s_t - m)
        p = p * (1.0 / jnp.maximum(jnp.sum(p, axis=0, keepdims=True), 1e-30))
        o_c.append(_dot(vct_ref[vrows[g], :], p.astype(BF16)))
        psum = p[:, 0:qb]
        for h in range(1, NSA_GROUP):
            psum = psum + p[:, h * qb:(h + 1) * qb]
        ph, pl_ = _split(psum)
        pl2 = (psum - ph.astype(F32) - pl_.astype(F32)).astype(BF16)
        imp_t = _dot(sm_t, ph) + _dot(sm_t, pl_) + _dot(sm_t, pl2)
        selm = _select_blocks_t(imp_t, tq).T
        blk_bias = jnp.where(selm > 0.5, 0.0, MASKV).astype(BF16)
        q_full.append(jnp.concatenate([jnp.concatenate([qh, blk_bias], axis=1) for qh in q_heads[g]], axis=0))
        blk_any = jnp.max(selm, axis=0, keepdims=True)
        bpt = kt // SEL_BLOCK
        for j in range(flags_ref.shape[1]):
            flags_ref[g, j] = (jnp.max(blk_any[:, j * bpt:(j + 1) * bpt]) > 0.5).astype(jnp.int32)
        for ref, val in zip((m_ref, l_ref, acc_ref), _flash_init_t(cols, hd)):
            ref[g] = val

    def sel_step(k0, g, bias):
        k_op = jnp.concatenate([ksel_ref[pl.ds(k0, kt), kcols[g]], onehot_ref[pl.ds(k0, kt), :]], axis=1)
        s_t = _dot(k_op, q_full[g], _NT)
        s_t = s_t if bias is None else s_t + bias
        m, l, acc = _flash_step_t((m_ref[g], l_ref[g], acc_ref[g]), s_t, vselt_ref[vrows[g], pl.ds(k0, kt)])
        m_ref[g], l_ref[g], acc_ref[g] = m, l, acc

    def sel_tile(j, c):
        for g in groups:
            @pl.when(flags_ref[g, j] > 0)
            def _():
                sel_step(pl.multiple_of(j * kt, kt), g, None)
        return c

    lax.fori_loop(0, jlast, sel_tile, 0)
    o_s = []
    for g in groups:
        sel_step(klast, g, diag_bias)
        o_s.append(finish((m_ref[g], l_ref[g], acc_ref[g])))

    outs = []
    for g in groups:
        for h in range(NSA_GROUP):
            c = 3 * (g * NSA_GROUP + h)
            r = slice(h * qb, (h + 1) * qb)
            outs.append(gates_t[c:c + 1, :] * o_c[g][:, r] + gates_t[c + 1:c + 2, :] * o_s[g][:, r]
                        + gates_t[c + 2:c + 3, :] * o_w[g][:, r])
    o_ref[...] = jnp.concatenate(outs, axis=0).T * sza_ref[...]


def _nsa_prompt(q_aug, kc_aug, vc_t, ksel, vsel_t, kwin, vwin_t, small, sza, batch):
    n = q_aug.shape[0]
    seq = n // batch
    nq = seq // Q_BLOCK
    n_cmp = kc_aug.shape[1]
    n_blk = -(-seq // SEL_BLOCK)
    assert n_blk <= AUG and seq % Q_BLOCK == 0 and seq >= WINDOW + Q_BLOCK
    kt = min(512, seq)
    selmap_t = _sel_map(n_cmp, n_blk, AUG).T
    pos = np.arange(seq)[:, None] // SEL_BLOCK
    onehot = jnp.asarray(pos == np.arange(AUG)[None, :], dtype=BF16)
    vw = NSA_KV_HEADS * HEAD_DIM
    kw = NSA_KV_HEADS * AUG
    row = lambda w: pl.BlockSpec((Q_BLOCK, w), lambda b, i: (b * nq + i, 0))
    kspec = pl.BlockSpec((None, seq, kw), lambda b, i: (b, 0, 0))
    vspec = pl.BlockSpec((None, vw, seq), lambda b, i: (b, 0, 0))
    return pl.pallas_call(
        functools.partial(_nsa_prompt_kernel, kt=kt),
        grid=(batch, nq),
        in_specs=[row(NSA_HEADS * AUG),
                  pl.BlockSpec((None, n_cmp, kw), lambda b, i: (b, 0, 0)),
                  pl.BlockSpec((None, vw, n_cmp), lambda b, i: (b, 0, 0)),
                  kspec, vspec, kspec, vspec,
                  pl.BlockSpec(onehot.shape, lambda b, i: (0, 0)),
                  row(SMALL_W), row(NSA_WIDTH),
                  pl.BlockSpec(selmap_t.shape, lambda b, i: (0, 0))],
        out_specs=row(NSA_WIDTH),
        out_shape=jax.ShapeDtypeStruct((n, NSA_WIDTH), F32),
        scratch_shapes=[pltpu.SMEM((NSA_KV_HEADS, seq // kt), jnp.int32),
                        pltpu.VMEM((NSA_KV_HEADS, 1, NSA_GROUP * Q_BLOCK), F32),
                        pltpu.VMEM((NSA_KV_HEADS, 1, NSA_GROUP * Q_BLOCK), F32),
                        pltpu.VMEM((NSA_KV_HEADS, HEAD_DIM, NSA_GROUP * Q_BLOCK), F32)],
        compiler_params=_cparams(("parallel", "arbitrary")),
        name="nsa_prompt",
    )(q_aug, kc_aug, vc_t, ksel.reshape(batch, seq, kw), vsel_t, kwin.reshape(batch, seq, kw), vwin_t, onehot, small, sza,
      selmap_t)


def _softplus(x):
    return jnp.maximum(x, 0.0) + jnp.log1p(jnp.exp(-jnp.abs(x)))


def _gdn_prompt_kernel(qkv_ref, small_ref, szb_ref, cw_ref, cb_ref, alog_ref, dtb_ref, gn_ref,
                       o_ref, sfin_ref, s_ref, prev_ref):
    c = pl.program_id(1)
    ck = GDN_CHUNK
    dk = GDN_DK

    @pl.when(c == 0)
    def _():
        s_ref[...] = jnp.zeros_like(s_ref)
        prev_ref[...] = jnp.zeros_like(prev_ref)

    u = qkv_ref[...]
    ext = jnp.concatenate([prev_ref[...], u], axis=0)
    y = cb_ref[...]
    for i in range(CONV_W):
        y = y + ext[8 - (CONV_W - 1) + i:8 - (CONV_W - 1) + i + ck] * cw_ref[i:i + 1, :]
    prev_ref[...] = u[ck - 8:ck]
    act = _silu(y)

    small = small_ref[...]
    g_all = -jnp.exp(alog_ref[...]) * _softplus(small + dtb_ref[...])
    beta_all = jax.nn.sigmoid(small)
    ii = lax.broadcasted_iota(jnp.int32, (ck, ck), 0)
    jj = lax.broadcasted_iota(jnp.int32, (ck, ck), 1)
    tri = jnp.where(ii >= jj, 1.0, 0.0).astype(F32)
    eye = jnp.where(ii == jj, 1.0, 0.0).astype(F32)
    gc_all = lax.dot_general(tri, g_all, _NN, precision=lax.Precision.HIGHEST, preferred_element_type=F32)
    gc_t = gc_all.T

    heads = range(GDN_HEADS)
    dv = GDN_DV
    q, k, v, beta, gc, decay = [], [], [], [], [], []
    for h in heads:
        qh = act[:, h * dk:(h + 1) * dk]
        kh = act[:, GDN_QK_WIDTH + h * dk:GDN_QK_WIDTH + (h + 1) * dk]
        q.append(qh * lax.rsqrt(jnp.sum(qh * qh, axis=-1, keepdims=True) + EPS) * (dk ** -0.5))
        k.append(kh * lax.rsqrt(jnp.sum(kh * kh, axis=-1, keepdims=True) + EPS))
        v.append(act[:, 2 * GDN_QK_WIDTH + h * dv:2 * GDN_QK_WIDTH + (h + 1) * dv])
        beta.append(beta_all[:, SMALL_B0 + h:SMALL_B0 + h + 1])
        gc.append(gc_all[:, SMALL_A0 + h:SMALL_A0 + h + 1])
        gr = gc_t[SMALL_A0 + h:SMALL_A0 + h + 1, :]
        decay.append(jnp.exp(jnp.where(ii >= jj, gc[h] - gr, NEG)))
    kb = [k[h] * beta[h] for h in heads]
    ks = [_split(k[h]) for h in heads]
    qk = [_dot3s(_split(jnp.concatenate([q[h], kb[h]], axis=0)), ks[h], _NT) for h in heads]
    a_in = [qk[h][0:ck] * decay[h] for h in heads]
    lmat = [jnp.where(ii > jj, qk[h][ck:2 * ck] * decay[h], 0.0) for h in heads]
    x = [eye - lmat[h] for h in heads]
    pw = [_split(lmat[h]) for h in heads]
    for _ in range(int(math.log2(ck)) - 1):
        pw = [_split(_dot3s(pw[h], pw[h])) for h in heads]
        x = [x[h] + _dot3s(_split(x[h]), pw[h]) for h in heads]
    eg = [jnp.exp(gc[h]) for h in heads]
    uw = [_dot3s(_split(x[h]), _split(jnp.concatenate([v[h] * beta[h], kb[h] * eg[h]], axis=1))) for h in heads]
    s_old = [s_ref[h] for h in heads]
    qw_s = [_dot3s(_split(jnp.concatenate([q[h] * eg[h], uw[h][:, dv:dv + dk]], axis=0)), _split(s_old[h])) for h in heads]
    v_new = [uw[h][:, 0:dv] - qw_s[h][ck:2 * ck] for h in heads]
    vns = [_split(v_new[h]) for h in heads]
    o = [qw_s[h][0:ck] + _dot3s(_split(a_in[h]), vns[h]) for h in heads]
    outs = []
    for h in heads:
        g_last = gc[h][ck - 1:ck, :]
        kd = k[h] * jnp.exp(g_last - gc[h])
        s_ref[h] = s_old[h] * jnp.exp(g_last) + _dot3s(_split(kd.T), vns[h])
        outs.append(o[h] * lax.rsqrt(jnp.mean(o[h] * o[h], axis=-1, keepdims=True) + EPS) * gn_ref[...])
    o_ref[...] = jnp.concatenate(outs, axis=1) * szb_ref[...]

    @pl.when(c == pl.num_programs(1) - 1)
    def _():
        sfin_ref[...] = s_ref[...]


def _lane_params(a_log, dt_bias):
    alog_l = jnp.zeros((1, SMALL_W), F32).at[0, SMALL_A0:SMALL_A0 + GDN_HEADS].set(a_log)
    dtb_l = jnp.zeros((1, SMALL_W), F32).at[0, SMALL_A0:SMALL_A0 + GDN_HEADS].set(dt_bias)
    return alog_l, dtb_l


def _gdn_prompt(qkvb, small, szb, conv_w, conv_b, a_log, dt_bias, gnorm, batch):
    n = qkvb.shape[0]
    seq = n // batch
    nc = seq // GDN_CHUNK
    alog_l, dtb_l = _lane_params(a_log, dt_bias)
    row = lambda w: pl.BlockSpec((GDN_CHUNK, w), lambda b, c: (b * nc + c, 0))
    full = lambda a: pl.BlockSpec(a.shape, lambda b, c: (0,) * a.ndim)
    cb = conv_b.reshape(1, -1)
    gn = gnorm.reshape(1, -1)
    return pl.pallas_call(
        _gdn_prompt_kernel,
        grid=(batch, nc),
        in_specs=[row(CONV_DIM), row(SMALL_W), row(GDN_WIDTH), full(conv_w), full(cb), full(alog_l), full(dtb_l), full(gn)],
        out_specs=[row(GDN_WIDTH),
                   pl.BlockSpec((None, GDN_HEADS, GDN_DK, GDN_DV), lambda b, c: (b, 0, 0, 0))],
        out_shape=[jax.ShapeDtypeStruct((n, GDN_WIDTH), F32),
                   jax.ShapeDtypeStruct((batch, GDN_HEADS, GDN_DK, GDN_DV), F32)],
        scratch_shapes=[pltpu.VMEM((GDN_HEADS, GDN_DK, GDN_DV), F32), pltpu.VMEM((8, CONV_DIM), F32)],
        compiler_params=_cparams(("parallel", "arbitrary")),
        name="gdn_prompt",
    )(qkvb, small, szb, conv_w, cb, alog_l, dtb_l, gn)


def _prep_w_in(w_in, d_model):
    sizes = (NSA_WIDTH, 6 * NSA_KV_HEADS * HEAD_DIM, 3 * NSA_HEADS, NSA_WIDTH, CONV_DIM, GDN_HEADS, GDN_HEADS,
             GDN_WIDTH, 2 * d_model)
    pts = np.cumsum(np.array(sizes))[:-1].tolist()
    q_a, kv_a, g_a, z_a, qkv_b, a_b, b_b, z_b, gm = jnp.split(w_in, pts, axis=1)
    small = jnp.concatenate([g_a, a_b, b_b], axis=1)
    small = jnp.pad(small, ((0, 0), (0, SMALL_W - small.shape[1])))
    return jnp.concatenate([q_a, kv_a, z_a, qkv_b, z_b, gm, small], axis=1).astype(BF16)


def _kv_leaf(kv_t):
    b, _, t = kv_t.shape
    return kv_t.reshape(1, b, 2, NSA_KV_HEADS, HEAD_DIM, t).transpose(0, 1, 5, 2, 3, 4)


def _kv_rows_t(cache):
    n, r = cache.shape[:2]
    return cache.transpose(0, 2, 3, 4, 1).reshape(n, KV_ROW, r)


def _prompt_path(x, ada, lw, cw):
    (norm_g, w_bf, offs, conv_w, conv_b, a_log, dt_bias, gnorm, wa, wb, wo, final_g) = lw
    batch, seq, d = x.shape
    n = batch * seq
    x2 = x.reshape(n, d)
    ada3 = ada.reshape(batch, 1, 3 * d)
    tm = 256
    mod = lambda k: pl.BlockSpec((None, 1, d), lambda i: (i * tm // seq, 0, k))
    (q_aug, kvc_t, kvs_t, kvw_t, kvcb, ksel, vsel_t, kwin, vwin_t, sza, qkvb, szb, gms, small) = _inproj(
        x2, ada3, ada3, (mod(1), mod(0)), norm_g.reshape(1, d), w_bf, offs, tm, seq)
    kc_aug, vc_t = _compress_prompt(kvcb, *cw, batch)
    o_a = _nsa_prompt(q_aug, kc_aug, vc_t, ksel, vsel_t, kwin, vwin_t, small, sza, batch)
    o_b, s_new = _gdn_prompt(qkvb, small, szb, conv_w, conv_b, a_log, dt_bias, gnorm, batch)
    tmo = 512
    gate_spec = pl.BlockSpec((None, 1, d), lambda i: (i * tmo // seq, 0, 2))
    y = _outproj(x2, o_a, o_b, gms, ada3, gate_spec, wa, wb, wo, final_g.reshape(1, d), tmo)
    keep = min(WINDOW, seq)
    new_conv = qkvb.reshape(batch, seq, CONV_DIM)[None, :, seq - (CONV_W - 1):]
    return (y.reshape(batch, seq, d), _kv_leaf(kvc_t), _kv_leaf(kvs_t), _kv_leaf(kvw_t[:, :, seq - keep:]), new_conv,
            s_new[None])


def _page_fetch(pt_ref, cache_hbm, buf, sem, npages):
    def copy(seq, slot, j):
        c0 = pl.multiple_of(j * PAGE_SIZE, PAGE_SIZE)
        return pltpu.make_async_copy(cache_hbm.at[pt_ref[seq, j]], buf.at[slot, :, pl.ds(c0, PAGE_SIZE)], sem.at[slot])

    def start(seq, slot):
        def body(j, c):
            copy(seq, slot, j).start()
            return c
        lax.fori_loop(0, npages, body, 0)

    def wait(seq, slot):
        for j in range(npages):
            copy(seq, slot, j).wait()

    return start, wait


def _fetch_this_prefetch_next(start, wait):
    b = pl.program_id(0)
    nb = pl.num_programs(0)

    @pl.when(b == 0)
    def _():
        start(0, 0)

    @pl.when(b + 1 < nb)
    def _():
        start(b + 1, (b + 1) % 2)

    slot = b % 2
    wait(b, slot)
    return slot


def _compress_sample_kernel(pt_ref, cache_hbm, w1_ref, pe_ref, w2_ref, o_ref, buf, sem, x_ref, xc_ref, *, npages, tt):
    start, wait = _page_fetch(pt_ref, cache_hbm, buf, sem, npages)
    slot = _fetch_this_prefetch_next(start, wait)
    past = npages * PAGE_SIZE
    n16 = past // CMP_STRIDE

    nlt = x_ref.shape[0]

    def untranspose(j, c):
        t0 = pl.multiple_of(j * tt, tt)
        x = buf[slot, :, pl.ds(t0, tt)].T
        for i in range(nlt):
            x_ref[i, pl.ds(t0, tt), :] = x[:, i * 128:(i + 1) * 128]
        return c

    lax.fori_loop(0, past // tt, untranspose, 0)

    for s in range(CMP_STRIDE):
        for i in range(nlt):
            c0 = s * KV_ROW + i * 128
            xc_ref[:, c0:c0 + 128] = x_ref[i, pl.ds(s, n16, stride=CMP_STRIDE), :].astype(BF16)
    o_ref[...] = _compress_math(xc_ref[...], w1_ref, pe_ref, w2_ref).astype(BF16)


def _compress_sample(cache_t, page_table, w1big, pebig, w2big):
    nseq, npages = page_table.shape
    past = npages * PAGE_SIZE
    n16 = past // CMP_STRIDE
    tt = min(1024, past)
    grid_spec = pltpu.PrefetchScalarGridSpec(
        num_scalar_prefetch=1,
        grid=(nseq,),
        in_specs=[pl.BlockSpec(memory_space=pl.ANY),
                  pl.BlockSpec(w1big.shape, lambda b, pt: (0, 0, 0)),
                  pl.BlockSpec(pebig.shape, lambda b, pt: (0, 0, 0)),
                  pl.BlockSpec(w2big.shape, lambda b, pt: (0, 0))],
        out_specs=pl.BlockSpec((None, n16, KV_ROW), lambda b, pt: (b, 0, 0)),
        scratch_shapes=[pltpu.VMEM((2, KV_ROW, past), F32), pltpu.SemaphoreType.DMA((2,)),
                        pltpu.VMEM((KV_ROW // 128, past, 128), F32),
                        pltpu.VMEM((n16, CMP_STRIDE * KV_ROW), BF16)],
    )
    return pl.pallas_call(
        functools.partial(_compress_sample_kernel, npages=npages, tt=tt),
        grid_spec=grid_spec,
        out_shape=jax.ShapeDtypeStruct((nseq, n16, KV_ROW), BF16),
        compiler_params=_cparams(("arbitrary",)),
        name="compress_sample",
    )(page_table, cache_t, w1big, pebig, w2big)


def _nsa_sample_kernel(pt_ref, q_ref, kc_ref, cache_hbm, win_ref, nsel_ref, nwin_ref, nwint_ref, small_ref, sza_ref,
                       selmap_ref, onehot_ref, o_ref, wout_ref, buf, sem, *, npages, tn, kt):
    hd = HEAD_DIM
    rt = 8
    start, wait = _page_fetch(pt_ref, cache_hbm, buf, sem, npages)
    slot = _fetch_this_prefetch_next(start, wait)
    past = npages * PAGE_SIZE
    nbuf = win_ref.shape[1]
    n_cmp = kc_ref.shape[0]
    n_blk_lanes = selmap_ref.shape[1]
    q8 = q_ref[...].astype(F32)
    gates = jax.nn.sigmoid(small_ref[...])
    tau = lax.broadcasted_iota(jnp.int32, (rt, 1), 0) % tn
    t_col = past + tau
    t4 = jnp.concatenate([t_col] * NSA_GROUP, axis=0)
    rows = NSA_GROUP * rt
    newcol = lax.broadcasted_iota(jnp.int32, (1, SMALL_W), 1)
    new_dist = t4 - (past + newcol)
    new_ok = (newcol < tn) & (new_dist >= 0)
    zpad = jnp.zeros((SMALL_W - rt, hd), F32)
    groups = range(NSA_KV_HEADS)
    kcols = [slice(g * hd, (g + 1) * hd) for g in groups]
    vcols = [slice((NSA_KV_HEADS + g) * hd, (NSA_KV_HEADS + g + 1) * hd) for g in groups]

    def new_tile(ref, col):
        return jnp.concatenate([ref[:, col], zpad], axis=0).astype(BF16)

    def pos_cols(p0, n):
        pos = (p0 + lax.broadcasted_iota(jnp.int32, (n, 1), 0)).astype(F32)
        hi = jnp.floor(pos * (1.0 / POS_SPLIT))
        return _aug_cols(n, hi, pos - hi * POS_SPLIT)

    qgs, slopes, o_cs, imps = [], [], [], []
    for g in groups:
        kcol, vcol = kcols[g], vcols[g]
        qg = jnp.concatenate([q8[:, (g * NSA_GROUP + h) * AUG:(g * NSA_GROUP + h) * AUG + hd]
                              for h in range(NSA_GROUP)], axis=0).astype(BF16)
        slope = jnp.concatenate([jnp.full((rt, 1), _SLOPES[g * NSA_GROUP + h], F32) for h in range(NSA_GROUP)], axis=0)

        s = _dot(qg, kc_ref[:, kcol], _NT)
        end = lax.broadcasted_iota(jnp.int32, (1, n_cmp), 1) * CMP_STRIDE + (CMP_BLOCK - 1)
        dist = t4 - end
        mask = dist >= 0
        _, p = _softmax_block(s - slope * dist.astype(F32), mask)
        p = p / jnp.maximum(jnp.sum(p, axis=-1, keepdims=True), 1e-30)
        o_c = _dot(p.astype(BF16), kc_ref[:, vcol])
        psum = p[0:rt]
        for h in range(1, NSA_GROUP):
            psum = psum + p[h * rt:(h + 1) * rt]
        for lst, val in ((qgs, qg), (slopes, slope), (o_cs, o_c), (imps, _importance(psum, selmap_ref[...]))):
            lst.append(val)

    ncol = NSA_KV_HEADS * rt
    imp_pad = jnp.concatenate(imps + [jnp.zeros((SMALL_W - ncol, n_blk_lanes), F32)], axis=0)
    tq_row = past + lax.broadcasted_iota(jnp.int32, (1, SMALL_W), 1) % tn
    selm_all = _select_blocks_t(imp_pad.T, tq_row).T

    selms, q_augs, q_fulls = [], [], []
    for g in groups:
        selm = selm_all[g * rt:(g + 1) * rt]
        q_aug = jnp.concatenate([q8[:, (g * NSA_GROUP + h) * AUG:(g * NSA_GROUP + h + 1) * AUG]
                                 for h in range(NSA_GROUP)], axis=0)
        blk_bias = jnp.where(selm > 0.5, 0.0, MASKV)
        q_full = jnp.concatenate([q_aug, jnp.concatenate([blk_bias] * NSA_GROUP, axis=0)], axis=1).astype(BF16)
        for lst, val in ((selms, selm), (q_augs, q_aug), (q_fulls, q_full)):
            lst.append(val)

    def sel_tile(j, carries):
        k0 = pl.multiple_of(j * kt, kt)
        pos = (k0 + lax.broadcasted_iota(jnp.int32, (1, kt), 1)).astype(F32)
        hi = jnp.floor(pos * (1.0 / POS_SPLIT))
        sub = lax.broadcasted_iota(jnp.int32, (AUG - hd, kt), 0)
        pos_rows = jnp.where(sub == 0, hi, jnp.where(sub == 1, pos - hi * POS_SPLIT, 0.0)).astype(BF16)
        aug = jnp.concatenate([pos_rows, onehot_ref[:, pl.ds(k0, kt)]], axis=0)
        out = []
        for g in groups:
            k_t = buf[slot, kcols[g], pl.ds(k0, kt)].astype(BF16)
            v_t = buf[slot, vcols[g], pl.ds(k0, kt)].astype(BF16)
            k_op = jnp.concatenate([k_t, aug], axis=0)
            out.append(_flash_step(carries[g], _dot(q_fulls[g], k_op), v_t, _NT))
        return tuple(out)

    carries = lax.fori_loop(0, past // kt, sel_tile, tuple(_online_init(rows, hd) for _ in groups))

    outs = []
    for g in groups:
        kcol, vcol = kcols[g], vcols[g]
        qg, slope, o_c, selm, q_aug = qgs[g], slopes[g], o_cs[g], selms[g], q_augs[g]
        nb_new = past // SEL_BLOCK
        new_sel = jnp.concatenate([selm[:, nb_new:nb_new + 1]] * NSA_GROUP, axis=0) > 0.5
        kn = jnp.concatenate([jnp.concatenate([nsel_ref[:, kcol], zpad], axis=0), pos_cols(past, SMALL_W)], axis=1)
        s = _dot(q_aug.astype(BF16), kn.astype(BF16), _NT) + jnp.where(new_ok & new_sel, 0.0, MASKV)
        _, l, acc = _flash_step(carries[g], s, new_tile(nsel_ref, vcol))
        o_s = acc / jnp.maximum(l, 1e-30)

        s = _dot(qg, win_ref[kcol, :].astype(BF16))
        dist = t4 - (past - nbuf + lax.broadcasted_iota(jnp.int32, (1, nbuf), 1))
        mask = (dist >= 0) & (dist < WINDOW)
        carry = _online_update(_online_init(rows, hd), s - slope * dist.astype(F32), mask,
                               win_ref[vcol, :].astype(BF16), _NT)
        s = _dot(qg, new_tile(nwin_ref, kcol), _NT)
        _, l, acc = _online_update(carry, s - slope * new_dist.astype(F32), new_ok, new_tile(nwin_ref, vcol))
        o_w = acc / jnp.maximum(l, 1e-30)

        for h in range(NSA_GROUP):
            c = 3 * (g * NSA_GROUP + h)
            r = slice(h * rt, (h + 1) * rt)
            outs.append(gates[:, c:c + 1] * o_c[r] + gates[:, c + 1:c + 2] * o_s[r] + gates[:, c + 2:c + 3] * o_w[r])
    o_ref[...] = jnp.concatenate(outs, axis=1) * sza_ref[...]
    wout_ref[...] = jnp.concatenate([win_ref[:, tn:nbuf], nwint_ref[...]], axis=1)


def _nsa_sample(qs, kc, cache_t, win_t, kvs, kvw, small, sza, page_table, tn):
    nseq, npages = page_table.shape
    past = npages * PAGE_SIZE
    nbuf = win_t.shape[2]
    assert nbuf == WINDOW and 8 % tn == 0 and tn <= SEL_BLOCK
    n_cmp = kc.shape[1]
    n_blk = -(-(past + tn) // SEL_BLOCK)
    n_blk_lanes = -(-n_blk // 128) * 128
    selmap = _sel_map(n_cmp, n_blk, n_blk_lanes)
    kt = min(4096, past)
    key_blk = np.arange(past)[None, :] // SEL_BLOCK
    onehot_t = jnp.asarray(key_blk == np.arange(n_blk_lanes)[:, None], dtype=BF16)
    seq3 = lambda r, w: pl.BlockSpec((None, r, w), lambda b, pt: (b, 0, 0))
    grid_spec = pltpu.PrefetchScalarGridSpec(
        num_scalar_prefetch=1,
        grid=(nseq,),
        in_specs=[seq3(8, NSA_HEADS * AUG), seq3(n_cmp, KV_ROW), pl.BlockSpec(memory_space=pl.ANY), seq3(KV_ROW, nbuf),
                  seq3(8, KV_ROW), seq3(8, KV_ROW), seq3(KV_ROW, tn), seq3(8, SMALL_W), seq3(8, NSA_WIDTH),
                  pl.BlockSpec(selmap.shape, lambda b, pt: (0, 0)),
                  pl.BlockSpec(onehot_t.shape, lambda b, pt: (0, 0))],
        out_specs=[seq3(8, NSA_WIDTH), seq3(KV_ROW, nbuf)],
        scratch_shapes=[pltpu.VMEM((2, KV_ROW, past), F32), pltpu.SemaphoreType.DMA((2,))],
    )
    rep8 = lambda a: jnp.concatenate([a.reshape(nseq, tn, a.shape[-1])] * (8 // tn), axis=1)
    o8, win_new_t = pl.pallas_call(
        functools.partial(_nsa_sample_kernel, npages=npages, tn=tn, kt=kt),
        grid_spec=grid_spec,
        out_shape=[jax.ShapeDtypeStruct((nseq, 8, NSA_WIDTH), F32), jax.ShapeDtypeStruct((nseq, KV_ROW, nbuf), F32)],
        compiler_params=_cparams(("arbitrary",)),
        name="nsa_sample",
    )(page_table, rep8(qs), kc, cache_t, win_t, rep8(kvs), rep8(kvw), kvw.transpose(0, 2, 1), rep8(small), rep8(sza),
      selmap, onehot_t)
    return o8[:, :tn].reshape(nseq * tn, NSA_WIDTH), win_new_t


def _gdn_sample_kernel(eq_ref, ek_ref, ev_ref, cwq_ref, cwk_ref, cwv_ref, cbq_ref, cbk_ref, cbv_ref, small_ref,
                       alog_ref, dtb_ref, szb_ref, gn_ref, s_ref, o_ref, so_ref, qs_ref, ks_ref):
    h = pl.program_id(0)
    tn = o_ref.shape[0]
    dk, dv, nseq = so_ref.shape

    def conv(e_ref, cw_ref, cb_ref, t):
        y = cb_ref[...]
        for i in range(CONV_W):
            y = y + e_ref[t + i] * cw_ref[:, i:i + 1]
        return _silu(y)

    so_ref[...] = s_ref[...]
    neg_rate = -jnp.exp(alog_ref[...])
    for t in range(tn):
        q = conv(eq_ref, cwq_ref, cbq_ref, t)
        k = conv(ek_ref, cwk_ref, cbk_ref, t)
        v = conv(ev_ref, cwv_ref, cbv_ref, t)
        qs_ref[...] = q * lax.rsqrt(jnp.sum(q * q, axis=0, keepdims=True) + EPS) * (dk ** -0.5)
        ks_ref[...] = k * lax.rsqrt(jnp.sum(k * k, axis=0, keepdims=True) + EPS)
        a_in = small_ref[t, pl.ds(SMALL_A0 + h, 1), :]
        b_in = small_ref[t, pl.ds(SMALL_B0 + h, 1), :]
        decay = jnp.exp(neg_rate * _softplus(a_in + dtb_ref[...]))
        beta = jax.nn.sigmoid(b_in)

        def ks_step(i, acc):
            return acc + ks_ref[pl.ds(i, 1), :] * so_ref[i]

        k_s = lax.fori_loop(0, dk, ks_step, jnp.zeros((dv, nseq), F32), unroll=8)
        delta = beta * (v - decay * k_s)

        def upd_step(i, acc):
            s_new = decay * so_ref[i] + ks_ref[pl.ds(i, 1), :] * delta
            so_ref[i] = s_new
            return acc + qs_ref[pl.ds(i, 1), :] * s_new

        o = lax.fori_loop(0, dk, upd_step, jnp.zeros((dv, nseq), F32), unroll=8)
        o = o * lax.rsqrt(jnp.mean(o * o, axis=0, keepdims=True) + EPS) * gn_ref[...]
        o_ref[t] = o * szb_ref[t]


def _gdn_sample(qkvb, small, szb, state_conv, state_gdn, conv_w, conv_b, a_log, dt_bias, gnorm, tn):
    nseq = state_gdn.shape[0]
    ext = jnp.concatenate([state_conv, qkvb.reshape(nseq, tn, CONV_DIM)], axis=1)
    ext_t = ext.transpose(1, 2, 0)
    small_t = small.reshape(nseq, tn, SMALL_W).transpose(1, 2, 0)
    szb_t = szb.reshape(nseq, tn, GDN_WIDTH).transpose(1, 2, 0)
    s_t = state_gdn.transpose(1, 2, 3, 0)
    cw_t = conv_w.T
    cb_t = conv_b.reshape(-1, 1)
    alog_b = jnp.broadcast_to(a_log[:, None, None], (GDN_HEADS, 1, nseq))
    dtb_b = jnp.broadcast_to(dt_bias[:, None, None], (GDN_HEADS, 1, nseq))
    gn = gnorm.reshape(-1, 1)
    nqk = GDN_QK_WIDTH // GDN_DK
    chan = lambda off: pl.BlockSpec((CONV_W - 1 + tn, GDN_DK, nseq), lambda h: (0, off + h, 0))
    cwb = lambda off: pl.BlockSpec((GDN_DK, CONV_W), lambda h: (off + h, 0))
    cbb = lambda off: pl.BlockSpec((GDN_DK, 1), lambda h: (off + h, 0))
    perhead = pl.BlockSpec((None, 1, nseq), lambda h: (h, 0, 0))
    o_t, s_new = pl.pallas_call(
        _gdn_sample_kernel,
        grid=(GDN_HEADS,),
        in_specs=[chan(0), chan(nqk), chan(2 * nqk), cwb(0), cwb(nqk), cwb(2 * nqk), cbb(0), cbb(nqk), cbb(2 * nqk),
                  pl.BlockSpec((tn, SMALL_W, nseq), lambda h: (0, 0, 0)), perhead, perhead,
                  pl.BlockSpec((tn, GDN_DV, nseq), lambda h: (0, h, 0)),
                  pl.BlockSpec((GDN_DV, 1), lambda h: (0, 0)),
                  pl.BlockSpec((None, GDN_DK, GDN_DV, nseq), lambda h: (h, 0, 0, 0))],
        out_specs=[pl.BlockSpec((tn, GDN_DV, nseq), lambda h: (0, h, 0)),
                   pl.BlockSpec((None, GDN_DK, GDN_DV, nseq), lambda h: (h, 0, 0, 0))],
        out_shape=[jax.ShapeDtypeStruct((tn, GDN_WIDTH, nseq), F32),
                   jax.ShapeDtypeStruct((GDN_HEADS, GDN_DK, GDN_DV, nseq), F32)],
        scratch_shapes=[pltpu.VMEM((GDN_DK, nseq), F32), pltpu.VMEM((GDN_DK, nseq), F32)],
        compiler_params=_cparams(("parallel",)),
        name="gdn_sample",
    )(ext_t, ext_t, ext_t, cw_t, cw_t, cw_t, cb_t, cb_t, cb_t, small_t, alog_b, dtb_b, szb_t, gn, s_t)
    o_b = o_t.transpose(2, 0, 1).reshape(nseq * tn, GDN_WIDTH)
    return o_b, ext[:, tn:], s_new.transpose(3, 0, 1, 2)


def _sample_path(x, ada, lw, cw, cache_cmp, cache_sel, cache_win, state_conv, state_gdn, page_table):
    (norm_g, w_bf, offs, conv_w, conv_b, a_log, dt_bias, gnorm, wa, wb, wo, final_g) = lw
    nseq, tn, d = x.shape
    n = nseq * tn
    x2 = x.reshape(n, d)
    ada_rows = jnp.repeat(ada, tn, axis=0)
    tm = min(256, n)
    mod = lambda k: pl.BlockSpec((tm, d), lambda i: (i, k))
    (qs, kvc_t, kvs_t, kvw_t, _, _, _, _, _, sza, qkvb, szb, gms, small) = _inproj(
        x2, ada_rows, ada_rows, (mod(1), mod(0)), norm_g.reshape(1, d), w_bf, offs, tm, n)
    rows = lambda a_t: a_t.reshape(KV_ROW, nseq, tn).transpose(1, 2, 0)
    kvc, kvs, kvw = rows(kvc_t), rows(kvs_t), rows(kvw_t)
    kc = _compress_sample(_kv_rows_t(cache_cmp), page_table, *cw)
    o_a, win_new_t = _nsa_sample(qs, kc, _kv_rows_t(cache_sel), _kv_rows_t(cache_win), kvs, kvw, small, sza, page_table, tn)
    o_b, conv_new, s_new = _gdn_sample(qkvb, small, szb, state_conv, state_gdn, conv_w, conv_b, a_log, dt_bias, gnorm, tn)
    y = _outproj(x2, o_a, o_b, gms, ada_rows, mod(2), wa, wb, wo, final_g.reshape(1, d), tm)
    kvshape = (1, nseq, tn, 2, NSA_KV_HEADS, HEAD_DIM)
    return (y.reshape(nseq, tn, d), kvc.reshape(kvshape), kvs.reshape(kvshape), _kv_leaf(win_new_t),
            conv_new[None], s_new[None])


def kernel(x_prompt, x_sample, cache_cmp_kv, cache_sel_kv, cache_win_kv, state_conv, state_gdn, page_table, c_prompt, c_sample, norm_g, w_ada, b_ada, w_in, cmp_pe_k, cmp_w1_k, cmp_w2_k, cmp_pe_v, cmp_w1_v, cmp_w2_v, conv_w, conv_b, gdn_a_log, gdn_dt_bias, gdn_norm_g, w_o_nsa, w_o_gdn, w_out, final_g):
    assert norm_g.shape[0] == 1, "single trunk layer"
    d = x_prompt.shape[-1]
    l = 0
    offs, _ = _seg_offsets(d)
    w_bf = _prep_w_in(w_in[l], d)
    ada = _ada(jnp.concatenate([c_prompt, c_sample], axis=0), w_ada[l].astype(BF16), b_ada[l])
    cw = _compress_weights(cmp_pe_k[l], cmp_w1_k[l], cmp_w2_k[l], cmp_pe_v[l], cmp_w1_v[l], cmp_w2_v[l])
    lw = (norm_g[l], w_bf, offs, conv_w[l], conv_b[l], gdn_a_log[l], gdn_dt_bias[l], gdn_norm_g[l],
          w_o_nsa[l].astype(BF16), w_o_gdn[l].astype(BF16), w_out[l].astype(BF16), final_g)
    nb = c_prompt.shape[0]
    yp, cmp_p, sel_p, win_p, conv_p, gdn_p = _prompt_path(x_prompt, ada[:nb], lw, cw)
    ys, cmp_s, sel_s, win_s, conv_s, gdn_s = _sample_path(
        x_sample, ada[nb:], lw, cw, cache_cmp_kv[l], cache_sel_kv[l], cache_win_kv[l], state_conv[l], state_gdn[l],
        page_table)
    return (yp, ys, cmp_p, sel_p, win_p, conv_p, gdn_p, cmp_s, sel_s, win_s, conv_s, gdn_s)
```

```python
import functools
import math

import numpy as np
import jax
import jax.numpy as jnp
from jax import lax
from jax.experimental import pallas as pl
from jax.experimental.pallas import tpu as pltpu

F32 = jnp.float32
BF16 = jnp.bfloat16

NSA_HEADS = 8
NSA_KV_HEADS = 2
NSA_GROUP = NSA_HEADS // NSA_KV_HEADS
HEAD_DIM = 64
CMP_BLOCK = 32
CMP_STRIDE = 16
CMP_HIDDEN = 64
SEL_BLOCK = 64
N_SEL = 16
WINDOW = 512
Q_BLOCK = 128
GDN_HEADS = 8
GDN_DK = 64
GDN_DV = 64
CONV_W = 4
GDN_CHUNK = 64
PAGE_SIZE = 128
EPS = 1e-6

NSA_WIDTH = NSA_HEADS * HEAD_DIM
KV_ROW = 2 * NSA_KV_HEADS * HEAD_DIM
GDN_QK_WIDTH = GDN_HEADS * GDN_DK
GDN_WIDTH = GDN_HEADS * GDN_DV
CONV_DIM = 2 * GDN_QK_WIDTH + GDN_WIDTH
SMALL_W = 128
SMALL_A0 = 3 * NSA_HEADS
SMALL_B0 = SMALL_A0 + GDN_HEADS

NEG = -1e30
MASKV = -(2.0 ** 100)
AUG = 128
POS_SPLIT = 128
VMEM_LIMIT = 56 * 1024 * 1024

_NT = (((1,), (1,)), ((), ()))
_NN = (((1,), (0,)), ((), ()))


def _alibi_slopes():
    h = np.arange(1, NSA_HEADS + 1, dtype=np.float32)
    return [float(v) for v in np.power(np.float32(2.0), -np.float32(8.0) * h / np.float32(NSA_HEADS))]


_SLOPES = _alibi_slopes()


def _dot(a, b, dims=_NN):
    return lax.dot_general(a, b, dims, preferred_element_type=F32)


def _split(a):
    hi = a.astype(BF16)
    lo = (a - hi.astype(F32)).astype(BF16)
    return hi, lo


def _dot3s(a_split, b_split, dims=_NN):
    (ah, al), (bh, bl) = a_split, b_split
    return _dot(ah, bh, dims) + _dot(ah, bl, dims) + _dot(al, bh, dims)


def _silu(x):
    return x * jax.nn.sigmoid(x)


def _cparams(sem):
    return pltpu.CompilerParams(dimension_semantics=sem, vmem_limit_bytes=VMEM_LIMIT)


def _ada_kernel(c_ref, w_ref, b_ref, o_ref):
    sc = _silu(c_ref[...]).astype(BF16)
    o_ref[...] = _dot(sc, w_ref[...]) + b_ref[...]


def _ada(c, w_bf, b):
    n, d = c.shape
    n_pad = -(-n // 8) * 8
    c = jnp.pad(c, ((0, n_pad - n), (0, 0)))
    out = pl.pallas_call(
        _ada_kernel,
        out_shape=jax.ShapeDtypeStruct((n_pad, w_bf.shape[1]), F32),
        name="ada",
    )(c, w_bf, b.reshape(1, -1))
    return out[:n]


_SEG = (("q", NSA_WIDTH), ("kvc", KV_ROW), ("kvs", KV_ROW), ("kvw", KV_ROW), ("za", NSA_WIDTH),
        ("qkvb", CONV_DIM), ("zb", GDN_WIDTH), ("gm", None), ("small", SMALL_W))


def _seg_offsets(d_model):
    offs, c = {}, 0
    for name, n in _SEG:
        n = 2 * d_model if n is None else n
        offs[name] = (c, n)
        c += n
    return offs, c


def _aug_cols(rows, c0, c1):
    lane = lax.broadcasted_iota(jnp.int32, (rows, AUG - HEAD_DIM), 1)
    return jnp.where(lane == 0, c0, jnp.where(lane == 1, c1, 0.0)).astype(F32)


def _aug_keys(kv, pos):
    hi = jnp.floor(pos * (1.0 / POS_SPLIT))
    cols = _aug_cols(kv.shape[0], hi, pos - hi * POS_SPLIT)
    parts = []
    for g in range(NSA_KV_HEADS):
        parts += [kv[:, g * HEAD_DIM:(g + 1) * HEAD_DIM], cols]
    k_aug = jnp.concatenate(parts, axis=1).astype(BF16)
    kv_t = kv.T
    return k_aug, kv_t[NSA_KV_HEADS * HEAD_DIM:].astype(BF16), kv_t


def _inproj_kernel(x_ref, scale_ref, shift_ref, ng_ref, w_ref,
                   q_ref, kvct_ref, kvst_ref, kvwt_ref, kvcb_ref, ksel_ref, vselt_ref, kwin_ref, vwint_ref,
                   sza_ref, qkvb_ref, szb_ref, gms_ref, small_ref, *, offs, seq):
    x = x_ref[...]
    tm = x.shape[0]
    y = x * lax.rsqrt(jnp.mean(x * x, axis=-1, keepdims=True) + EPS) * ng_ref[...]
    h = y * (1.0 + scale_ref[...]) + shift_ref[...]
    hb = h.astype(BF16)

    def seg(name):
        c0, n = offs[name]
        return _dot(hb, w_ref[:, c0:c0 + n])

    qv = seg("q") * (HEAD_DIM ** -0.5)
    parts = []
    for hh in range(NSA_HEADS):
        parts += [qv[:, hh * HEAD_DIM:(hh + 1) * HEAD_DIM], _aug_cols(tm, _SLOPES[hh] * POS_SPLIT, _SLOPES[hh])]
    q_ref[...] = jnp.concatenate(parts, axis=1).astype(BF16)

    pos = ((pl.program_id(0) * tm) % seq + lax.broadcasted_iota(jnp.int32, (tm, 1), 0)).astype(F32)
    v = seg("kvc")
    kvct_ref[...] = v.T
    kvcb_ref[...] = v.astype(BF16)
    for name, f_ref, k_ref, vt_ref in (("kvs", kvst_ref, ksel_ref, vselt_ref), ("kvw", kvwt_ref, kwin_ref, vwint_ref)):
        k_ref[...], vt_ref[...], f_ref[...] = _aug_keys(seg(name), pos)
    sza_ref[...] = _silu(seg("za"))
    qkvb_ref[...] = seg("qkvb")
    szb_ref[...] = _silu(seg("zb"))
    gms_ref[...] = jax.nn.sigmoid(seg("gm"))
    small_ref[...] = seg("small")


def _inproj(x2, scale_arr, shift_arr, mod_specs, ng, w_bf, offs, tm, seq):
    n, d = x2.shape
    wtot = w_bf.shape[1]
    nbatch = n // seq
    spt = seq // tm
    vw = NSA_KV_HEADS * HEAD_DIM
    row = lambda w: pl.BlockSpec((tm, w), lambda i: (i, 0))
    rows = lambda w, dt: (jax.ShapeDtypeStruct((n, w), dt), row(w))
    tr = lambda w, dt: (jax.ShapeDtypeStruct((nbatch, w, seq), dt),
                        pl.BlockSpec((None, w, tm), lambda i: (i // spt, 0, i % spt)))
    vt = tr(vw, BF16)
    kvt = tr(KV_ROW, F32)
    outs = [rows(NSA_HEADS * AUG, BF16), kvt, kvt, kvt, rows(KV_ROW, BF16),
            rows(NSA_KV_HEADS * AUG, BF16), vt, rows(NSA_KV_HEADS * AUG, BF16), vt,
            rows(offs["za"][1], F32), rows(offs["qkvb"][1], F32), rows(offs["zb"][1], F32), rows(offs["gm"][1], F32),
            rows(offs["small"][1], F32)]
    out_shape = [o[0] for o in outs]
    out_specs = [o[1] for o in outs]
    return pl.pallas_call(
        functools.partial(_inproj_kernel, offs=offs, seq=seq),
        grid=(n // tm,),
        in_specs=[row(d), mod_specs[0], mod_specs[1],
                  pl.BlockSpec((1, d), lambda i: (0, 0)),
                  pl.BlockSpec((d, wtot), lambda i: (0, 0))],
        out_specs=out_specs,
        out_shape=out_shape,
        compiler_params=_cparams(("parallel",)),
        name="inproj",
    )(x2, scale_arr, shift_arr, ng, w_bf)


def _outproj_kernel(x_ref, oa_ref, ob_ref, gms_ref, gate_ref, wa_ref, wb_ref, wo_ref, fg_ref, y_ref, *, d):
    ma = _dot(oa_ref[...].astype(BF16), wa_ref[...])
    mb = _dot(ob_ref[...].astype(BF16), wb_ref[...])
    m = gms_ref[:, 0:d] * ma + gms_ref[:, d:2 * d] * mb
    y = x_ref[...] + gate_ref[...] * _dot(m.astype(BF16), wo_ref[...])
    y_ref[...] = y * lax.rsqrt(jnp.mean(y * y, axis=-1, keepdims=True) + EPS) * fg_ref[...]


def _outproj(x2, oa, ob, gms, gate_arr, gate_spec, wa, wb, wo, fg, tm):
    n, d = x2.shape
    row = lambda w: pl.BlockSpec((tm, w), lambda i: (i, 0))
    full = lambda a: pl.BlockSpec(a.shape, lambda i: (0, 0))
    return pl.pallas_call(
        functools.partial(_outproj_kernel, d=d),
        grid=(n // tm,),
        in_specs=[row(d), row(oa.shape[1]), row(ob.shape[1]), row(2 * d), gate_spec,
                  full(wa), full(wb), full(wo), full(fg)],
        out_specs=row(d),
        out_shape=jax.ShapeDtypeStruct((n, d), F32),
        compiler_params=_cparams(("parallel",)),
        name="outproj",
    )(x2, oa, ob, gms, gate_arr, wa, wb, wo, fg)


def _compress_math(x, w1_ref, pe_ref, w2_ref):
    n16 = x.shape[0]
    a0 = _dot(x, w1_ref[0])
    a1 = _dot(x, w1_ref[1])
    p0 = _dot(pe_ref[0], w1_ref[0]) + _dot(pe_ref[1], w1_ref[1])
    pre = a0 + pltpu.roll(a1, n16 - 1, 0) + p0[0:1]
    return _dot(_silu(pre).astype(BF16), w2_ref[...])


def _compress_kernel(x_ref, w1_ref, pe_ref, w2_ref, k_ref, vt_ref):
    kv = _compress_math(x_ref[...], w1_ref, pe_ref, w2_ref)
    n16 = kv.shape[0]
    end = (lax.broadcasted_iota(jnp.int32, (n16, 1), 0) * CMP_STRIDE + (CMP_BLOCK - 1)).astype(F32)
    k_ref[...], vt_ref[...], _ = _aug_keys(kv, end)


def _compress_prompt(kvcb, w1big, pebig, w2big, batch):
    n = kvcb.shape[0]
    n16 = n // batch // CMP_STRIDE
    x = kvcb.reshape(batch, n16, CMP_STRIDE * KV_ROW)
    vw = NSA_KV_HEADS * HEAD_DIM
    return pl.pallas_call(
        _compress_kernel,
        grid=(batch,),
        in_specs=[pl.BlockSpec((None, n16, CMP_STRIDE * KV_ROW), lambda b: (b, 0, 0)),
                  pl.BlockSpec(w1big.shape, lambda b: (0, 0, 0)),
                  pl.BlockSpec(pebig.shape, lambda b: (0, 0, 0)),
                  pl.BlockSpec(w2big.shape, lambda b: (0, 0))],
        out_specs=[pl.BlockSpec((None, n16, NSA_KV_HEADS * AUG), lambda b: (b, 0, 0)),
                   pl.BlockSpec((None, vw, n16), lambda b: (b, 0, 0))],
        out_shape=[jax.ShapeDtypeStruct((batch, n16, NSA_KV_HEADS * AUG), BF16),
                   jax.ShapeDtypeStruct((batch, vw, n16), BF16)],
        compiler_params=_cparams(("parallel",)),
        name="compress_prompt",
    )(x, w1big, pebig, w2big)


def _compress_weights(pe_k, w1_k, w2_k, pe_v, w1_v, w2_v):
    r_cnt = CMP_BLOCK // CMP_STRIDE
    g = NSA_KV_HEADS
    eye2 = jnp.eye(2, dtype=F32)
    eyeg = jnp.eye(g, dtype=F32)
    w1 = jnp.stack([w1_k, w1_v]).reshape(2, r_cnt, CMP_STRIDE, HEAD_DIM, CMP_HIDDEN)
    w1big = jnp.einsum("krsde,kK,gG->rskgdKGe", w1, eye2, eyeg)
    w1big = w1big.reshape(r_cnt, CMP_STRIDE * KV_ROW, 2 * g * CMP_HIDDEN).astype(BF16)
    pe = jnp.stack([pe_k, pe_v]).reshape(2, r_cnt, CMP_STRIDE, HEAD_DIM)
    pebig = jnp.broadcast_to(pe.transpose(1, 2, 0, 3)[:, :, :, None, :], (r_cnt, CMP_STRIDE, 2, g, HEAD_DIM))
    pebig = pebig.reshape(r_cnt, 1, CMP_STRIDE * KV_ROW)
    pebig = jnp.pad(pebig, ((0, 0), (0, 7), (0, 0))).astype(BF16)
    w2 = jnp.stack([w2_k, w2_v])
    w2big = jnp.einsum("ked,kK,gG->kgeKGd", w2, eye2, eyeg).reshape(2 * g * CMP_HIDDEN, KV_ROW).astype(BF16)
    return w1big, pebig, w2big


def _sel_map(n_cmp_rows, n_blk, n_blk_pad):
    i = np.arange(n_cmp_rows)[:, None] * CMP_STRIDE
    j = np.arange(n_blk_pad)[None, :] * SEL_BLOCK
    ov = np.minimum(i + CMP_BLOCK, j + SEL_BLOCK) - np.maximum(i, j)
    m = np.clip(ov, 0, None).astype(np.float32) / np.float32(CMP_BLOCK)
    m[:, n_blk:] = 0.0
    return jnp.asarray(m, dtype=BF16)


def _softmax_block(s, mask):
    s = jnp.where(mask, s, NEG)
    m = jnp.max(s, axis=-1, keepdims=True)
    p = jnp.where(mask, jnp.exp(s - m), 0.0)
    return m, p


def _online_update(carry, s, mask, v_bf, v_dims=_NN):
    m, l, acc = carry
    s = jnp.where(mask, s, NEG)
    m_new = jnp.maximum(m, jnp.max(s, axis=-1, keepdims=True))
    alpha = jnp.exp(m - m_new)
    p = jnp.where(mask, jnp.exp(s - m_new), 0.0)
    l = alpha * l + jnp.sum(p, axis=-1, keepdims=True)
    acc = alpha * acc + _dot(p.astype(BF16), v_bf, v_dims)
    return m_new, l, acc


def _flash_step(carry, s, v_bf, v_dims=_NN):
    m, l, acc = carry
    m_new = jnp.maximum(m, jnp.max(s, axis=-1, keepdims=True))
    alpha = jnp.exp(m - m_new)
    p = jnp.exp(s - m_new)
    return (m_new, alpha * l + jnp.sum(p, axis=-1, keepdims=True),
            alpha * acc + _dot(p.astype(BF16), v_bf, v_dims))


def _online_init(rows, hd):
    return (jnp.full((rows, 1), NEG, F32), jnp.zeros((rows, 1), F32), jnp.zeros((rows, hd), F32))


def _importance(p_sum, selmap):
    ph, pl_ = _split(p_sum)
    pl2 = (p_sum - ph.astype(F32) - pl_.astype(F32)).astype(BF16)
    return _dot(ph, selmap) + _dot(pl_, selmap) + _dot(pl2, selmap)


def _flash_step_t(carry, s_t, v_t):
    m, l, acc = carry
    m_new = jnp.maximum(m, jnp.max(s_t, axis=0, keepdims=True))
    alpha = jnp.exp(m - m_new)
    p = jnp.exp(s_t - m_new)
    l = alpha * l + jnp.sum(p, axis=0, keepdims=True)
    acc = alpha * acc + _dot(v_t, p.astype(BF16))
    return m_new, l, acc


def _flash_init_t(cols, dv):
    return (jnp.full((1, cols), NEG, F32), jnp.zeros((1, cols), F32), jnp.zeros((dv, cols), F32))


def _select_blocks_t(imp_t, tq):
    nb, nq = imp_t.shape
    jj = lax.broadcasted_iota(jnp.int32, (nb, nq), 0)
    cur = tq // SEL_BLOCK
    forced = (jj == 0) | (jj == cur) | (jj == cur - 1)
    jf = jj.astype(F32)
    score = jnp.where(forced, NEG, jnp.where(jj <= cur, imp_t, NEG))
    selm = jnp.where(forced, 1.0, 0.0).astype(F32)
    for _ in range(N_SEL - 3):
        m = jnp.max(score, axis=0, keepdims=True)
        idx = jnp.min(jnp.where(score == m, jf, float(nb)), axis=0, keepdims=True)
        hit = jnp.where(m > 0.5 * NEG, idx, -1.0) == jf
        selm = jnp.where(hit, 1.0, selm)
        score = jnp.where(hit, NEG, score)
    return selm


def _nsa_prompt_kernel(q_ref, kc_ref, vct_ref, ksel_ref, vselt_ref, kwin_ref, vwint_ref, onehot_ref, small_ref, sza_ref,
                       selmapt_ref, o_ref, flags_ref, m_ref, l_ref, acc_ref, *, kt):
    qb = Q_BLOCK
    hd = HEAD_DIM
    q0 = pl.program_id(1) * qb
    n_cmp = kc_ref.shape[0]
    cols = NSA_GROUP * qb
    tq = q0 + lax.broadcasted_iota(jnp.int32, (1, qb), 1)
    gates_t = jax.nn.sigmoid(small_ref[...]).T
    wlen = WINDOW + qb
    w0 = pl.multiple_of(jnp.maximum(q0 - WINDOW, 0), qb)
    tile4 = lambda a: jnp.concatenate([a] * NSA_GROUP, axis=1)

    def keymask(pos_col, lo):
        dist = tq - pos_col
        ok = (dist >= 0) if lo is None else ((dist >= 0) & (dist < lo))
        return tile4(jnp.where(ok, 0.0, MASKV).astype(F32))

    cmp_bias = keymask(lax.broadcasted_iota(jnp.int32, (n_cmp, 1), 0) * CMP_STRIDE + (CMP_BLOCK - 1), None)
    win_bias = keymask(w0 + lax.broadcasted_iota(jnp.int32, (wlen, 1), 0), WINDOW)
    jlast = q0 // kt
    klast = pl.multiple_of(jlast * kt, kt)
    diag_bias = keymask(klast + lax.broadcasted_iota(jnp.int32, (kt, 1), 0), None)

    groups = range(NSA_KV_HEADS)
    kcols = [slice(g * AUG, (g + 1) * AUG) for g in groups]
    vrows = [slice(g * hd, (g + 1) * hd) for g in groups]
    q_heads = [[q_ref[:, (g * NSA_GROUP + h) * AUG:(g * NSA_GROUP + h + 1) * AUG] for h in range(NSA_GROUP)]
               for g in groups]
    q_cat = [jnp.concatenate(q_heads[g], axis=0) for g in groups]
    finish = lambda carry: carry[2] * (1.0 / jnp.maximum(carry[1], 1e-30))

    o_w = [finish(_flash_step_t(_flash_init_t(cols, hd), _dot(kwin_ref[pl.ds(w0, wlen), kcols[g]], q_cat[g], _NT) + win_bias,
                                vwint_ref[vrows[g], pl.ds(w0, wlen)])) for g in groups]

    o_c, q_full = [], []
    sm_t = selmapt_ref[...]
    for g in groups:
        s_t = _dot(kc_ref[:, kcols[g]], q_cat[g], _NT) + cmp_bias
        m = jnp.maximum(jnp.max(s_t, axis=0, keepdims=True), NEG)
        p = jnp.exp(s_t - m)
        p = p * (1.0 / jnp.maximum(jnp.sum(p, axis=0, keepdims=True), 1e-30))
        o_c.append(_dot(vct_ref[vrows[g], :], p.astype(BF16)))
        psum = p[:, 0:qb]
        for h in range(1, NSA_GROUP):
            psum = psum + p[:, h * qb:(h + 1) * qb]
        ph, pl_ = _split(psum)
        pl2 = (psum - ph.astype(F32) - pl_.astype(F32)).astype(BF16)
        imp_t = _dot(sm_t, ph) + _dot(sm_t, pl_) + _dot(sm_t, pl2)
        selm = _select_blocks_t(imp_t, tq).T
        blk_bias = jnp.where(selm > 0.5, 0.0, MASKV).astype(BF16)
        q_full.append(jnp.concatenate([jnp.concatenate([qh, blk_bias], axis=1) for qh in q_heads[g]], axis=0))
        blk_any = jnp.max(selm, axis=0, keepdims=True)
        bpt = kt // SEL_BLOCK
        for j in range(flags_ref.shape[1]):
            flags_ref[g, j] = (jnp.max(blk_any[:, j * bpt:(j + 1) * bpt]) > 0.5).astype(jnp.int32)
        for ref, val in zip((m_ref, l_ref, acc_ref), _flash_init_t(cols, hd)):
            ref[g] = val

    def sel_step(k0, g, bias):
        k_op = jnp.concatenate([ksel_ref[pl.ds(k0, kt), kcols[g]], onehot_ref[pl.ds(k0, kt), :]], axis=1)
        s_t = _dot(k_op, q_full[g], _NT)
        s_t = s_t if bias is None else s_t + bias
        m, l, acc = _flash_step_t((m_ref[g], l_ref[g], acc_ref[g]), s_t, vselt_ref[vrows[g], pl.ds(k0, kt)])
        m_ref[g], l_ref[g], acc_ref[g] = m, l, acc

    def sel_tile(j, c):
        for g in groups:
            @pl.when(flags_ref[g, j] > 0)
            def _():
                sel_step(pl.multiple_of(j * kt, kt), g, None)
        return c

    lax.fori_loop(0, jlast, sel_tile, 0)
    o_s = []
    for g in groups:
        sel_step(klast, g, diag_bias)
        o_s.append(finish((m_ref[g], l_ref[g], acc_ref[g])))

    outs = []
    for g in groups:
        for h in range(NSA_GROUP):
            c = 3 * (g * NSA_GROUP + h)
            r = slice(h * qb, (h + 1) * qb)
            outs.append(gates_t[c:c + 1, :] * o_c[g][:, r] + gates_t[c + 1:c + 2, :] * o_s[g][:, r]
                        + gates_t[c + 2:c + 3, :] * o_w[g][:, r])
    o_ref[...] = jnp.concatenate(outs, axis=0).T * sza_ref[...]


def _nsa_prompt(q_aug, kc_aug, vc_t, ksel, vsel_t, kwin, vwin_t, small, sza, batch):
    n = q_aug.shape[0]
    seq = n // batch
    nq = seq // Q_BLOCK
    n_cmp = kc_aug.shape[1]
    n_blk = -(-seq // SEL_BLOCK)
    assert n_blk <= AUG and seq % Q_BLOCK == 0 and seq >= WINDOW + Q_BLOCK
    kt = min(512, seq)
    selmap_t = _sel_map(n_cmp, n_blk, AUG).T
    pos = np.arange(seq)[:, None] // SEL_BLOCK
    onehot = jnp.asarray(pos == np.arange(AUG)[None, :], dtype=BF16)
    vw = NSA_KV_HEADS * HEAD_DIM
    kw = NSA_KV_HEADS * AUG
    row = lambda w: pl.BlockSpec((Q_BLOCK, w), lambda b, i: (b * nq + i, 0))
    kspec = pl.BlockSpec((None, seq, kw), lambda b, i: (b, 0, 0))
    vspec = pl.BlockSpec((None, vw, seq), lambda b, i: (b, 0, 0))
    return pl.pallas_call(
        functools.partial(_nsa_prompt_kernel, kt=kt),
        grid=(batch, nq),
        in_specs=[row(NSA_HEADS * AUG),
                  pl.BlockSpec((None, n_cmp, kw), lambda b, i: (b, 0, 0)),
                  pl.BlockSpec((None, vw, n_cmp), lambda b, i: (b, 0, 0)),
                  kspec, vspec, kspec, vspec,
                  pl.BlockSpec(onehot.shape, lambda b, i: (0, 0)),
                  row(SMALL_W), row(NSA_WIDTH),
                  pl.BlockSpec(selmap_t.shape, lambda b, i: (0, 0))],
        out_specs=row(NSA_WIDTH),
        out_shape=jax.ShapeDtypeStruct((n, NSA_WIDTH), F32),
        scratch_shapes=[pltpu.SMEM((NSA_KV_HEADS, seq // kt), jnp.int32),
                        pltpu.VMEM((NSA_KV_HEADS, 1, NSA_GROUP * Q_BLOCK), F32),
                        pltpu.VMEM((NSA_KV_HEADS, 1, NSA_GROUP * Q_BLOCK), F32),
                        pltpu.VMEM((NSA_KV_HEADS, HEAD_DIM, NSA_GROUP * Q_BLOCK), F32)],
        compiler_params=_cparams(("parallel", "arbitrary")),
        name="nsa_prompt",
    )(q_aug, kc_aug, vc_t, ksel.reshape(batch, seq, kw), vsel_t, kwin.reshape(batch, seq, kw), vwin_t, onehot, small, sza,
      selmap_t)


def _softplus(x):
    return jnp.maximum(x, 0.0) + jnp.log1p(jnp.exp(-jnp.abs(x)))


def _gdn_prompt_kernel(qkv_ref, small_ref, szb_ref, cw_ref, cb_ref, alog_ref, dtb_ref, gn_ref,
                       o_ref, sfin_ref, s_ref, prev_ref):
    c = pl.program_id(1)
    ck = GDN_CHUNK
    dk = GDN_DK

    @pl.when(c == 0)
    def _():
        s_ref[...] = jnp.zeros_like(s_ref)
        prev_ref[...] = jnp.zeros_like(prev_ref)

    u = qkv_ref[...]
    ext = jnp.concatenate([prev_ref[...], u], axis=0)
    y = cb_ref[...]
    for i in range(CONV_W):
        y = y + ext[8 - (CONV_W - 1) + i:8 - (CONV_W - 1) + i + ck] * cw_ref[i:i + 1, :]
    prev_ref[...] = u[ck - 8:ck]
    act = _silu(y)

    small = small_ref[...]
    g_all = -jnp.exp(alog_ref[...]) * _softplus(small + dtb_ref[...])
    beta_all = jax.nn.sigmoid(small)
    ii = lax.broadcasted_iota(jnp.int32, (ck, ck), 0)
    jj = lax.broadcasted_iota(jnp.int32, (ck, ck), 1)
    tri = jnp.where(ii >= jj, 1.0, 0.0).astype(F32)
    eye = jnp.where(ii == jj, 1.0, 0.0).astype(F32)
    gc_all = lax.dot_general(tri, g_all, _NN, precision=lax.Precision.HIGHEST, preferred_element_type=F32)
    gc_t = gc_all.T

    heads = range(GDN_HEADS)
    dv = GDN_DV
    q, k, v, beta, gc, decay = [], [], [], [], [], []
    for h in heads:
        qh = act[:, h * dk:(h + 1) * dk]
        kh = act[:, GDN_QK_WIDTH + h * dk:GDN_QK_WIDTH + (h + 1) * dk]
        q.append(qh * lax.rsqrt(jnp.sum(qh * qh, axis=-1, keepdims=True) + EPS) * (dk ** -0.5))
        k.append(kh * lax.rsqrt(jnp.sum(kh * kh, axis=-1, keepdims=True) + EPS))
        v.append(act[:, 2 * GDN_QK_WIDTH + h * dv:2 * GDN_QK_WIDTH + (h + 1) * dv])
        beta.append(beta_all[:, SMALL_B0 + h:SMALL_B0 + h + 1])
        gc.append(gc_all[:, SMALL_A0 + h:SMALL_A0 + h + 1])
        gr = gc_t[SMALL_A0 + h:SMALL_A0 + h + 1, :]
        decay.append(jnp.exp(jnp.where(ii >= jj, gc[h] - gr, NEG)))
    kb = [k[h] * beta[h] for h in heads]
    ks = [_split(k[h]) for h in heads]
    qk = [_dot3s(_split(jnp.concatenate([q[h], kb[h]], axis=0)), ks[h], _NT) for h in heads]
    a_in = [qk[h][0:ck] * decay[h] for h in heads]
    lmat = [jnp.where(ii > jj, qk[h][ck:2 * ck] * decay[h], 0.0) for h in heads]
    x = [eye - lmat[h] for h in heads]
    pw = [_split(lmat[h]) for h in heads]
    for it in range(int(math.log2(ck)) - 1):
        if it < 3:
            pw = [_split(_dot3s(pw[h], pw[h])) for h in heads]
            x = [x[h] + _dot3s(_split(x[h]), pw[h]) for h in heads]
        else:
            pw = [_split(_dot(pw[h][0], pw[h][0])) for h in heads]
            x = [x[h] + _dot(x[h].astype(BF16), pw[h][0]) for h in heads]
    eg = [jnp.exp(gc[h]) for h in heads]
    uw = [_dot3s(_split(x[h]), _split(jnp.concatenate([v[h] * beta[h], kb[h] * eg[h]], axis=1))) for h in heads]
    s_old = [s_ref[h] for h in heads]
    qw_s = [_dot3s(_split(jnp.concatenate([q[h] * eg[h], uw[h][:, dv:dv + dk]], axis=0)), _split(s_old[h])) for h in heads]
    v_new = [uw[h][:, 0:dv] - qw_s[h][ck:2 * ck] for h in heads]
    vns = [_split(v_new[h]) for h in heads]
    o = [qw_s[h][0:ck] + _dot(a_in[h].astype(BF16), vns[h][0]) for h in heads]
    outs = []
    for h in heads:
        g_last = gc[h][ck - 1:ck, :]
        kd = k[h] * jnp.exp(g_last - gc[h])
        s_ref[h] = s_old[h] * jnp.exp(g_last) + _dot3s(_split(kd.T), vns[h])
        outs.append(o[h] * lax.rsqrt(jnp.mean(o[h] * o[h], axis=-1, keepdims=True) + EPS) * gn_ref[...])
    o_ref[...] = jnp.concatenate(outs, axis=1) * szb_ref[...]

    @pl.when(c == pl.num_programs(1) - 1)
    def _():
        sfin_ref[...] = s_ref[...]


def _lane_params(a_log, dt_bias):
    alog_l = jnp.zeros((1, SMALL_W), F32).at[0, SMALL_A0:SMALL_A0 + GDN_HEADS].set(a_log)
    dtb_l = jnp.zeros((1, SMALL_W), F32).at[0, SMALL_A0:SMALL_A0 + GDN_HEADS].set(dt_bias)
    return alog_l, dtb_l


def _gdn_prompt(qkvb, small, szb, conv_w, conv_b, a_log, dt_bias, gnorm, batch):
    n = qkvb.shape[0]
    seq = n // batch
    nc = seq // GDN_CHUNK
    alog_l, dtb_l = _lane_params(a_log, dt_bias)
    row = lambda w: pl.BlockSpec((GDN_CHUNK, w), lambda b, c: (b * nc + c, 0))
    full = lambda a: pl.BlockSpec(a.shape, lambda b, c: (0,) * a.ndim)
    cb = conv_b.reshape(1, -1)
    gn = gnorm.reshape(1, -1)
    return pl.pallas_call(
        _gdn_prompt_kernel,
        grid=(batch, nc),
        in_specs=[row(CONV_DIM), row(SMALL_W), row(GDN_WIDTH), full(conv_w), full(cb), full(alog_l), full(dtb_l), full(gn)],
        out_specs=[row(GDN_WIDTH),
                   pl.BlockSpec((None, GDN_HEADS, GDN_DK, GDN_DV), lambda b, c: (b, 0, 0, 0))],
        out_shape=[jax.ShapeDtypeStruct((n, GDN_WIDTH), F32),
                   jax.ShapeDtypeStruct((batch, GDN_HEADS, GDN_DK, GDN_DV), F32)],
        scratch_shapes=[pltpu.VMEM((GDN_HEADS, GDN_DK, GDN_DV), F32), pltpu.VMEM((8, CONV_DIM), F32)],
        compiler_params=_cparams(("parallel", "arbitrary")),
        name="gdn_prompt",
    )(qkvb, small, szb, conv_w, cb, alog_l, dtb_l, gn)


def _prep_w_in(w_in, d_model):
    sizes = (NSA_WIDTH, 6 * NSA_KV_HEADS * HEAD_DIM, 3 * NSA_HEADS, NSA_WIDTH, CONV_DIM, GDN_HEADS, GDN_HEADS,
             GDN_WIDTH, 2 * d_model)
    pts = np.cumsum(np.array(sizes))[:-1].tolist()
    q_a, kv_a, g_a, z_a, qkv_b, a_b, b_b, z_b, gm = jnp.split(w_in, pts, axis=1)
    small = jnp.concatenate([g_a, a_b, b_b], axis=1)
    small = jnp.pad(small, ((0, 0), (0, SMALL_W - small.shape[1])))
    return jnp.concatenate([q_a, kv_a, z_a, qkv_b, z_b, gm, small], axis=1).astype(BF16)


def _kv_leaf(kv_t):
    b, _, t = kv_t.shape
    return kv_t.reshape(1, b, 2, NSA_KV_HEADS, HEAD_DIM, t).transpose(0, 1, 5, 2, 3, 4)


def _kv_rows_t(cache):
    n, r = cache.shape[:2]
    return cache.transpose(0, 2, 3, 4, 1).reshape(n, KV_ROW, r)


def _prompt_path(x, ada, lw, cw):
    (norm_g, w_bf, offs, conv_w, conv_b, a_log, dt_bias, gnorm, wa, wb, wo, final_g) = lw
    batch, seq, d = x.shape
    n = batch * seq
    x2 = x.reshape(n, d)
    ada3 = ada.reshape(batch, 1, 3 * d)
    tm = 256
    mod = lambda k: pl.BlockSpec((None, 1, d), lambda i: (i * tm // seq, 0, k))
    (q_aug, kvc_t, kvs_t, kvw_t, kvcb, ksel, vsel_t, kwin, vwin_t, sza, qkvb, szb, gms, small) = _inproj(
        x2, ada3, ada3, (mod(1), mod(0)), norm_g.reshape(1, d), w_bf, offs, tm, seq)
    kc_aug, vc_t = _compress_prompt(kvcb, *cw, batch)
    o_a = _nsa_prompt(q_aug, kc_aug, vc_t, ksel, vsel_t, kwin, vwin_t, small, sza, batch)
    o_b, s_new = _gdn_prompt(qkvb, small, szb, conv_w, conv_b, a_log, dt_bias, gnorm, batch)
    tmo = 512
    gate_spec = pl.BlockSpec((None, 1, d), lambda i: (i * tmo // seq, 0, 2))
    y = _outproj(x2, o_a, o_b, gms, ada3, gate_spec, wa, wb, wo, final_g.reshape(1, d), tmo)
    keep = min(WINDOW, seq)
    new_conv = qkvb.reshape(batch, seq, CONV_DIM)[None, :, seq - (CONV_W - 1):]
    return (y.reshape(batch, seq, d), _kv_leaf(kvc_t), _kv_leaf(kvs_t), _kv_leaf(kvw_t[:, :, seq - keep:]), new_conv,
            s_new[None])


def _page_fetch(pt_ref, cache_hbm, buf, sem, npages):
    def copy(seq, slot, j):
        c0 = pl.multiple_of(j * PAGE_SIZE, PAGE_SIZE)
        return pltpu.make_async_copy(cache_hbm.at[pt_ref[seq, j]], buf.at[slot, :, pl.ds(c0, PAGE_SIZE)], sem.at[slot])

    def start(seq, slot):
        def body(j, c):
            copy(seq, slot, j).start()
            return c
        lax.fori_loop(0, npages, body, 0)

    def wait(seq, slot):
        for j in range(npages):
            copy(seq, slot, j).wait()

    return start, wait


def _fetch_this_prefetch_next(start, wait):
    b = pl.program_id(0)
    nb = pl.num_programs(0)

    @pl.when(b == 0)
    def _():
        start(0, 0)

    @pl.when(b + 1 < nb)
    def _():
        start(b + 1, (b + 1) % 2)

    slot = b % 2
    wait(b, slot)
    return slot


def _compress_sample_kernel(pt_ref, cache_hbm, perm_ref, w1_ref, pe_ref, w2_ref, o_ref, buf, sem, x_ref, xc_ref, *, npages):
    start, wait = _page_fetch(pt_ref, cache_hbm, buf, sem, npages)
    slot = _fetch_this_prefetch_next(start, wait)
    past = npages * PAGE_SIZE
    n16 = past // CMP_STRIDE
    gt = perm_ref.shape[0]
    cps = gt // CMP_STRIDE

    def regroup(j, c):
        t0 = pl.multiple_of(j * gt, gt)
        x_t = buf[slot, :, pl.ds(t0, gt)].astype(BF16)
        x_ref[j] = _dot(perm_ref[...], x_t, _NT).astype(BF16)
        return c

    lax.fori_loop(0, past // gt, regroup, 0, unroll=min(8, past // gt))
    for s in range(CMP_STRIDE):
        xc_ref[:, s * KV_ROW:(s + 1) * KV_ROW] = x_ref[:, s * cps:(s + 1) * cps, :].reshape(n16, KV_ROW)
    o_ref[...] = _compress_math(xc_ref[...], w1_ref, pe_ref, w2_ref).astype(BF16)


def _compress_sample(cache_t, page_table, w1big, pebig, w2big):
    nseq, npages = page_table.shape
    past = npages * PAGE_SIZE
    n16 = past // CMP_STRIDE
    gt = 2 * PAGE_SIZE
    assert past % gt == 0 and gt // CMP_STRIDE == 16
    row = np.arange(gt)
    perm_t = jnp.asarray((row[:, None] % (gt // CMP_STRIDE)) * CMP_STRIDE + row[:, None] // (gt // CMP_STRIDE)
                         == row[None, :], dtype=BF16)
    grid_spec = pltpu.PrefetchScalarGridSpec(
        num_scalar_prefetch=1,
        grid=(nseq,),
        in_specs=[pl.BlockSpec(memory_space=pl.ANY),
                  pl.BlockSpec(perm_t.shape, lambda b, pt: (0, 0)),
                  pl.BlockSpec(w1big.shape, lambda b, pt: (0, 0, 0)),
                  pl.BlockSpec(pebig.shape, lambda b, pt: (0, 0, 0)),
                  pl.BlockSpec(w2big.shape, lambda b, pt: (0, 0))],
        out_specs=pl.BlockSpec((None, n16, KV_ROW), lambda b, pt: (b, 0, 0)),
        scratch_shapes=[pltpu.VMEM((2, KV_ROW, past), F32), pltpu.SemaphoreType.DMA((2,)),
                        pltpu.VMEM((past // gt, gt, KV_ROW), BF16),
                        pltpu.VMEM((n16, CMP_STRIDE * KV_ROW), BF16)],
    )
    return pl.pallas_call(
        functools.partial(_compress_sample_kernel, npages=npages),
        grid_spec=grid_spec,
        out_shape=jax.ShapeDtypeStruct((nseq, n16, KV_ROW), BF16),
        compiler_params=_cparams(("arbitrary",)),
        name="compress_sample",
    )(page_table, cache_t, perm_t, w1big, pebig, w2big)


def _nsa_sample_kernel(pt_ref, q_ref, kc_ref, cache_hbm, win_ref, nsel_ref, nwin_ref, nwint_ref, small_ref, sza_ref,
                       selmap_ref, onehot_ref, o_ref, wout_ref, buf, sem, *, npages, tn, kt):
    hd = HEAD_DIM
    rt = 8
    start, wait = _page_fetch(pt_ref, cache_hbm, buf, sem, npages)
    slot = _fetch_this_prefetch_next(start, wait)
    past = npages * PAGE_SIZE
    nbuf = win_ref.shape[1]
    n_cmp = kc_ref.shape[0]
    n_blk_lanes = selmap_ref.shape[1]
    q8 = q_ref[...].astype(F32)
    gates = jax.nn.sigmoid(small_ref[...])
    tau = lax.broadcasted_iota(jnp.int32, (rt, 1), 0) % tn
    t_col = past + tau
    t4 = jnp.concatenate([t_col] * NSA_GROUP, axis=0)
    rows = NSA_GROUP * rt
    newcol = lax.broadcasted_iota(jnp.int32, (1, SMALL_W), 1)
    new_dist = t4 - (past + newcol)
    new_ok = (newcol < tn) & (new_dist >= 0)
    zpad = jnp.zeros((SMALL_W - rt, hd), F32)
    groups = range(NSA_KV_HEADS)
    kcols = [slice(g * hd, (g + 1) * hd) for g in groups]
    vcols = [slice((NSA_KV_HEADS + g) * hd, (NSA_KV_HEADS + g + 1) * hd) for g in groups]

    def new_tile(ref, col):
        return jnp.concatenate([ref[:, col], zpad], axis=0).astype(BF16)

    def pos_cols(p0, n):
        pos = (p0 + lax.broadcasted_iota(jnp.int32, (n, 1), 0)).astype(F32)
        hi = jnp.floor(pos * (1.0 / POS_SPLIT))
        return _aug_cols(n, hi, pos - hi * POS_SPLIT)

    qgs, slopes, o_cs, imps = [], [], [], []
    for g in groups:
        kcol, vcol = kcols[g], vcols[g]
        qg = jnp.concatenate([q8[:, (g * NSA_GROUP + h) * AUG:(g * NSA_GROUP + h) * AUG + hd]
                              for h in range(NSA_GROUP)], axis=0).astype(BF16)
        slope = jnp.concatenate([jnp.full((rt, 1), _SLOPES[g * NSA_GROUP + h], F32) for h in range(NSA_GROUP)], axis=0)

        s = _dot(qg, kc_ref[:, kcol], _NT)
        end = lax.broadcasted_iota(jnp.int32, (1, n_cmp), 1) * CMP_STRIDE + (CMP_BLOCK - 1)
        dist = t4 - end
        mask = dist >= 0
        _, p = _softmax_block(s - slope * dist.astype(F32), mask)
        p = p / jnp.maximum(jnp.sum(p, axis=-1, keepdims=True), 1e-30)
        o_c = _dot(p.astype(BF16), kc_ref[:, vcol])
        psum = p[0:rt]
        for h in range(1, NSA_GROUP):
            psum = psum + p[h * rt:(h + 1) * rt]
        for lst, val in ((qgs, qg), (slopes, slope), (o_cs, o_c), (imps, _importance(psum, selmap_ref[...]))):
            lst.append(val)

    ncol = NSA_KV_HEADS * rt
    imp_pad = jnp.concatenate(imps + [jnp.zeros((SMALL_W - ncol, n_blk_lanes), F32)], axis=0)
    tq_row = past + lax.broadcasted_iota(jnp.int32, (1, SMALL_W), 1) % tn
    selm_all = _select_blocks_t(imp_pad.T, tq_row).T

    selms, q_augs, q_fulls = [], [], []
    for g in groups:
        selm = selm_all[g * rt:(g + 1) * rt]
        q_aug = jnp.concatenate([q8[:, (g * NSA_GROUP + h) * AUG:(g * NSA_GROUP + h + 1) * AUG]
                                 for h in range(NSA_GROUP)], axis=0)
        blk_bias = jnp.where(selm > 0.5, 0.0, MASKV)
        q_full = jnp.concatenate([q_aug, jnp.concatenate([blk_bias] * NSA_GROUP, axis=0)], axis=1).astype(BF16)
        for lst, val in ((selms, selm), (q_augs, q_aug), (q_fulls, q_full)):
            lst.append(val)

    bpt = kt // SEL_BLOCK
    carries = [_online_init(rows, hd) for _ in groups]
    for j in range(past // kt):
        k0 = j * kt
        for g in groups:
            k_t = buf[slot, kcols[g], k0:k0 + kt].astype(BF16)
            v_t = buf[slot, vcols[g], k0:k0 + kt].astype(BF16)
            q_rest = jnp.concatenate([q_fulls[g][:, hd:AUG], q_fulls[g][:, AUG + j * bpt:AUG + (j + 1) * bpt]], axis=1)
            s = _dot(q_fulls[g][:, 0:hd], k_t) + _dot(q_rest, onehot_ref[j])
            carries[g] = _flash_step(carries[g], s, v_t, _NT)

    outs = []
    for g in groups:
        kcol, vcol = kcols[g], vcols[g]
        qg, slope, o_c, selm, q_aug = qgs[g], slopes[g], o_cs[g], selms[g], q_augs[g]
        nb_new = past // SEL_BLOCK
        new_sel = jnp.concatenate([selm[:, nb_new:nb_new + 1]] * NSA_GROUP, axis=0) > 0.5
        kn = jnp.concatenate([jnp.concatenate([nsel_ref[:, kcol], zpad], axis=0), pos_cols(past, SMALL_W)], axis=1)
        s = _dot(q_aug.astype(BF16), kn.astype(BF16), _NT) + jnp.where(new_ok & new_sel, 0.0, MASKV)
        _, l, acc = _flash_step(carries[g], s, new_tile(nsel_ref, vcol))
        o_s = acc / jnp.maximum(l, 1e-30)

        s = _dot(qg, win_ref[kcol, :].astype(BF16))
        dist = t4 - (past - nbuf + lax.broadcasted_iota(jnp.int32, (1, nbuf), 1))
        mask = (dist >= 0) & (dist < WINDOW)
        carry = _online_update(_online_init(rows, hd), s - slope * dist.astype(F32), mask,
                               win_ref[vcol, :].astype(BF16), _NT)
        s = _dot(qg, new_tile(nwin_ref, kcol), _NT)
        _, l, acc = _online_update(carry, s - slope * new_dist.astype(F32), new_ok, new_tile(nwin_ref, vcol))
        o_w = acc / jnp.maximum(l, 1e-30)

        for h in range(NSA_GROUP):
            c = 3 * (g * NSA_GROUP + h)
            r = slice(h * rt, (h + 1) * rt)
            outs.append(gates[:, c:c + 1] * o_c[r] + gates[:, c + 1:c + 2] * o_s[r] + gates[:, c + 2:c + 3] * o_w[r])
    o_ref[...] = jnp.concatenate(outs, axis=1) * sza_ref[...]
    wout_ref[...] = jnp.concatenate([win_ref[:, tn:nbuf], nwint_ref[...]], axis=1)


def _nsa_sample(qs, kc, cache_t, win_t, kvs, kvw, small, sza, page_table, tn):
    nseq, npages = page_table.shape
    past = npages * PAGE_SIZE
    nbuf = win_t.shape[2]
    assert nbuf == WINDOW and 8 % tn == 0 and tn <= SEL_BLOCK
    n_cmp = kc.shape[1]
    n_blk = -(-(past + tn) // SEL_BLOCK)
    n_blk_lanes = -(-n_blk // 128) * 128
    selmap = _sel_map(n_cmp, n_blk, n_blk_lanes)
    kt = min(4096, past)
    ntile, bpt = past // kt, kt // SEL_BLOCK
    key = np.arange(past).reshape(ntile, 1, kt)
    pos_rows = np.zeros((ntile, AUG - HEAD_DIM, kt), np.float32)
    pos_rows[:, 0], pos_rows[:, 1] = key[:, 0] // POS_SPLIT, key[:, 0] % POS_SPLIT
    member = (key % kt // SEL_BLOCK == np.arange(bpt)[None, :, None]).astype(np.float32)
    onehot_t = jnp.asarray(np.concatenate([pos_rows, member], axis=1), dtype=BF16)
    seq3 = lambda r, w: pl.BlockSpec((None, r, w), lambda b, pt: (b, 0, 0))
    grid_spec = pltpu.PrefetchScalarGridSpec(
        num_scalar_prefetch=1,
        grid=(nseq,),
        in_specs=[seq3(8, NSA_HEADS * AUG), seq3(n_cmp, KV_ROW), pl.BlockSpec(memory_space=pl.ANY), seq3(KV_ROW, nbuf),
                  seq3(8, KV_ROW), seq3(8, KV_ROW), seq3(KV_ROW, tn), seq3(8, SMALL_W), seq3(8, NSA_WIDTH),
                  pl.BlockSpec(selmap.shape, lambda b, pt: (0, 0)),
                  pl.BlockSpec(onehot_t.shape, lambda b, pt: (0, 0, 0))],
        out_specs=[seq3(8, NSA_WIDTH), seq3(KV_ROW, nbuf)],
        scratch_shapes=[pltpu.VMEM((2, KV_ROW, past), F32), pltpu.SemaphoreType.DMA((2,))],
    )
    rep8 = lambda a: jnp.concatenate([a.reshape(nseq, tn, a.shape[-1])] * (8 // tn), axis=1)
    o8, win_new_t = pl.pallas_call(
        functools.partial(_nsa_sample_kernel, npages=npages, tn=tn, kt=kt),
        grid_spec=grid_spec,
        out_shape=[jax.ShapeDtypeStruct((nseq, 8, NSA_WIDTH), F32), jax.ShapeDtypeStruct((nseq, KV_ROW, nbuf), F32)],
        compiler_params=_cparams(("arbitrary",)),
        name="nsa_sample",
    )(page_table, rep8(qs), kc, cache_t, win_t, rep8(kvs), rep8(kvw), kvw.transpose(0, 2, 1), rep8(small), rep8(sza),
      selmap, onehot_t)
    return o8[:, :tn].reshape(nseq * tn, NSA_WIDTH), win_new_t


def _gdn_sample_kernel(eq_ref, ek_ref, ev_ref, cwq_ref, cwk_ref, cwv_ref, cbq_ref, cbk_ref, cbv_ref, small_ref,
                       alog_ref, dtb_ref, szb_ref, gn_ref, s_ref, o_ref, so_ref, qs_ref, ks_ref):
    h = pl.program_id(0)
    tn = o_ref.shape[0]
    dk, dv, nseq = so_ref.shape

    def conv(e_ref, cw_ref, cb_ref, t):
        y = cb_ref[...]
        for i in range(CONV_W):
            y = y + e_ref[t + i] * cw_ref[:, i:i + 1]
        return _silu(y)

    so_ref[...] = s_ref[...]
    neg_rate = -jnp.exp(alog_ref[...])
    for t in range(tn):
        q = conv(eq_ref, cwq_ref, cbq_ref, t)
        k = conv(ek_ref, cwk_ref, cbk_ref, t)
        v = conv(ev_ref, cwv_ref, cbv_ref, t)
        qs_ref[...] = q * lax.rsqrt(jnp.sum(q * q, axis=0, keepdims=True) + EPS) * (dk ** -0.5)
        ks_ref[...] = k * lax.rsqrt(jnp.sum(k * k, axis=0, keepdims=True) + EPS)
        a_in = small_ref[t, pl.ds(SMALL_A0 + h, 1), :]
        b_in = small_ref[t, pl.ds(SMALL_B0 + h, 1), :]
        decay = jnp.exp(neg_rate * _softplus(a_in + dtb_ref[...]))
        beta = jax.nn.sigmoid(b_in)

        def ks_step(i, acc):
            return acc + ks_ref[pl.ds(i, 1), :] * so_ref[i]

        k_s = lax.fori_loop(0, dk, ks_step, jnp.zeros((dv, nseq), F32), unroll=8)
        delta = beta * (v - decay * k_s)

        def upd_step(i, acc):
            s_new = decay * so_ref[i] + ks_ref[pl.ds(i, 1), :] * delta
            so_ref[i] = s_new
            return acc + qs_ref[pl.ds(i, 1), :] * s_new

        o = lax.fori_loop(0, dk, upd_step, jnp.zeros((dv, nseq), F32), unroll=8)
        o = o * lax.rsqrt(jnp.mean(o * o, axis=0, keepdims=True) + EPS) * gn_ref[...]
        o_ref[t] = o * szb_ref[t]


def _gdn_sample(qkvb, small, szb, state_conv, state_gdn, conv_w, conv_b, a_log, dt_bias, gnorm, tn):
    nseq = state_gdn.shape[0]
    ext = jnp.concatenate([state_conv, qkvb.reshape(nseq, tn, CONV_DIM)], axis=1)
    ext_t = ext.transpose(1, 2, 0)
    small_t = small.reshape(nseq, tn, SMALL_W).transpose(1, 2, 0)
    szb_t = szb.reshape(nseq, tn, GDN_WIDTH).transpose(1, 2, 0)
    s_t = state_gdn.transpose(1, 2, 3, 0)
    cw_t = conv_w.T
    cb_t = conv_b.reshape(-1, 1)
    alog_b = jnp.broadcast_to(a_log[:, None, None], (GDN_HEADS, 1, nseq))
    dtb_b = jnp.broadcast_to(dt_bias[:, None, None], (GDN_HEADS, 1, nseq))
    gn = gnorm.reshape(-1, 1)
    nqk = GDN_QK_WIDTH // GDN_DK
    chan = lambda off: pl.BlockSpec((CONV_W - 1 + tn, GDN_DK, nseq), lambda h: (0, off + h, 0))
    cwb = lambda off: pl.BlockSpec((GDN_DK, CONV_W), lambda h: (off + h, 0))
    cbb = lambda off: pl.BlockSpec((GDN_DK, 1), lambda h: (off + h, 0))
    perhead = pl.BlockSpec((None, 1, nseq), lambda h: (h, 0, 0))
    o_t, s_new = pl.pallas_call(
        _gdn_sample_kernel,
        grid=(GDN_HEADS,),
        in_specs=[chan(0), chan(nqk), chan(2 * nqk), cwb(0), cwb(nqk), cwb(2 * nqk), cbb(0), cbb(nqk), cbb(2 * nqk),
                  pl.BlockSpec((tn, SMALL_W, nseq), lambda h: (0, 0, 0)), perhead, perhead,
                  pl.BlockSpec((tn, GDN_DV, nseq), lambda h: (0, h, 0)),
                  pl.BlockSpec((GDN_DV, 1), lambda h: (0, 0)),
                  pl.BlockSpec((None, GDN_DK, GDN_DV, nseq), lambda h: (h, 0, 0, 0))],
        out_specs=[pl.BlockSpec((tn, GDN_DV, nseq), lambda h: (0, h, 0)),
                   pl.BlockSpec((None, GDN_DK, GDN_DV, nseq), lambda h: (h, 0, 0, 0))],
        out_shape=[jax.ShapeDtypeStruct((tn, GDN_WIDTH, nseq), F32),
                   jax.ShapeDtypeStruct((GDN_HEADS, GDN_DK, GDN_DV, nseq), F32)],
        scratch_shapes=[pltpu.VMEM((GDN_DK, nseq), F32), pltpu.VMEM((GDN_DK, nseq), F32)],
        compiler_params=_cparams(("parallel",)),
        name="gdn_sample",
    )(ext_t, ext_t, ext_t, cw_t, cw_t, cw_t, cb_t, cb_t, cb_t, small_t, alog_b, dtb_b, szb_t, gn, s_t)
    o_b = o_t.transpose(2, 0, 1).reshape(nseq * tn, GDN_WIDTH)
    return o_b, ext[:, tn:], s_new.transpose(3, 0, 1, 2)


def _sample_path(x, ada, lw, cw, cache_cmp, cache_sel, cache_win, state_conv, state_gdn, page_table):
    (norm_g, w_bf, offs, conv_w, conv_b, a_log, dt_bias, gnorm, wa, wb, wo, final_g) = lw
    nseq, tn, d = x.shape
    n = nseq * tn
    x2 = x.reshape(n, d)
    ada_rows = jnp.repeat(ada, tn, axis=0)
    tm = min(256, n)
    mod = lambda k: pl.BlockSpec((tm, d), lambda i: (i, k))
    (qs, kvc_t, kvs_t, kvw_t, _, _, _, _, _, sza, qkvb, szb, gms, small) = _inproj(
        x2, ada_rows, ada_rows, (mod(1), mod(0)), norm_g.reshape(1, d), w_bf, offs, tm, n)
    rows = lambda a_t: a_t.reshape(KV_ROW, nseq, tn).transpose(1, 2, 0)
    kvc, kvs, kvw = rows(kvc_t), rows(kvs_t), rows(kvw_t)
    kc = _compress_sample(_kv_rows_t(cache_cmp), page_table, *cw)
    o_a, win_new_t = _nsa_sample(qs, kc, _kv_rows_t(cache_sel), _kv_rows_t(cache_win), kvs, kvw, small, sza, page_table, tn)
    o_b, conv_new, s_new = _gdn_sample(qkvb, small, szb, state_conv, state_gdn, conv_w, conv_b, a_log, dt_bias, gnorm, tn)
    y = _outproj(x2, o_a, o_b, gms, ada_rows, mod(2), wa, wb, wo, final_g.reshape(1, d), tm)
    kvshape = (1, nseq, tn, 2, NSA_KV_HEADS, HEAD_DIM)
    return (y.reshape(nseq, tn, d), kvc.reshape(kvshape), kvs.reshape(kvshape), _kv_leaf(win_new_t),
            conv_new[None], s_new[None])


def kernel(x_prompt, x_sample, cache_cmp_kv, cache_sel_kv, cache_win_kv, state_conv, state_gdn, page_table, c_prompt, c_sample, norm_g, w_ada, b_ada, w_in, cmp_pe_k, cmp_w1_k, cmp_w2_k, cmp_pe_v, cmp_w1_v, cmp_w2_v, conv_w, conv_b, gdn_a_log, gdn_dt_bias, gdn_norm_g, w_o_nsa, w_o_gdn, w_out, final_g):
    assert norm_g.shape[0] == 1, "single trunk layer"
    d = x_prompt.shape[-1]
    l = 0
    offs, _ = _seg_offsets(d)
    w_bf = _prep_w_in(w_in[l], d)
    ada = _ada(jnp.concatenate([c_prompt, c_sample], axis=0), w_ada[l].astype(BF16), b_ada[l])
    cw = _compress_weights(cmp_pe_k[l], cmp_w1_k[l], cmp_w2_k[l], cmp_pe_v[l], cmp_w1_v[l], cmp_w2_v[l])
    lw = (norm_g[l], w_bf, offs, conv_w[l], conv_b[l], gdn_a_log[l], gdn_dt_bias[l], gdn_norm_g[l],
          w_o_nsa[l].astype(BF16), w_o_gdn[l].astype(BF16), w_out[l].astype(BF16), final_g)
    nb = c_prompt.shape[0]
    yp, cmp_p, sel_p, win_p, conv_p, gdn_p = _prompt_path(x_prompt, ada[:nb], lw, cw)
    ys, cmp_s, sel_s, win_s, conv_s, gdn_s = _sample_path(
        x_sample, ada[nb:], lw, cw, cache_cmp_kv[l], cache_sel_kv[l], cache_win_kv[l], state_conv[l], state_gdn[l],
        page_table)
    return (yp, ys, cmp_p, sel_p, win_p, conv_p, gdn_p, cmp_s, sel_s, win_s, conv_s, gdn_s)
```

```python
import functools
import math

import numpy as np
import jax
import jax.numpy as jnp
from jax import lax
from jax.experimental import pallas as pl
from jax.experimental.pallas import tpu as pltpu

F32 = jnp.float32
BF16 = jnp.bfloat16

NSA_HEADS = 8
NSA_KV_HEADS = 2
NSA_GROUP = NSA_HEADS // NSA_KV_HEADS
HEAD_DIM = 64
CMP_BLOCK = 32
CMP_STRIDE = 16
CMP_HIDDEN = 64
SEL_BLOCK = 64
N_SEL = 16
WINDOW = 512
Q_BLOCK = 128
GDN_HEADS = 8
GDN_DK = 64
GDN_DV = 64
CONV_W = 4
GDN_CHUNK = 64
PAGE_SIZE = 128
EPS = 1e-6

NSA_WIDTH = NSA_HEADS * HEAD_DIM
KV_ROW = 2 * NSA_KV_HEADS * HEAD_DIM
GDN_QK_WIDTH = GDN_HEADS * GDN_DK
GDN_WIDTH = GDN_HEADS * GDN_DV
CONV_DIM = 2 * GDN_QK_WIDTH + GDN_WIDTH
SMALL_W = 128
SMALL_A0 = 3 * NSA_HEADS
SMALL_B0 = SMALL_A0 + GDN_HEADS

NEG = -1e30
MASKV = -(2.0 ** 100)
AUG = 128
POS_SPLIT = 128
VMEM_LIMIT = 56 * 1024 * 1024

_NT = (((1,), (1,)), ((), ()))
_NN = (((1,), (0,)), ((), ()))


def _alibi_slopes():
    h = np.arange(1, NSA_HEADS + 1, dtype=np.float32)
    return [float(v) for v in np.power(np.float32(2.0), -np.float32(8.0) * h / np.float32(NSA_HEADS))]


_SLOPES = _alibi_slopes()


def _dot(a, b, dims=_NN):
    return lax.dot_general(a, b, dims, preferred_element_type=F32)


def _split(a):
    hi = a.astype(BF16)
    lo = (a - hi.astype(F32)).astype(BF16)
    return hi, lo


def _dot3s(a_split, b_split, dims=_NN):
    (ah, al), (bh, bl) = a_split, b_split
    return _dot(ah, bh, dims) + _dot(ah, bl, dims) + _dot(al, bh, dims)


def _silu(x):
    return x * jax.nn.sigmoid(x)


def _cparams(sem, flags=None):
    return pltpu.CompilerParams(dimension_semantics=sem, vmem_limit_bytes=VMEM_LIMIT, flags=flags)


def _ada_kernel(c_ref, w_ref, b_ref, o_ref):
    sc = _silu(c_ref[...]).astype(BF16)
    o_ref[...] = _dot(sc, w_ref[...]) + b_ref[...]


def _ada(c, w_bf, b):
    n, d = c.shape
    n_pad = -(-n // 8) * 8
    c = jnp.pad(c, ((0, n_pad - n), (0, 0)))
    out = pl.pallas_call(
        _ada_kernel,
        out_shape=jax.ShapeDtypeStruct((n_pad, w_bf.shape[1]), F32),
        name="ada",
    )(c, w_bf, b.reshape(1, -1))
    return out[:n]


_SEG = (("q", NSA_WIDTH), ("kvc", KV_ROW), ("kvs", KV_ROW), ("kvw", KV_ROW), ("za", NSA_WIDTH),
        ("qkvb", CONV_DIM), ("zb", GDN_WIDTH), ("gm", None), ("small", SMALL_W))


def _seg_offsets(d_model):
    offs, c = {}, 0
    for name, n in _SEG:
        n = 2 * d_model if n is None else n
        offs[name] = (c, n)
        c += n
    return offs, c


def _aug_cols(rows, c0, c1):
    lane = lax.broadcasted_iota(jnp.int32, (rows, AUG - HEAD_DIM), 1)
    return jnp.where(lane == 0, c0, jnp.where(lane == 1, c1, 0.0)).astype(F32)


def _aug_keys(kv, pos):
    hi = jnp.floor(pos * (1.0 / POS_SPLIT))
    cols = _aug_cols(kv.shape[0], hi, pos - hi * POS_SPLIT)
    parts = []
    for g in range(NSA_KV_HEADS):
        parts += [kv[:, g * HEAD_DIM:(g + 1) * HEAD_DIM], cols]
    k_aug = jnp.concatenate(parts, axis=1).astype(BF16)
    kv_t = kv.T
    return k_aug, kv_t[NSA_KV_HEADS * HEAD_DIM:].astype(BF16), kv_t


def _inproj_kernel(x_ref, scale_ref, shift_ref, ng_ref, w_ref,
                   q_ref, kvct_ref, kvst_ref, kvwt_ref, kvcb_ref, ksel_ref, vselt_ref, kwin_ref, vwint_ref,
                   sza_ref, qkvb_ref, szb_ref, gms_ref, small_ref, *, offs, seq):
    x = x_ref[...]
    tm = x.shape[0]
    y = x * lax.rsqrt(jnp.mean(x * x, axis=-1, keepdims=True) + EPS) * ng_ref[...]
    h = y * (1.0 + scale_ref[...]) + shift_ref[...]
    hb = h.astype(BF16)

    def seg(name):
        c0, n = offs[name]
        return _dot(hb, w_ref[:, c0:c0 + n])

    qv = seg("q") * (HEAD_DIM ** -0.5)
    parts = []
    for hh in range(NSA_HEADS):
        parts += [qv[:, hh * HEAD_DIM:(hh + 1) * HEAD_DIM], _aug_cols(tm, _SLOPES[hh] * POS_SPLIT, _SLOPES[hh])]
    q_ref[...] = jnp.concatenate(parts, axis=1).astype(BF16)

    pos = ((pl.program_id(0) * tm) % seq + lax.broadcasted_iota(jnp.int32, (tm, 1), 0)).astype(F32)
    v = seg("kvc")
    kvct_ref[...] = v.T
    kvcb_ref[...] = v.astype(BF16)
    for name, f_ref, k_ref, vt_ref in (("kvs", kvst_ref, ksel_ref, vselt_ref), ("kvw", kvwt_ref, kwin_ref, vwint_ref)):
        k_ref[...], vt_ref[...], f_ref[...] = _aug_keys(seg(name), pos)
    sza_ref[...] = _silu(seg("za"))
    qkvb_ref[...] = seg("qkvb")
    szb_ref[...] = _silu(seg("zb"))
    gms_ref[...] = jax.nn.sigmoid(seg("gm"))
    small_ref[...] = seg("small")


def _inproj(x2, scale_arr, shift_arr, mod_specs, ng, w_bf, offs, tm, seq):
    n, d = x2.shape
    wtot = w_bf.shape[1]
    nbatch = n // seq
    spt = seq // tm
    vw = NSA_KV_HEADS * HEAD_DIM
    row = lambda w: pl.BlockSpec((tm, w), lambda i: (i, 0))
    rows = lambda w, dt: (jax.ShapeDtypeStruct((n, w), dt), row(w))
    tr = lambda w, dt: (jax.ShapeDtypeStruct((nbatch, w, seq), dt),
                        pl.BlockSpec((None, w, tm), lambda i: (i // spt, 0, i % spt)))
    vt = tr(vw, BF16)
    kvt = tr(KV_ROW, F32)
    outs = [rows(NSA_HEADS * AUG, BF16), kvt, kvt, kvt, rows(KV_ROW, BF16),
            rows(NSA_KV_HEADS * AUG, BF16), vt, rows(NSA_KV_HEADS * AUG, BF16), vt,
            rows(offs["za"][1], F32), rows(offs["qkvb"][1], F32), rows(offs["zb"][1], F32), rows(offs["gm"][1], F32),
            rows(offs["small"][1], F32)]
    out_shape = [o[0] for o in outs]
    out_specs = [o[1] for o in outs]
    return pl.pallas_call(
        functools.partial(_inproj_kernel, offs=offs, seq=seq),
        grid=(n // tm,),
        in_specs=[row(d), mod_specs[0], mod_specs[1],
                  pl.BlockSpec((1, d), lambda i: (0, 0)),
                  pl.BlockSpec((d, wtot), lambda i: (0, 0))],
        out_specs=out_specs,
        out_shape=out_shape,
        compiler_params=_cparams(("parallel",)),
        name="inproj",
    )(x2, scale_arr, shift_arr, ng, w_bf)


def _outproj_kernel(x_ref, oa_ref, ob_ref, gms_ref, gate_ref, wa_ref, wb_ref, wo_ref, fg_ref, y_ref, *, d):
    ma = _dot(oa_ref[...].astype(BF16), wa_ref[...])
    mb = _dot(ob_ref[...].astype(BF16), wb_ref[...])
    m = gms_ref[:, 0:d] * ma + gms_ref[:, d:2 * d] * mb
    y = x_ref[...] + gate_ref[...] * _dot(m.astype(BF16), wo_ref[...])
    y_ref[...] = y * lax.rsqrt(jnp.mean(y * y, axis=-1, keepdims=True) + EPS) * fg_ref[...]


def _outproj(x2, oa, ob, gms, gate_arr, gate_spec, wa, wb, wo, fg, tm):
    n, d = x2.shape
    row = lambda w: pl.BlockSpec((tm, w), lambda i: (i, 0))
    full = lambda a: pl.BlockSpec(a.shape, lambda i: (0, 0))
    return pl.pallas_call(
        functools.partial(_outproj_kernel, d=d),
        grid=(n // tm,),
        in_specs=[row(d), row(oa.shape[1]), row(ob.shape[1]), row(2 * d), gate_spec,
                  full(wa), full(wb), full(wo), full(fg)],
        out_specs=row(d),
        out_shape=jax.ShapeDtypeStruct((n, d), F32),
        compiler_params=_cparams(("parallel",)),
        name="outproj",
    )(x2, oa, ob, gms, gate_arr, wa, wb, wo, fg)


def _compress_math(x, w1_ref, pe_ref, w2_ref):
    n16 = x.shape[0]
    a0 = _dot(x, w1_ref[0])
    a1 = _dot(x, w1_ref[1])
    p0 = _dot(pe_ref[0], w1_ref[0]) + _dot(pe_ref[1], w1_ref[1])
    pre = a0 + pltpu.roll(a1, n16 - 1, 0) + p0[0:1]
    return _dot(_silu(pre).astype(BF16), w2_ref[...])


def _compress_kernel(x_ref, w1_ref, pe_ref, w2_ref, k_ref, vt_ref):
    kv = _compress_math(x_ref[...], w1_ref, pe_ref, w2_ref)
    n16 = kv.shape[0]
    end = (lax.broadcasted_iota(jnp.int32, (n16, 1), 0) * CMP_STRIDE + (CMP_BLOCK - 1)).astype(F32)
    k_ref[...], vt_ref[...], _ = _aug_keys(kv, end)


def _compress_prompt(kvcb, w1big, pebig, w2big, batch):
    n = kvcb.shape[0]
    n16 = n // batch // CMP_STRIDE
    x = kvcb.reshape(batch, n16, CMP_STRIDE * KV_ROW)
    vw = NSA_KV_HEADS * HEAD_DIM
    return pl.pallas_call(
        _compress_kernel,
        grid=(batch,),
        in_specs=[pl.BlockSpec((None, n16, CMP_STRIDE * KV_ROW), lambda b: (b, 0, 0)),
                  pl.BlockSpec(w1big.shape, lambda b: (0, 0, 0)),
                  pl.BlockSpec(pebig.shape, lambda b: (0, 0, 0)),
                  pl.BlockSpec(w2big.shape, lambda b: (0, 0))],
        out_specs=[pl.BlockSpec((None, n16, NSA_KV_HEADS * AUG), lambda b: (b, 0, 0)),
                   pl.BlockSpec((None, vw, n16), lambda b: (b, 0, 0))],
        out_shape=[jax.ShapeDtypeStruct((batch, n16, NSA_KV_HEADS * AUG), BF16),
                   jax.ShapeDtypeStruct((batch, vw, n16), BF16)],
        compiler_params=_cparams(("parallel",)),
        name="compress_prompt",
    )(x, w1big, pebig, w2big)


def _compress_weights(pe_k, w1_k, w2_k, pe_v, w1_v, w2_v):
    r_cnt = CMP_BLOCK // CMP_STRIDE
    g = NSA_KV_HEADS
    eye2 = jnp.eye(2, dtype=F32)
    eyeg = jnp.eye(g, dtype=F32)
    w1 = jnp.stack([w1_k, w1_v]).reshape(2, r_cnt, CMP_STRIDE, HEAD_DIM, CMP_HIDDEN)
    w1big = jnp.einsum("krsde,kK,gG->rskgdKGe", w1, eye2, eyeg)
    w1big = w1big.reshape(r_cnt, CMP_STRIDE * KV_ROW, 2 * g * CMP_HIDDEN).astype(BF16)
    pe = jnp.stack([pe_k, pe_v]).reshape(2, r_cnt, CMP_STRIDE, HEAD_DIM)
    pebig = jnp.broadcast_to(pe.transpose(1, 2, 0, 3)[:, :, :, None, :], (r_cnt, CMP_STRIDE, 2, g, HEAD_DIM))
    pebig = pebig.reshape(r_cnt, 1, CMP_STRIDE * KV_ROW)
    pebig = jnp.pad(pebig, ((0, 0), (0, 7), (0, 0))).astype(BF16)
    w2 = jnp.stack([w2_k, w2_v])
    w2big = jnp.einsum("ked,kK,gG->kgeKGd", w2, eye2, eyeg).reshape(2 * g * CMP_HIDDEN, KV_ROW).astype(BF16)
    return w1big, pebig, w2big


def _sel_map(n_cmp_rows, n_blk, n_blk_pad):
    i = np.arange(n_cmp_rows)[:, None] * CMP_STRIDE
    j = np.arange(n_blk_pad)[None, :] * SEL_BLOCK
    ov = np.minimum(i + CMP_BLOCK, j + SEL_BLOCK) - np.maximum(i, j)
    m = np.clip(ov, 0, None).astype(np.float32) / np.float32(CMP_BLOCK)
    m[:, n_blk:] = 0.0
    return jnp.asarray(m, dtype=BF16)


def _softmax_block(s, mask):
    s = jnp.where(mask, s, NEG)
    m = jnp.max(s, axis=-1, keepdims=True)
    p = jnp.where(mask, jnp.exp(s - m), 0.0)
    return m, p


def _online_update(carry, s, mask, v_bf, v_dims=_NN):
    m, l, acc = carry
    s = jnp.where(mask, s, NEG)
    m_new = jnp.maximum(m, jnp.max(s, axis=-1, keepdims=True))
    alpha = jnp.exp(m - m_new)
    p = jnp.where(mask, jnp.exp(s - m_new), 0.0)
    l = alpha * l + jnp.sum(p, axis=-1, keepdims=True)
    acc = alpha * acc + _dot(p.astype(BF16), v_bf, v_dims)
    return m_new, l, acc


def _flash_step(carry, s, v_bf, v_dims=_NN):
    m, l, acc = carry
    m_new = jnp.maximum(m, jnp.max(s, axis=-1, keepdims=True))
    alpha = jnp.exp(m - m_new)
    p = jnp.exp(s - m_new)
    return (m_new, alpha * l + jnp.sum(p, axis=-1, keepdims=True),
            alpha * acc + _dot(p.astype(BF16), v_bf, v_dims))


def _online_init(rows, hd):
    return (jnp.full((rows, 1), NEG, F32), jnp.zeros((rows, 1), F32), jnp.zeros((rows, hd), F32))


def _importance(p_sum, selmap):
    ph, pl_ = _split(p_sum)
    pl2 = (p_sum - ph.astype(F32) - pl_.astype(F32)).astype(BF16)
    return _dot(ph, selmap) + _dot(pl_, selmap) + _dot(pl2, selmap)


def _flash_step_t(carry, s_t, v_t):
    m, l, acc = carry
    m_new = jnp.maximum(m, jnp.max(s_t, axis=0, keepdims=True))
    alpha = jnp.exp(m - m_new)
    p = jnp.exp(s_t - m_new)
    l = alpha * l + jnp.sum(p, axis=0, keepdims=True)
    acc = alpha * acc + _dot(v_t, p.astype(BF16))
    return m_new, l, acc


def _flash_init_t(cols, dv):
    return (jnp.full((1, cols), NEG, F32), jnp.zeros((1, cols), F32), jnp.zeros((dv, cols), F32))


def _select_blocks_t(imp_t, tq):
    nb, nq = imp_t.shape
    jj = lax.broadcasted_iota(jnp.int32, (nb, nq), 0)
    cur = tq // SEL_BLOCK
    forced = (jj == 0) | (jj == cur) | (jj == cur - 1)
    jf = jj.astype(F32)
    score = jnp.where(forced, NEG, jnp.where(jj <= cur, imp_t, NEG))
    selm = jnp.where(forced, 1.0, 0.0).astype(F32)
    for _ in range(N_SEL - 3):
        m = jnp.max(score, axis=0, keepdims=True)
        idx = jnp.min(jnp.where(score == m, jf, float(nb)), axis=0, keepdims=True)
        hit = jnp.where(m > 0.5 * NEG, idx, -1.0) == jf
        selm = jnp.where(hit, 1.0, selm)
        score = jnp.where(hit, NEG, score)
    return selm


def _nsa_prompt_kernel(q_ref, kc_ref, vct_ref, ksel_ref, vselt_ref, kwin_ref, vwint_ref, onehot_ref, small_ref, sza_ref,
                       selmapt_ref, o_ref, flags_ref, m_ref, l_ref, acc_ref, *, kt):
    qb = Q_BLOCK
    hd = HEAD_DIM
    q0 = pl.program_id(1) * qb
    n_cmp = kc_ref.shape[0]
    cols = NSA_GROUP * qb
    tq = q0 + lax.broadcasted_iota(jnp.int32, (1, qb), 1)
    gates_t = jax.nn.sigmoid(small_ref[...]).T
    wlen = WINDOW + qb
    w0 = pl.multiple_of(jnp.maximum(q0 - WINDOW, 0), qb)
    tile4 = lambda a: jnp.concatenate([a] * NSA_GROUP, axis=1)

    def keymask(pos_col, lo):
        dist = tq - pos_col
        ok = (dist >= 0) if lo is None else ((dist >= 0) & (dist < lo))
        return tile4(jnp.where(ok, 0.0, MASKV).astype(F32))

    cmp_bias = keymask(lax.broadcasted_iota(jnp.int32, (n_cmp, 1), 0) * CMP_STRIDE + (CMP_BLOCK - 1), None)
    win_bias = keymask(w0 + lax.broadcasted_iota(jnp.int32, (wlen, 1), 0), WINDOW)
    jlast = q0 // kt
    klast = pl.multiple_of(jlast * kt, kt)
    diag_bias = keymask(klast + lax.broadcasted_iota(jnp.int32, (kt, 1), 0), None)

    groups = range(NSA_KV_HEADS)
    kcols = [slice(g * AUG, (g + 1) * AUG) for g in groups]
    vrows = [slice(g * hd, (g + 1) * hd) for g in groups]
    q_heads = [[q_ref[:, (g * NSA_GROUP + h) * AUG:(g * NSA_GROUP + h + 1) * AUG] for h in range(NSA_GROUP)]
               for g in groups]
    q_cat = [jnp.concatenate(q_heads[g], axis=0) for g in groups]
    finish = lambda carry: carry[2] * (1.0 / jnp.maximum(carry[1], 1e-30))

    o_w = [finish(_flash_step_t(_flash_init_t(cols, hd), _dot(kwin_ref[pl.ds(w0, wlen), kcols[g]], q_cat[g], _NT) + win_bias,
                                vwint_ref[vrows[g], pl.ds(w0, wlen)])) for g in groups]

    o_c, q_full, psums = [], [], []
    sm_t = selmapt_ref[...]
    for g in groups:
        s_t = _dot(kc_ref[:, kcols[g]], q_cat[g], _NT) + cmp_bias
        m = jnp.maximum(jnp.max(s_t, axis=0, keepdims=True), NEG)
        p = jnp.exp(s_t - m)
        p = p * (1.0 / jnp.maximum(jnp.sum(p, axis=0, keepdims=True), 1e-30))
        o_c.append(_dot(vct_ref[vrows[g], :], p.astype(BF16)))
        psum = p[:, 0:qb]
        for h in range(1, NSA_GROUP):
            psum = psum + p[:, h * qb:(h + 1) * qb]
        psums.append(psum)
    psum = jnp.concatenate(psums, axis=1)
    ph, pl_ = _split(psum)
    pl2 = (psum - ph.astype(F32) - pl_.astype(F32)).astype(BF16)
    imp_t = _dot(sm_t, ph) + _dot(sm_t, pl_) + _dot(sm_t, pl2)
    selm_t = _select_blocks_t(imp_t, jnp.concatenate([tq] * NSA_KV_HEADS, axis=1))
    for g in groups:
        selm = selm_t[:, g * qb:(g + 1) * qb].T
        blk_bias = jnp.where(selm > 0.5, 0.0, MASKV).astype(BF16)
        q_full.append(jnp.concatenate([jnp.concatenate([qh, blk_bias], axis=1) for qh in q_heads[g]], axis=0))
        blk_any = jnp.max(selm, axis=0, keepdims=True)
        bpt = kt // SEL_BLOCK
        for j in range(flags_ref.shape[1]):
            flags_ref[g, j] = (jnp.max(blk_any[:, j * bpt:(j + 1) * bpt]) > 0.5).astype(jnp.int32)
        for ref, val in zip((m_ref, l_ref, acc_ref), _flash_init_t(cols, hd)):
            ref[g] = val

    def sel_step(k0, g, bias):
        k_op = jnp.concatenate([ksel_ref[pl.ds(k0, kt), kcols[g]], onehot_ref[pl.ds(k0, kt), :]], axis=1)
        s_t = _dot(k_op, q_full[g], _NT)
        s_t = s_t if bias is None else s_t + bias
        m, l, acc = _flash_step_t((m_ref[g], l_ref[g], acc_ref[g]), s_t, vselt_ref[vrows[g], pl.ds(k0, kt)])
        m_ref[g], l_ref[g], acc_ref[g] = m, l, acc

    def sel_tile(j, c):
        for g in groups:
            @pl.when(flags_ref[g, j] > 0)
            def _():
                sel_step(pl.multiple_of(j * kt, kt), g, None)
        return c

    lax.fori_loop(0, jlast, sel_tile, 0)
    o_s = []
    for g in groups:
        sel_step(klast, g, diag_bias)
        o_s.append(finish((m_ref[g], l_ref[g], acc_ref[g])))

    outs = []
    for g in groups:
        for h in range(NSA_GROUP):
            c = 3 * (g * NSA_GROUP + h)
            r = slice(h * qb, (h + 1) * qb)
            outs.append(gates_t[c:c + 1, :] * o_c[g][:, r] + gates_t[c + 1:c + 2, :] * o_s[g][:, r]
                        + gates_t[c + 2:c + 3, :] * o_w[g][:, r])
    o_ref[...] = jnp.concatenate(outs, axis=0).T * sza_ref[...]


def _nsa_prompt(q_aug, kc_aug, vc_t, ksel, vsel_t, kwin, vwin_t, small, sza, batch):
    n = q_aug.shape[0]
    seq = n // batch
    nq = seq // Q_BLOCK
    n_cmp = kc_aug.shape[1]
    n_blk = -(-seq // SEL_BLOCK)
    assert n_blk <= AUG and seq % Q_BLOCK == 0 and seq >= WINDOW + Q_BLOCK
    kt = min(512, seq)
    selmap_t = _sel_map(n_cmp, n_blk, AUG).T
    pos = np.arange(seq)[:, None] // SEL_BLOCK
    onehot = jnp.asarray(pos == np.arange(AUG)[None, :], dtype=BF16)
    vw = NSA_KV_HEADS * HEAD_DIM
    kw = NSA_KV_HEADS * AUG
    row = lambda w: pl.BlockSpec((Q_BLOCK, w), lambda b, i: (b * nq + i, 0))
    kspec = pl.BlockSpec((None, seq, kw), lambda b, i: (b, 0, 0))
    vspec = pl.BlockSpec((None, vw, seq), lambda b, i: (b, 0, 0))
    return pl.pallas_call(
        functools.partial(_nsa_prompt_kernel, kt=kt),
        grid=(batch, nq),
        in_specs=[row(NSA_HEADS * AUG),
                  pl.BlockSpec((None, n_cmp, kw), lambda b, i: (b, 0, 0)),
                  pl.BlockSpec((None, vw, n_cmp), lambda b, i: (b, 0, 0)),
                  kspec, vspec, kspec, vspec,
                  pl.BlockSpec(onehot.shape, lambda b, i: (0, 0)),
                  row(SMALL_W), row(NSA_WIDTH),
                  pl.BlockSpec(selmap_t.shape, lambda b, i: (0, 0))],
        out_specs=row(NSA_WIDTH),
        out_shape=jax.ShapeDtypeStruct((n, NSA_WIDTH), F32),
        scratch_shapes=[pltpu.SMEM((NSA_KV_HEADS, seq // kt), jnp.int32),
                        pltpu.VMEM((NSA_KV_HEADS, 1, NSA_GROUP * Q_BLOCK), F32),
                        pltpu.VMEM((NSA_KV_HEADS, 1, NSA_GROUP * Q_BLOCK), F32),
                        pltpu.VMEM((NSA_KV_HEADS, HEAD_DIM, NSA_GROUP * Q_BLOCK), F32)],
        compiler_params=_cparams(("parallel", "arbitrary")),
        name="nsa_prompt",
    )(q_aug, kc_aug, vc_t, ksel.reshape(batch, seq, kw), vsel_t, kwin.reshape(batch, seq, kw), vwin_t, onehot, small, sza,
      selmap_t)


def _softplus(x):
    return jnp.maximum(x, 0.0) + jnp.log1p(jnp.exp(-jnp.abs(x)))


def _gdn_prompt_kernel(qkv_ref, small_ref, szb_ref, cw_ref, cb_ref, alog_ref, dtb_ref, gn_ref,
                       o_ref, sfin_ref, s_ref, prev_ref):
    c = pl.program_id(1)
    ck = GDN_CHUNK
    dk = GDN_DK

    @pl.when(c == 0)
    def _():
        s_ref[...] = jnp.zeros_like(s_ref)
        prev_ref[...] = jnp.zeros_like(prev_ref)

    u = qkv_ref[...]
    ext = jnp.concatenate([prev_ref[...], u], axis=0)
    y = cb_ref[...]
    for i in range(CONV_W):
        y = y + ext[8 - (CONV_W - 1) + i:8 - (CONV_W - 1) + i + ck] * cw_ref[i:i + 1, :]
    prev_ref[...] = u[ck - 8:ck]
    act = _silu(y)

    small = small_ref[...]
    g_all = -jnp.exp(alog_ref[...]) * _softplus(small + dtb_ref[...])
    beta_all = jax.nn.sigmoid(small)
    ii = lax.broadcasted_iota(jnp.int32, (ck, ck), 0)
    jj = lax.broadcasted_iota(jnp.int32, (ck, ck), 1)
    tri = jnp.where(ii >= jj, 1.0, 0.0).astype(F32)
    gc_all = lax.dot_general(tri, g_all, _NN, precision=lax.Precision.HIGHEST, preferred_element_type=F32)
    gc_t = gc_all.T

    pairs = range(GDN_HEADS // 2)
    pw2 = 2 * dk
    lane = lax.broadcasted_iota(jnp.int32, (ck, pw2), 1)
    left = lane < dk
    row2 = lax.broadcasted_iota(jnp.int32, (ck, pw2), 0)
    col2 = jnp.where(left, lane, lane - dk)
    eye2 = jnp.where(row2 == col2, 1.0, 0.0).astype(F32)

    def halves(col_l, col_r):
        return jnp.where(left, col_l, col_r)

    def head_norm(x):
        sq = x * x
        return halves(jnp.sum(jnp.where(left, sq, 0.0), axis=-1, keepdims=True),
                      jnp.sum(jnp.where(left, 0.0, sq), axis=-1, keepdims=True))

    def bd(b):
        return _split(jnp.concatenate([jnp.where(left, b, 0.0), jnp.where(left, 0.0, b)], axis=0))

    q, k, v, beta, gc, decay, g_last = [], [], [], [], [], [], []
    for p in pairs:
        qp = act[:, p * pw2:(p + 1) * pw2]
        kp = act[:, GDN_QK_WIDTH + p * pw2:GDN_QK_WIDTH + (p + 1) * pw2]
        q.append(qp * lax.rsqrt(head_norm(qp) + EPS) * (dk ** -0.5))
        k.append(kp * lax.rsqrt(head_norm(kp) + EPS))
        v.append(act[:, 2 * GDN_QK_WIDTH + p * pw2:2 * GDN_QK_WIDTH + (p + 1) * pw2])
        h0, h1 = 2 * p, 2 * p + 1
        beta.append(halves(beta_all[:, SMALL_B0 + h0:SMALL_B0 + h0 + 1], beta_all[:, SMALL_B0 + h1:SMALL_B0 + h1 + 1]))
        gcp = halves(gc_all[:, SMALL_A0 + h0:SMALL_A0 + h0 + 1], gc_all[:, SMALL_A0 + h1:SMALL_A0 + h1 + 1])
        grp = jnp.concatenate([gc_t[SMALL_A0 + h0:SMALL_A0 + h0 + 1, :], gc_t[SMALL_A0 + h1:SMALL_A0 + h1 + 1, :]], axis=1)
        gc.append(gcp)
        g_last.append(gcp[ck - 1:ck, :])
        decay.append(jnp.exp(jnp.where(row2 >= col2, gcp - grp, NEG)))
    kb = [k[p] * beta[p] for p in pairs]
    kbd = [bd(k[p]) for p in pairs]
    qk = [_dot3s(_split(jnp.concatenate([q[p], kb[p]], axis=0)), kbd[p], _NT) for p in pairs]
    a_in = [qk[p][0:ck] * decay[p] for p in pairs]
    lmat = [jnp.where(row2 > col2, qk[p][ck:2 * ck] * decay[p], 0.0) for p in pairs]
    x = [eye2 - lmat[p] for p in pairs]
    pw = lmat
    for it in range(int(math.log2(ck)) - 1):
        if it < 3:
            pw = [_dot3s(_split(pw[p]), bd(pw[p])) for p in pairs]
            x = [x[p] + _dot3s(_split(x[p]), bd(pw[p])) for p in pairs]
        else:
            pw = [_dot(pw[p].astype(BF16), bd(pw[p])[0]) for p in pairs]
            x = [x[p] + _dot(x[p].astype(BF16), bd(pw[p])[0]) for p in pairs]
    eg = [jnp.exp(gc[p]) for p in pairs]
    xs = [_split(x[p]) for p in pairs]
    uu = [_dot3s(xs[p], bd(v[p] * beta[p])) for p in pairs]
    ww = [_dot3s(xs[p], bd(kb[p] * eg[p])) for p in pairs]
    s_old = [s_ref[p] for p in pairs]
    qw_s = [_dot3s(_split(jnp.concatenate([q[p] * eg[p], ww[p]], axis=0)), bd(s_old[p])) for p in pairs]
    v_new = [uu[p] - qw_s[p][ck:2 * ck] for p in pairs]
    o = [qw_s[p][0:ck] + _dot(a_in[p].astype(BF16), bd(v_new[p])[0]) for p in pairs]
    gn2 = jnp.concatenate([gn_ref[...]] * 2, axis=1)
    for p in pairs:
        kd = k[p] * jnp.exp(g_last[p] - gc[p])
        r = _dot3s(_split(kd.T), _split(v_new[p]))
        s_ref[p] = s_old[p] * jnp.exp(g_last[p]) + jnp.where(left, r[0:dk], r[dk:pw2])
        on = o[p] * lax.rsqrt(head_norm(o[p]) * (1.0 / GDN_DV) + EPS) * gn2
        o_ref[:, p * pw2:(p + 1) * pw2] = on * szb_ref[:, p * pw2:(p + 1) * pw2]

    @pl.when(c == pl.num_programs(1) - 1)
    def _():
        for p in pairs:
            sp = s_ref[p]
            sfin_ref[2 * p] = sp[:, 0:GDN_DV]
            sfin_ref[2 * p + 1] = sp[:, GDN_DV:2 * GDN_DV]


def _lane_params(a_log, dt_bias):
    alog_l = jnp.zeros((1, SMALL_W), F32).at[0, SMALL_A0:SMALL_A0 + GDN_HEADS].set(a_log)
    dtb_l = jnp.zeros((1, SMALL_W), F32).at[0, SMALL_A0:SMALL_A0 + GDN_HEADS].set(dt_bias)
    return alog_l, dtb_l


def _gdn_prompt(qkvb, small, szb, conv_w, conv_b, a_log, dt_bias, gnorm, batch):
    n = qkvb.shape[0]
    seq = n // batch
    nc = seq // GDN_CHUNK
    alog_l, dtb_l = _lane_params(a_log, dt_bias)
    row = lambda w: pl.BlockSpec((GDN_CHUNK, w), lambda b, c: (b * nc + c, 0))
    full = lambda a: pl.BlockSpec(a.shape, lambda b, c: (0,) * a.ndim)
    cb = conv_b.reshape(1, -1)
    gn = gnorm.reshape(1, -1)
    return pl.pallas_call(
        _gdn_prompt_kernel,
        grid=(batch, nc),
        in_specs=[row(CONV_DIM), row(SMALL_W), row(GDN_WIDTH), full(conv_w), full(cb), full(alog_l), full(dtb_l), full(gn)],
        out_specs=[row(GDN_WIDTH),
                   pl.BlockSpec((None, GDN_HEADS, GDN_DK, GDN_DV), lambda b, c: (b, 0, 0, 0))],
        out_shape=[jax.ShapeDtypeStruct((n, GDN_WIDTH), F32),
                   jax.ShapeDtypeStruct((batch, GDN_HEADS, GDN_DK, GDN_DV), F32)],
        scratch_shapes=[pltpu.VMEM((GDN_HEADS // 2, GDN_DK, 2 * GDN_DV), F32), pltpu.VMEM((8, CONV_DIM), F32)],
        compiler_params=_cparams(("parallel", "arbitrary")),
        name="gdn_prompt",
    )(qkvb, small, szb, conv_w, cb, alog_l, dtb_l, gn)


def _prep_w_in(w_in, d_model):
    sizes = (NSA_WIDTH, 6 * NSA_KV_HEADS * HEAD_DIM, 3 * NSA_HEADS, NSA_WIDTH, CONV_DIM, GDN_HEADS, GDN_HEADS,
             GDN_WIDTH, 2 * d_model)
    pts = np.cumsum(np.array(sizes))[:-1].tolist()
    q_a, kv_a, g_a, z_a, qkv_b, a_b, b_b, z_b, gm = jnp.split(w_in, pts, axis=1)
    small = jnp.concatenate([g_a, a_b, b_b], axis=1)
    small = jnp.pad(small, ((0, 0), (0, SMALL_W - small.shape[1])))
    return jnp.concatenate([q_a, kv_a, z_a, qkv_b, z_b, gm, small], axis=1).astype(BF16)


def _kv_leaf(kv_t):
    b, _, t = kv_t.shape
    return kv_t.reshape(1, b, 2, NSA_KV_HEADS, HEAD_DIM, t).transpose(0, 1, 5, 2, 3, 4)


def _kv_rows_t(cache):
    n, r = cache.shape[:2]
    return cache.transpose(0, 2, 3, 4, 1).reshape(n, KV_ROW, r)


def _prompt_path(x, ada, lw, cw):
    (norm_g, w_bf, offs, conv_w, conv_b, a_log, dt_bias, gnorm, wa, wb, wo, final_g) = lw
    batch, seq, d = x.shape
    n = batch * seq
    x2 = x.reshape(n, d)
    ada3 = ada.reshape(batch, 1, 3 * d)
    tm = 256
    mod = lambda k: pl.BlockSpec((None, 1, d), lambda i: (i * tm // seq, 0, k))
    (q_aug, kvc_t, kvs_t, kvw_t, kvcb, ksel, vsel_t, kwin, vwin_t, sza, qkvb, szb, gms, small) = _inproj(
        x2, ada3, ada3, (mod(1), mod(0)), norm_g.reshape(1, d), w_bf, offs, tm, seq)
    kc_aug, vc_t = _compress_prompt(kvcb, *cw, batch)
    o_a = _nsa_prompt(q_aug, kc_aug, vc_t, ksel, vsel_t, kwin, vwin_t, small, sza, batch)
    o_b, s_new = _gdn_prompt(qkvb, small, szb, conv_w, conv_b, a_log, dt_bias, gnorm, batch)
    tmo = 512
    gate_spec = pl.BlockSpec((None, 1, d), lambda i: (i * tmo // seq, 0, 2))
    y = _outproj(x2, o_a, o_b, gms, ada3, gate_spec, wa, wb, wo, final_g.reshape(1, d), tmo)
    keep = min(WINDOW, seq)
    new_conv = qkvb.reshape(batch, seq, CONV_DIM)[None, :, seq - (CONV_W - 1):]
    return (y.reshape(batch, seq, d), _kv_leaf(kvc_t), _kv_leaf(kvs_t), _kv_leaf(kvw_t[:, :, seq - keep:]), new_conv,
            s_new[None])


def _page_fetch(pt_ref, cache_hbm, buf, sem, npages):
    def copy(seq, slot, j):
        c0 = pl.multiple_of(j * PAGE_SIZE, PAGE_SIZE)
        return pltpu.make_async_copy(cache_hbm.at[pt_ref[seq, j]], buf.at[slot, :, pl.ds(c0, PAGE_SIZE)], sem.at[slot])

    def start(seq, slot):
        def body(j, c):
            copy(seq, slot, j).start()
            return c
        lax.fori_loop(0, npages, body, 0)

    def wait(seq, slot):
        for j in range(npages):
            copy(seq, slot, j).wait()

    return start, wait


def _fetch_this_prefetch_next(start, wait):
    b = pl.program_id(0)
    nb = pl.num_programs(0)

    @pl.when(b == 0)
    def _():
        start(0, 0)

    @pl.when(b + 1 < nb)
    def _():
        start(b + 1, (b + 1) % 2)

    slot = b % 2
    wait(b, slot)
    return slot


def _compress_sample_kernel(pt_ref, cache_hbm, perm_ref, w1_ref, pe_ref, w2_ref, o_ref, buf, sem, x_ref, xc_ref, *, npages):
    start, wait = _page_fetch(pt_ref, cache_hbm, buf, sem, npages)
    slot = _fetch_this_prefetch_next(start, wait)
    past = npages * PAGE_SIZE
    n16 = past // CMP_STRIDE
    gt = perm_ref.shape[0]
    cps = gt // CMP_STRIDE

    def regroup(j, c):
        t0 = pl.multiple_of(j * gt, gt)
        x_t = buf[slot, :, pl.ds(t0, gt)].astype(BF16)
        x_ref[j] = _dot(perm_ref[...], x_t, _NT).astype(BF16)
        return c

    lax.fori_loop(0, past // gt, regroup, 0, unroll=min(8, past // gt))
    half = KV_ROW // 2
    kw = CMP_STRIDE * half
    for s in range(CMP_STRIDE):
        xs = x_ref[:, s * cps:(s + 1) * cps, :].reshape(n16, KV_ROW)
        for kv in range(2):
            xc_ref[:, kv * kw + s * half:kv * kw + (s + 1) * half] = xs[:, kv * half:(kv + 1) * half]
    outs = []
    for kv in range(2):
        a = _dot(xc_ref[:, kv * kw:(kv + 1) * kw], w1_ref[kv])
        p0 = _dot(pe_ref[kv, 0], w1_ref[kv, :, 0:half]) + _dot(pe_ref[kv, 1], w1_ref[kv, :, half:2 * half])
        pre = a[:, 0:half] + pltpu.roll(a[:, half:2 * half], n16 - 1, 0) + p0[0:1]
        outs.append(_dot(_silu(pre).astype(BF16), w2_ref[kv]))
    o_ref[...] = jnp.concatenate(outs, axis=1).astype(BF16)


def _compress_weights_kv(pe_k, w1_k, w2_k, pe_v, w1_v, w2_v):
    r_cnt = CMP_BLOCK // CMP_STRIDE
    g = NSA_KV_HEADS
    eyeg = jnp.eye(g, dtype=F32)
    w1 = jnp.stack([w1_k, w1_v]).reshape(2, r_cnt, CMP_STRIDE, HEAD_DIM, CMP_HIDDEN)
    w1s = jnp.einsum("krsde,gG->ksgdrGe", w1, eyeg)
    w1s = w1s.reshape(2, CMP_STRIDE * g * HEAD_DIM, r_cnt * g * CMP_HIDDEN).astype(BF16)
    pe = jnp.stack([pe_k, pe_v]).reshape(2, r_cnt, CMP_STRIDE, 1, HEAD_DIM)
    pes = jnp.broadcast_to(pe, (2, r_cnt, CMP_STRIDE, g, HEAD_DIM)).reshape(2, r_cnt, 1, CMP_STRIDE * g * HEAD_DIM)
    pes = jnp.pad(pes, ((0, 0), (0, 0), (0, 7), (0, 0))).astype(BF16)
    w2s = jnp.einsum("ked,gG->kgeGd", jnp.stack([w2_k, w2_v]), eyeg).reshape(2, g * CMP_HIDDEN, g * HEAD_DIM).astype(BF16)
    return w1s, pes, w2s


def _compress_sample(cache_t, page_table, w1big, pebig, w2big):
    nseq, npages = page_table.shape
    past = npages * PAGE_SIZE
    n16 = past // CMP_STRIDE
    gt = 2 * PAGE_SIZE
    assert past % gt == 0 and gt // CMP_STRIDE == 16
    row = np.arange(gt)
    perm_t = jnp.asarray((row[:, None] % (gt // CMP_STRIDE)) * CMP_STRIDE + row[:, None] // (gt // CMP_STRIDE)
                         == row[None, :], dtype=BF16)
    grid_spec = pltpu.PrefetchScalarGridSpec(
        num_scalar_prefetch=1,
        grid=(nseq,),
        in_specs=[pl.BlockSpec(memory_space=pl.ANY),
                  pl.BlockSpec(perm_t.shape, lambda b, pt: (0, 0)),
                  pl.BlockSpec(w1big.shape, lambda b, pt: (0, 0, 0)),
                  pl.BlockSpec(pebig.shape, lambda b, pt: (0, 0, 0, 0)),
                  pl.BlockSpec(w2big.shape, lambda b, pt: (0, 0, 0))],
        out_specs=pl.BlockSpec((None, n16, KV_ROW), lambda b, pt: (b, 0, 0)),
        scratch_shapes=[pltpu.VMEM((2, KV_ROW, past), F32), pltpu.SemaphoreType.DMA((2,)),
                        pltpu.VMEM((past // gt, gt, KV_ROW), BF16),
                        pltpu.VMEM((n16, CMP_STRIDE * KV_ROW), BF16)],
    )
    return pl.pallas_call(
        functools.partial(_compress_sample_kernel, npages=npages),
        grid_spec=grid_spec,
        out_shape=jax.ShapeDtypeStruct((nseq, n16, KV_ROW), BF16),
        compiler_params=_cparams(("arbitrary",)),
        name="compress_sample",
    )(page_table, cache_t, perm_t, w1big, pebig, w2big)


def _nsa_sample_kernel(pt_ref, q_ref, kc_ref, cache_hbm, win_ref, nsel_ref, nwin_ref, nwint_ref, small_ref, sza_ref,
                       selmap_ref, onehot_ref, o_ref, wout_ref, buf, sem, *, npages, tn, kt):
    hd = HEAD_DIM
    rt = 8
    start, wait = _page_fetch(pt_ref, cache_hbm, buf, sem, npages)
    slot = _fetch_this_prefetch_next(start, wait)
    past = npages * PAGE_SIZE
    nbuf = win_ref.shape[1]
    n_cmp = kc_ref.shape[0]
    n_blk_lanes = selmap_ref.shape[1]
    q8 = q_ref[...].astype(F32)
    gates = jax.nn.sigmoid(small_ref[...])
    tau = lax.broadcasted_iota(jnp.int32, (rt, 1), 0) % tn
    t_col = past + tau
    t4 = jnp.concatenate([t_col] * NSA_GROUP, axis=0)
    rows = NSA_GROUP * rt
    newcol = lax.broadcasted_iota(jnp.int32, (1, SMALL_W), 1)
    new_dist = t4 - (past + newcol)
    new_ok = (newcol < tn) & (new_dist >= 0)
    zpad = jnp.zeros((SMALL_W - rt, hd), F32)
    groups = range(NSA_KV_HEADS)
    kcols = [slice(g * hd, (g + 1) * hd) for g in groups]
    vcols = [slice((NSA_KV_HEADS + g) * hd, (NSA_KV_HEADS + g + 1) * hd) for g in groups]

    def new_tile(ref, col):
        return jnp.concatenate([ref[:, col], zpad], axis=0).astype(BF16)

    def pos_cols(p0, n):
        pos = (p0 + lax.broadcasted_iota(jnp.int32, (n, 1), 0)).astype(F32)
        hi = jnp.floor(pos * (1.0 / POS_SPLIT))
        return _aug_cols(n, hi, pos - hi * POS_SPLIT)

    qgs, slopes, o_cs, imps = [], [], [], []
    for g in groups:
        kcol, vcol = kcols[g], vcols[g]
        qg = jnp.concatenate([q8[:, (g * NSA_GROUP + h) * AUG:(g * NSA_GROUP + h) * AUG + hd]
                              for h in range(NSA_GROUP)], axis=0).astype(BF16)
        slope = jnp.concatenate([jnp.full((rt, 1), _SLOPES[g * NSA_GROUP + h], F32) for h in range(NSA_GROUP)], axis=0)

        s = _dot(qg, kc_ref[:, kcol], _NT)
        end = lax.broadcasted_iota(jnp.int32, (1, n_cmp), 1) * CMP_STRIDE + (CMP_BLOCK - 1)
        dist = t4 - end
        mask = dist >= 0
        _, p = _softmax_block(s - slope * dist.astype(F32), mask)
        p = p / jnp.maximum(jnp.sum(p, axis=-1, keepdims=True), 1e-30)
        o_c = _dot(p.astype(BF16), kc_ref[:, vcol])
        psum = p[0:rt]
        for h in range(1, NSA_GROUP):
            psum = psum + p[h * rt:(h + 1) * rt]
        for lst, val in ((qgs, qg), (slopes, slope), (o_cs, o_c), (imps, _importance(psum, selmap_ref[...]))):
            lst.append(val)

    ncol = NSA_KV_HEADS * rt
    imp_pad = jnp.concatenate(imps + [jnp.zeros((SMALL_W - ncol, n_blk_lanes), F32)], axis=0)
    tq_row = past + lax.broadcasted_iota(jnp.int32, (1, SMALL_W), 1) % tn
    selm_all = _select_blocks_t(imp_pad.T, tq_row).T

    selms, q_augs, q_fulls = [], [], []
    for g in groups:
        selm = selm_all[g * rt:(g + 1) * rt]
        q_aug = jnp.concatenate([q8[:, (g * NSA_GROUP + h) * AUG:(g * NSA_GROUP + h + 1) * AUG]
                                 for h in range(NSA_GROUP)], axis=0)
        blk_bias = jnp.where(selm > 0.5, 0.0, MASKV)
        q_full = jnp.concatenate([q_aug, jnp.concatenate([blk_bias] * NSA_GROUP, axis=0)], axis=1).astype(BF16)
        for lst, val in ((selms, selm), (q_augs, q_aug), (q_fulls, q_full)):
            lst.append(val)

    bpt = kt // SEL_BLOCK
    carries = [_online_init(rows, hd) for _ in groups]
    for j in range(past // kt):
        k0 = j * kt
        for g in groups:
            k_t = buf[slot, kcols[g], k0:k0 + kt].astype(BF16)
            v_t = buf[slot, vcols[g], k0:k0 + kt].astype(BF16)
            q_rest = jnp.concatenate([q_fulls[g][:, hd:AUG], q_fulls[g][:, AUG + j * bpt:AUG + (j + 1) * bpt]], axis=1)
            s = _dot(q_fulls[g][:, 0:hd], k_t) + _dot(q_rest, onehot_ref[j])
            carries[g] = _flash_step(carries[g], s, v_t, _NT)

    outs = []
    for g in groups:
        kcol, vcol = kcols[g], vcols[g]
        qg, slope, o_c, selm, q_aug = qgs[g], slopes[g], o_cs[g], selms[g], q_augs[g]
        nb_new = past // SEL_BLOCK
        new_sel = jnp.concatenate([selm[:, nb_new:nb_new + 1]] * NSA_GROUP, axis=0) > 0.5
        kn = jnp.concatenate([jnp.concatenate([nsel_ref[:, kcol], zpad], axis=0), pos_cols(past, SMALL_W)], axis=1)
        s = _dot(q_aug.astype(BF16), kn.astype(BF16), _NT) + jnp.where(new_ok & new_sel, 0.0, MASKV)
        _, l, acc = _flash_step(carries[g], s, new_tile(nsel_ref, vcol))
        o_s = acc / jnp.maximum(l, 1e-30)

        s = _dot(qg, win_ref[kcol, :].astype(BF16))
        dist = t4 - (past - nbuf + lax.broadcasted_iota(jnp.int32, (1, nbuf), 1))
        mask = (dist >= 0) & (dist < WINDOW)
        carry = _online_update(_online_init(rows, hd), s - slope * dist.astype(F32), mask,
                               win_ref[vcol, :].astype(BF16), _NT)
        s = _dot(qg, new_tile(nwin_ref, kcol), _NT)
        _, l, acc = _online_update(carry, s - slope * new_dist.astype(F32), new_ok, new_tile(nwin_ref, vcol))
        o_w = acc / jnp.maximum(l, 1e-30)

        for h in range(NSA_GROUP):
            c = 3 * (g * NSA_GROUP + h)
            r = slice(h * rt, (h + 1) * rt)
            outs.append(gates[:, c:c + 1] * o_c[r] + gates[:, c + 1:c + 2] * o_s[r] + gates[:, c + 2:c + 3] * o_w[r])
    o_ref[...] = jnp.concatenate(outs, axis=1) * sza_ref[...]
    wout_ref[...] = jnp.concatenate([win_ref[:, tn:nbuf], nwint_ref[...]], axis=1)


def _nsa_sample(qs, kc, cache_t, win_t, kvs, kvw, small, sza, page_table, tn):
    nseq, npages = page_table.shape
    past = npages * PAGE_SIZE
    nbuf = win_t.shape[2]
    assert nbuf == WINDOW and 8 % tn == 0 and tn <= SEL_BLOCK
    n_cmp = kc.shape[1]
    n_blk = -(-(past + tn) // SEL_BLOCK)
    n_blk_lanes = -(-n_blk // 128) * 128
    selmap = _sel_map(n_cmp, n_blk, n_blk_lanes)
    kt = min(4096, past)
    ntile, bpt = past // kt, kt // SEL_BLOCK
    key = np.arange(past).reshape(ntile, 1, kt)
    pos_rows = np.zeros((ntile, AUG - HEAD_DIM, kt), np.float32)
    pos_rows[:, 0], pos_rows[:, 1] = key[:, 0] // POS_SPLIT, key[:, 0] % POS_SPLIT
    member = (key % kt // SEL_BLOCK == np.arange(bpt)[None, :, None]).astype(np.float32)
    onehot_t = jnp.asarray(np.concatenate([pos_rows, member], axis=1), dtype=BF16)
    seq3 = lambda r, w: pl.BlockSpec((None, r, w), lambda b, pt: (b, 0, 0))
    grid_spec = pltpu.PrefetchScalarGridSpec(
        num_scalar_prefetch=1,
        grid=(nseq,),
        in_specs=[seq3(8, NSA_HEADS * AUG), seq3(n_cmp, KV_ROW), pl.BlockSpec(memory_space=pl.ANY), seq3(KV_ROW, nbuf),
                  seq3(8, KV_ROW), seq3(8, KV_ROW), seq3(KV_ROW, tn), seq3(8, SMALL_W), seq3(8, NSA_WIDTH),
                  pl.BlockSpec(selmap.shape, lambda b, pt: (0, 0)),
                  pl.BlockSpec(onehot_t.shape, lambda b, pt: (0, 0, 0))],
        out_specs=[seq3(8, NSA_WIDTH), seq3(KV_ROW, nbuf)],
        scratch_shapes=[pltpu.VMEM((2, KV_ROW, past), F32), pltpu.SemaphoreType.DMA((2,))],
    )
    rep8 = lambda a: jnp.concatenate([a.reshape(nseq, tn, a.shape[-1])] * (8 // tn), axis=1)
    o8, win_new_t = pl.pallas_call(
        functools.partial(_nsa_sample_kernel, npages=npages, tn=tn, kt=kt),
        grid_spec=grid_spec,
        out_shape=[jax.ShapeDtypeStruct((nseq, 8, NSA_WIDTH), F32), jax.ShapeDtypeStruct((nseq, KV_ROW, nbuf), F32)],
        compiler_params=_cparams(("arbitrary",)),
        name="nsa_sample",
    )(page_table, rep8(qs), kc, cache_t, win_t, rep8(kvs), rep8(kvw), kvw.transpose(0, 2, 1), rep8(small), rep8(sza),
      selmap, onehot_t)
    return o8[:, :tn].reshape(nseq * tn, NSA_WIDTH), win_new_t


def _gdn_sample_kernel(eq_ref, ek_ref, ev_ref, cwq_ref, cwk_ref, cwv_ref, cbq_ref, cbk_ref, cbv_ref, small_ref,
                       alog_ref, dtb_ref, szb_ref, gn_ref, s_ref, o_ref, so_ref, qs_ref, ks_ref):
    h = pl.program_id(0)
    tn = o_ref.shape[0]
    dk, dv, nseq = so_ref.shape

    def conv(e_ref, cw_ref, cb_ref, t):
        y = cb_ref[...]
        for i in range(CONV_W):
            y = y + e_ref[t + i] * cw_ref[:, i:i + 1]
        return _silu(y)

    so_ref[...] = s_ref[...]
    neg_rate = -jnp.exp(alog_ref[...])
    for t in range(tn):
        q = conv(eq_ref, cwq_ref, cbq_ref, t)
        k = conv(ek_ref, cwk_ref, cbk_ref, t)
        v = conv(ev_ref, cwv_ref, cbv_ref, t)
        qs_ref[...] = q * lax.rsqrt(jnp.sum(q * q, axis=0, keepdims=True) + EPS) * (dk ** -0.5)
        ks_ref[...] = k * lax.rsqrt(jnp.sum(k * k, axis=0, keepdims=True) + EPS)
        a_in = small_ref[t, pl.ds(SMALL_A0 + h, 1), :]
        b_in = small_ref[t, pl.ds(SMALL_B0 + h, 1), :]
        decay = jnp.exp(neg_rate * _softplus(a_in + dtb_ref[...]))
        beta = jax.nn.sigmoid(b_in)

        def ks_step(i, acc):
            return acc + ks_ref[pl.ds(i, 1), :] * so_ref[i]

        k_s = lax.fori_loop(0, dk, ks_step, jnp.zeros((dv, nseq), F32), unroll=8)
        delta = beta * (v - decay * k_s)

        def upd_step(i, acc):
            s_new = decay * so_ref[i] + ks_ref[pl.ds(i, 1), :] * delta
            so_ref[i] = s_new
            return acc + qs_ref[pl.ds(i, 1), :] * s_new

        o = lax.fori_loop(0, dk, upd_step, jnp.zeros((dv, nseq), F32), unroll=8)
        o = o * lax.rsqrt(jnp.mean(o * o, axis=0, keepdims=True) + EPS) * gn_ref[...]
        o_ref[t] = o * szb_ref[t]


def _gdn_sample(qkvb, small, szb, state_conv, state_gdn, conv_w, conv_b, a_log, dt_bias, gnorm, tn):
    nseq = state_gdn.shape[0]
    ext = jnp.concatenate([state_conv, qkvb.reshape(nseq, tn, CONV_DIM)], axis=1)
    ext_t = ext.transpose(1, 2, 0)
    small_t = small.reshape(nseq, tn, SMALL_W).transpose(1, 2, 0)
    szb_t = szb.reshape(nseq, tn, GDN_WIDTH).transpose(1, 2, 0)
    s_t = state_gdn.transpose(1, 2, 3, 0)
    cw_t = conv_w.T
    cb_t = conv_b.reshape(-1, 1)
    alog_b = jnp.broadcast_to(a_log[:, None, None], (GDN_HEADS, 1, nseq))
    dtb_b = jnp.broadcast_to(dt_bias[:, None, None], (GDN_HEADS, 1, nseq))
    gn = gnorm.reshape(-1, 1)
    nqk = GDN_QK_WIDTH // GDN_DK
    chan = lambda off: pl.BlockSpec((CONV_W - 1 + tn, GDN_DK, nseq), lambda h: (0, off + h, 0))
    cwb = lambda off: pl.BlockSpec((GDN_DK, CONV_W), lambda h: (off + h, 0))
    cbb = lambda off: pl.BlockSpec((GDN_DK, 1), lambda h: (off + h, 0))
    perhead = pl.BlockSpec((None, 1, nseq), lambda h: (h, 0, 0))
    o_t, s_new = pl.pallas_call(
        _gdn_sample_kernel,
        grid=(GDN_HEADS,),
        in_specs=[chan(0), chan(nqk), chan(2 * nqk), cwb(0), cwb(nqk), cwb(2 * nqk), cbb(0), cbb(nqk), cbb(2 * nqk),
                  pl.BlockSpec((tn, SMALL_W, nseq), lambda h: (0, 0, 0)), perhead, perhead,
                  pl.BlockSpec((tn, GDN_DV, nseq), lambda h: (0, h, 0)),
                  pl.BlockSpec((GDN_DV, 1), lambda h: (0, 0)),
                  pl.BlockSpec((None, GDN_DK, GDN_DV, nseq), lambda h: (h, 0, 0, 0))],
        out_specs=[pl.BlockSpec((tn, GDN_DV, nseq), lambda h: (0, h, 0)),
                   pl.BlockSpec((None, GDN_DK, GDN_DV, nseq), lambda h: (h, 0, 0, 0))],
        out_shape=[jax.ShapeDtypeStruct((tn, GDN_WIDTH, nseq), F32),
                   jax.ShapeDtypeStruct((GDN_HEADS, GDN_DK, GDN_DV, nseq), F32)],
        scratch_shapes=[pltpu.VMEM((GDN_DK, nseq), F32), pltpu.VMEM((GDN_DK, nseq), F32)],
        compiler_params=_cparams(("parallel",)),
        name="gdn_sample",
    )(ext_t, ext_t, ext_t, cw_t, cw_t, cw_t, cb_t, cb_t, cb_t, small_t, alog_b, dtb_b, szb_t, gn, s_t)
    o_b = o_t.transpose(2, 0, 1).reshape(nseq * tn, GDN_WIDTH)
    return o_b, ext[:, tn:], s_new.transpose(3, 0, 1, 2)


def _sample_path(x, ada, lw, cw, cache_cmp, cache_sel, cache_win, state_conv, state_gdn, page_table):
    (norm_g, w_bf, offs, conv_w, conv_b, a_log, dt_bias, gnorm, wa, wb, wo, final_g) = lw
    nseq, tn, d = x.shape
    n = nseq * tn
    x2 = x.reshape(n, d)
    ada_rows = jnp.repeat(ada, tn, axis=0)
    tm = min(256, n)
    mod = lambda k: pl.BlockSpec((tm, d), lambda i: (i, k))
    (qs, kvc_t, kvs_t, kvw_t, _, _, _, _, _, sza, qkvb, szb, gms, small) = _inproj(
        x2, ada_rows, ada_rows, (mod(1), mod(0)), norm_g.reshape(1, d), w_bf, offs, tm, n)
    rows = lambda a_t: a_t.reshape(KV_ROW, nseq, tn).transpose(1, 2, 0)
    kvc, kvs, kvw = rows(kvc_t), rows(kvs_t), rows(kvw_t)
    kc = _compress_sample(_kv_rows_t(cache_cmp), page_table, *cw)
    o_a, win_new_t = _nsa_sample(qs, kc, _kv_rows_t(cache_sel), _kv_rows_t(cache_win), kvs, kvw, small, sza, page_table, tn)
    o_b, conv_new, s_new = _gdn_sample(qkvb, small, szb, state_conv, state_gdn, conv_w, conv_b, a_log, dt_bias, gnorm, tn)
    y = _outproj(x2, o_a, o_b, gms, ada_rows, mod(2), wa, wb, wo, final_g.reshape(1, d), tm)
    kvshape = (1, nseq, tn, 2, NSA_KV_HEADS, HEAD_DIM)
    return (y.reshape(nseq, tn, d), kvc.reshape(kvshape), kvs.reshape(kvshape), _kv_leaf(win_new_t),
            conv_new[None], s_new[None])


def kernel(x_prompt, x_sample, cache_cmp_kv, cache_sel_kv, cache_win_kv, state_conv, state_gdn, page_table, c_prompt, c_sample, norm_g, w_ada, b_ada, w_in, cmp_pe_k, cmp_w1_k, cmp_w2_k, cmp_pe_v, cmp_w1_v, cmp_w2_v, conv_w, conv_b, gdn_a_log, gdn_dt_bias, gdn_norm_g, w_o_nsa, w_o_gdn, w_out, final_g):
    assert norm_g.shape[0] == 1, "single trunk layer"
    d = x_prompt.shape[-1]
    l = 0
    offs, _ = _seg_offsets(d)
    w_bf = _prep_w_in(w_in[l], d)
    ada = _ada(jnp.concatenate([c_prompt, c_sample], axis=0), w_ada[l].astype(BF16), b_ada[l])
    cw = _compress_weights(cmp_pe_k[l], cmp_w1_k[l], cmp_w2_k[l], cmp_pe_v[l], cmp_w1_v[l], cmp_w2_v[l])
    lw = (norm_g[l], w_bf, offs, conv_w[l], conv_b[l], gdn_a_log[l], gdn_dt_bias[l], gdn_norm_g[l],
          w_o_nsa[l].astype(BF16), w_o_gdn[l].astype(BF16), w_out[l].astype(BF16), final_g)
    nb = c_prompt.shape[0]
    yp, cmp_p, sel_p, win_p, conv_p, gdn_p = _prompt_path(x_prompt, ada[:nb], lw, cw)
    ys, cmp_s, sel_s, win_s, conv_s, gdn_s = _sample_path(
        x_sample, ada[nb:], lw,
        _compress_weights_kv(cmp_pe_k[l], cmp_w1_k[l], cmp_w2_k[l], cmp_pe_v[l], cmp_w1_v[l], cmp_w2_v[l]),
        cache_cmp_kv[l], cache_sel_kv[l], cache_win_kv[l], state_conv[l], state_gdn[l],
        page_table)
    return (yp, ys, cmp_p, sel_p, win_p, conv_p, gdn_p, cmp_s, sel_s, win_s, conv_s, gdn_s)
```

```python
import functools
import math

import numpy as np
import jax
import jax.numpy as jnp
from jax import lax
from jax.experimental import pallas as pl
from jax.experimental.pallas import tpu as pltpu

F32 = jnp.float32
BF16 = jnp.bfloat16

NSA_HEADS = 8
NSA_KV_HEADS = 2
NSA_GROUP = NSA_HEADS // NSA_KV_HEADS
HEAD_DIM = 64
CMP_BLOCK = 32
CMP_STRIDE = 16
CMP_HIDDEN = 64
SEL_BLOCK = 64
N_SEL = 16
WINDOW = 512
Q_BLOCK = 128
GDN_HEADS = 8
GDN_DK = 64
GDN_DV = 64
CONV_W = 4
GDN_CHUNK = 64
PAGE_SIZE = 128
EPS = 1e-6

NSA_WIDTH = NSA_HEADS * HEAD_DIM
KV_ROW = 2 * NSA_KV_HEADS * HEAD_DIM
GDN_QK_WIDTH = GDN_HEADS * GDN_DK
GDN_WIDTH = GDN_HEADS * GDN_DV
CONV_DIM = 2 * GDN_QK_WIDTH + GDN_WIDTH
SMALL_W = 128
SMALL_A0 = 3 * NSA_HEADS
SMALL_B0 = SMALL_A0 + GDN_HEADS

NEG = -1e30
MASKV = -(2.0 ** 100)
AUG = 128
POS_SPLIT = 128
VMEM_LIMIT = 56 * 1024 * 1024

_NT = (((1,), (1,)), ((), ()))
_NN = (((1,), (0,)), ((), ()))


def _alibi_slopes():
    h = np.arange(1, NSA_HEADS + 1, dtype=np.float32)
    return [float(v) for v in np.power(np.float32(2.0), -np.float32(8.0) * h / np.float32(NSA_HEADS))]


_SLOPES = _alibi_slopes()


def _dot(a, b, dims=_NN):
    return lax.dot_general(a, b, dims, preferred_element_type=F32)


def _split(a):
    hi = a.astype(BF16)
    lo = (a - hi.astype(F32)).astype(BF16)
    return hi, lo


def _dot3s(a_split, b_split, dims=_NN):
    (ah, al), (bh, bl) = a_split, b_split
    return _dot(ah, bh, dims) + _dot(ah, bl, dims) + _dot(al, bh, dims)


def _silu(x):
    return x * jax.nn.sigmoid(x)


def _cparams(sem, flags=None):
    return pltpu.CompilerParams(dimension_semantics=sem, vmem_limit_bytes=VMEM_LIMIT, flags=flags)


def _ada_kernel(c_ref, w_ref, b_ref, o_ref):
    sc = _silu(c_ref[...]).astype(BF16)
    o_ref[...] = _dot(sc, w_ref[...]) + b_ref[...]


def _ada(c, w_bf, b):
    n, d = c.shape
    n_pad = -(-n // 8) * 8
    c = jnp.pad(c, ((0, n_pad - n), (0, 0)))
    out = pl.pallas_call(
        _ada_kernel,
        out_shape=jax.ShapeDtypeStruct((n_pad, w_bf.shape[1]), F32),
        name="ada",
    )(c, w_bf, b.reshape(1, -1))
    return out[:n]


_SEG = (("q", NSA_WIDTH), ("kvc", KV_ROW), ("kvs", KV_ROW), ("kvw", KV_ROW), ("za", NSA_WIDTH),
        ("qkvb", CONV_DIM), ("zb", GDN_WIDTH), ("gm", None), ("small", SMALL_W))


def _seg_offsets(d_model):
    offs, c = {}, 0
    for name, n in _SEG:
        n = 2 * d_model if n is None else n
        offs[name] = (c, n)
        c += n
    return offs, c


def _aug_cols(rows, c0, c1):
    lane = lax.broadcasted_iota(jnp.int32, (rows, AUG - HEAD_DIM), 1)
    return jnp.where(lane == 0, c0, jnp.where(lane == 1, c1, 0.0)).astype(F32)


def _aug_keys(kv, pos):
    hi = jnp.floor(pos * (1.0 / POS_SPLIT))
    cols = _aug_cols(kv.shape[0], hi, pos - hi * POS_SPLIT)
    parts = []
    for g in range(NSA_KV_HEADS):
        parts += [kv[:, g * HEAD_DIM:(g + 1) * HEAD_DIM], cols]
    k_aug = jnp.concatenate(parts, axis=1).astype(BF16)
    kv_t = kv.T
    return k_aug, kv_t[NSA_KV_HEADS * HEAD_DIM:].astype(BF16), kv_t


def _inproj_kernel(x_ref, scale_ref, shift_ref, ng_ref, w_ref,
                   q_ref, kvct_ref, kvst_ref, kvwt_ref, kvcb_ref, ksel_ref, vselt_ref, kwin_ref, vwint_ref,
                   sza_ref, qkvb_ref, szb_ref, gms_ref, small_ref, *, offs, seq):
    x = x_ref[...]
    tm = x.shape[0]
    y = x * lax.rsqrt(jnp.mean(x * x, axis=-1, keepdims=True) + EPS) * ng_ref[...]
    h = y * (1.0 + scale_ref[...]) + shift_ref[...]
    hb = h.astype(BF16)

    def seg(name):
        c0, n = offs[name]
        return _dot(hb, w_ref[:, c0:c0 + n])

    qv = seg("q") * (HEAD_DIM ** -0.5)
    parts = []
    for hh in range(NSA_HEADS):
        parts += [qv[:, hh * HEAD_DIM:(hh + 1) * HEAD_DIM], _aug_cols(tm, _SLOPES[hh] * POS_SPLIT, _SLOPES[hh])]
    q_ref[...] = jnp.concatenate(parts, axis=1).astype(BF16)

    pos = ((pl.program_id(0) * tm) % seq + lax.broadcasted_iota(jnp.int32, (tm, 1), 0)).astype(F32)
    v = seg("kvc")
    kvct_ref[...] = v.T
    kvcb_ref[...] = v.astype(BF16)
    for name, f_ref, k_ref, vt_ref in (("kvs", kvst_ref, ksel_ref, vselt_ref), ("kvw", kvwt_ref, kwin_ref, vwint_ref)):
        k_ref[...], vt_ref[...], f_ref[...] = _aug_keys(seg(name), pos)
    sza_ref[...] = _silu(seg("za"))
    qkvb_ref[...] = seg("qkvb")
    szb_ref[...] = _silu(seg("zb"))
    gms_ref[...] = jax.nn.sigmoid(seg("gm"))
    small_ref[...] = seg("small")


def _inproj(x2, scale_arr, shift_arr, mod_specs, ng, w_bf, offs, tm, seq):
    n, d = x2.shape
    wtot = w_bf.shape[1]
    nbatch = n // seq
    spt = seq // tm
    vw = NSA_KV_HEADS * HEAD_DIM
    row = lambda w: pl.BlockSpec((tm, w), lambda i: (i, 0))
    rows = lambda w, dt: (jax.ShapeDtypeStruct((n, w), dt), row(w))
    tr = lambda w, dt: (jax.ShapeDtypeStruct((nbatch, w, seq), dt),
                        pl.BlockSpec((None, w, tm), lambda i: (i // spt, 0, i % spt)))
    vt = tr(vw, BF16)
    kvt = tr(KV_ROW, F32)
    outs = [rows(NSA_HEADS * AUG, BF16), kvt, kvt, kvt, rows(KV_ROW, BF16),
            rows(NSA_KV_HEADS * AUG, BF16), vt, rows(NSA_KV_HEADS * AUG, BF16), vt,
            rows(offs["za"][1], F32), rows(offs["qkvb"][1], F32), rows(offs["zb"][1], F32), rows(offs["gm"][1], F32),
            rows(offs["small"][1], F32)]
    out_shape = [o[0] for o in outs]
    out_specs = [o[1] for o in outs]
    return pl.pallas_call(
        functools.partial(_inproj_kernel, offs=offs, seq=seq),
        grid=(n // tm,),
        in_specs=[row(d), mod_specs[0], mod_specs[1],
                  pl.BlockSpec((1, d), lambda i: (0, 0)),
                  pl.BlockSpec((d, wtot), lambda i: (0, 0))],
        out_specs=out_specs,
        out_shape=out_shape,
        compiler_params=_cparams(("parallel",)),
        name="inproj",
    )(x2, scale_arr, shift_arr, ng, w_bf)


def _outproj_kernel(x_ref, oa_ref, ob_ref, gms_ref, gate_ref, wa_ref, wb_ref, wo_ref, fg_ref, y_ref, *, d):
    ma = _dot(oa_ref[...].astype(BF16), wa_ref[...])
    mb = _dot(ob_ref[...].astype(BF16), wb_ref[...])
    m = gms_ref[:, 0:d] * ma + gms_ref[:, d:2 * d] * mb
    y = x_ref[...] + gate_ref[...] * _dot(m.astype(BF16), wo_ref[...])
    y_ref[...] = y * lax.rsqrt(jnp.mean(y * y, axis=-1, keepdims=True) + EPS) * fg_ref[...]


def _outproj(x2, oa, ob, gms, gate_arr, gate_spec, wa, wb, wo, fg, tm):
    n, d = x2.shape
    row = lambda w: pl.BlockSpec((tm, w), lambda i: (i, 0))
    full = lambda a: pl.BlockSpec(a.shape, lambda i: (0, 0))
    return pl.pallas_call(
        functools.partial(_outproj_kernel, d=d),
        grid=(n // tm,),
        in_specs=[row(d), row(oa.shape[1]), row(ob.shape[1]), row(2 * d), gate_spec,
                  full(wa), full(wb), full(wo), full(fg)],
        out_specs=row(d),
        out_shape=jax.ShapeDtypeStruct((n, d), F32),
        compiler_params=_cparams(("parallel",)),
        name="outproj",
    )(x2, oa, ob, gms, gate_arr, wa, wb, wo, fg)


def _compress_math(x, w1_ref, pe_ref, w2_ref):
    n16 = x.shape[0]
    a0 = _dot(x, w1_ref[0])
    a1 = _dot(x, w1_ref[1])
    p0 = _dot(pe_ref[0], w1_ref[0]) + _dot(pe_ref[1], w1_ref[1])
    pre = a0 + pltpu.roll(a1, n16 - 1, 0) + p0[0:1]
    return _dot(_silu(pre).astype(BF16), w2_ref[...])


def _compress_kernel(x_ref, w1_ref, pe_ref, w2_ref, k_ref, vt_ref):
    kv = _compress_math(x_ref[...], w1_ref, pe_ref, w2_ref)
    n16 = kv.shape[0]
    end = (lax.broadcasted_iota(jnp.int32, (n16, 1), 0) * CMP_STRIDE + (CMP_BLOCK - 1)).astype(F32)
    k_ref[...], vt_ref[...], _ = _aug_keys(kv, end)


def _compress_prompt(kvcb, w1big, pebig, w2big, batch):
    n = kvcb.shape[0]
    n16 = n // batch // CMP_STRIDE
    x = kvcb.reshape(batch, n16, CMP_STRIDE * KV_ROW)
    vw = NSA_KV_HEADS * HEAD_DIM
    return pl.pallas_call(
        _compress_kernel,
        grid=(batch,),
        in_specs=[pl.BlockSpec((None, n16, CMP_STRIDE * KV_ROW), lambda b: (b, 0, 0)),
                  pl.BlockSpec(w1big.shape, lambda b: (0, 0, 0)),
                  pl.BlockSpec(pebig.shape, lambda b: (0, 0, 0)),
                  pl.BlockSpec(w2big.shape, lambda b: (0, 0))],
        out_specs=[pl.BlockSpec((None, n16, NSA_KV_HEADS * AUG), lambda b: (b, 0, 0)),
                   pl.BlockSpec((None, vw, n16), lambda b: (b, 0, 0))],
        out_shape=[jax.ShapeDtypeStruct((batch, n16, NSA_KV_HEADS * AUG), BF16),
                   jax.ShapeDtypeStruct((batch, vw, n16), BF16)],
        compiler_params=_cparams(("parallel",)),
        name="compress_prompt",
    )(x, w1big, pebig, w2big)


def _compress_weights(pe_k, w1_k, w2_k, pe_v, w1_v, w2_v):
    r_cnt = CMP_BLOCK // CMP_STRIDE
    g = NSA_KV_HEADS
    eye2 = jnp.eye(2, dtype=F32)
    eyeg = jnp.eye(g, dtype=F32)
    w1 = jnp.stack([w1_k, w1_v]).reshape(2, r_cnt, CMP_STRIDE, HEAD_DIM, CMP_HIDDEN)
    w1big = jnp.einsum("krsde,kK,gG->rskgdKGe", w1, eye2, eyeg)
    w1big = w1big.reshape(r_cnt, CMP_STRIDE * KV_ROW, 2 * g * CMP_HIDDEN).astype(BF16)
    pe = jnp.stack([pe_k, pe_v]).reshape(2, r_cnt, CMP_STRIDE, HEAD_DIM)
    pebig = jnp.broadcast_to(pe.transpose(1, 2, 0, 3)[:, :, :, None, :], (r_cnt, CMP_STRIDE, 2, g, HEAD_DIM))
    pebig = pebig.reshape(r_cnt, 1, CMP_STRIDE * KV_ROW)
    pebig = jnp.pad(pebig, ((0, 0), (0, 7), (0, 0))).astype(BF16)
    w2 = jnp.stack([w2_k, w2_v])
    w2big = jnp.einsum("ked,kK,gG->kgeKGd", w2, eye2, eyeg).reshape(2 * g * CMP_HIDDEN, KV_ROW).astype(BF16)
    return w1big, pebig, w2big


def _sel_map(n_cmp_rows, n_blk, n_blk_pad):
    i = np.arange(n_cmp_rows)[:, None] * CMP_STRIDE
    j = np.arange(n_blk_pad)[None, :] * SEL_BLOCK
    ov = np.minimum(i + CMP_BLOCK, j + SEL_BLOCK) - np.maximum(i, j)
    m = np.clip(ov, 0, None).astype(np.float32) / np.float32(CMP_BLOCK)
    m[:, n_blk:] = 0.0
    return jnp.asarray(m, dtype=BF16)


def _softmax_block(s, mask):
    s = jnp.where(mask, s, NEG)
    m = jnp.max(s, axis=-1, keepdims=True)
    p = jnp.where(mask, jnp.exp(s - m), 0.0)
    return m, p


def _online_update(carry, s, mask, v_bf, v_dims=_NN):
    m, l, acc = carry
    s = jnp.where(mask, s, NEG)
    m_new = jnp.maximum(m, jnp.max(s, axis=-1, keepdims=True))
    alpha = jnp.exp(m - m_new)
    p = jnp.where(mask, jnp.exp(s - m_new), 0.0)
    l = alpha * l + jnp.sum(p, axis=-1, keepdims=True)
    acc = alpha * acc + _dot(p.astype(BF16), v_bf, v_dims)
    return m_new, l, acc


def _flash_step(carry, s, v_bf, v_dims=_NN):
    m, l, acc = carry
    m_new = jnp.maximum(m, jnp.max(s, axis=-1, keepdims=True))
    alpha = jnp.exp(m - m_new)
    p = jnp.exp(s - m_new)
    return (m_new, alpha * l + jnp.sum(p, axis=-1, keepdims=True),
            alpha * acc + _dot(p.astype(BF16), v_bf, v_dims))


def _online_init(rows, hd):
    return (jnp.full((rows, 1), NEG, F32), jnp.zeros((rows, 1), F32), jnp.zeros((rows, hd), F32))


def _importance(p_sum, selmap):
    ph, pl_ = _split(p_sum)
    pl2 = (p_sum - ph.astype(F32) - pl_.astype(F32)).astype(BF16)
    return _dot(ph, selmap) + _dot(pl_, selmap) + _dot(pl2, selmap)


def _flash_step_t(carry, s_t, v_t):
    m, l, acc = carry
    m_new = jnp.maximum(m, jnp.max(s_t, axis=0, keepdims=True))
    alpha = jnp.exp(m - m_new)
    p = jnp.exp(s_t - m_new)
    l = alpha * l + jnp.sum(p, axis=0, keepdims=True)
    acc = alpha * acc + _dot(v_t, p.astype(BF16))
    return m_new, l, acc


def _flash_init_t(cols, dv):
    return (jnp.full((1, cols), NEG, F32), jnp.zeros((1, cols), F32), jnp.zeros((dv, cols), F32))


def _select_blocks_t(imp_t, tq):
    nb, nq = imp_t.shape
    jj = lax.broadcasted_iota(jnp.int32, (nb, nq), 0)
    cur = tq // SEL_BLOCK
    forced = (jj == 0) | (jj == cur) | (jj == cur - 1)
    jf = jj.astype(F32)
    score = jnp.where(forced, NEG, jnp.where(jj <= cur, imp_t, NEG))
    selm = jnp.where(forced, 1.0, 0.0).astype(F32)
    for _ in range(N_SEL - 3):
        m = jnp.max(score, axis=0, keepdims=True)
        idx = jnp.min(jnp.where(score == m, jf, float(nb)), axis=0, keepdims=True)
        hit = jnp.where(m > 0.5 * NEG, idx, -1.0) == jf
        selm = jnp.where(hit, 1.0, selm)
        score = jnp.where(hit, NEG, score)
    return selm


def _nsa_prompt_kernel(q_ref, kc_ref, vct_ref, ksel_ref, vselt_ref, kwin_ref, vwint_ref, onehot_ref, small_ref, sza_ref,
                       selmapt_ref, o_ref, flags_ref, m_ref, l_ref, acc_ref, *, kt):
    qb = Q_BLOCK
    hd = HEAD_DIM
    q0 = pl.program_id(1) * qb
    n_cmp = kc_ref.shape[0]
    cols = NSA_GROUP * qb
    tq = q0 + lax.broadcasted_iota(jnp.int32, (1, qb), 1)
    gates_t = jax.nn.sigmoid(small_ref[...]).T
    wlen = WINDOW + qb
    w0 = pl.multiple_of(jnp.maximum(q0 - WINDOW, 0), qb)
    tile4 = lambda a: jnp.concatenate([a] * NSA_GROUP, axis=1)

    def keymask(pos_col, lo):
        dist = tq - pos_col
        ok = (dist >= 0) if lo is None else ((dist >= 0) & (dist < lo))
        return tile4(jnp.where(ok, 0.0, MASKV).astype(F32))

    cmp_bias = keymask(lax.broadcasted_iota(jnp.int32, (n_cmp, 1), 0) * CMP_STRIDE + (CMP_BLOCK - 1), None)
    win_bias = keymask(w0 + lax.broadcasted_iota(jnp.int32, (wlen, 1), 0), WINDOW)
    jlast = q0 // kt
    klast = pl.multiple_of(jlast * kt, kt)
    diag_bias = keymask(klast + lax.broadcasted_iota(jnp.int32, (kt, 1), 0), None)

    groups = range(NSA_KV_HEADS)
    kcols = [slice(g * AUG, (g + 1) * AUG) for g in groups]
    vrows = [slice(g * hd, (g + 1) * hd) for g in groups]
    q_heads = [[q_ref[:, (g * NSA_GROUP + h) * AUG:(g * NSA_GROUP + h + 1) * AUG] for h in range(NSA_GROUP)]
               for g in groups]
    q_cat = [jnp.concatenate(q_heads[g], axis=0) for g in groups]
    finish = lambda carry: carry[2] * (1.0 / jnp.maximum(carry[1], 1e-30))

    o_w = [finish(_flash_step_t(_flash_init_t(cols, hd), _dot(kwin_ref[pl.ds(w0, wlen), kcols[g]], q_cat[g], _NT) + win_bias,
                                vwint_ref[vrows[g], pl.ds(w0, wlen)])) for g in groups]

    o_c, q_full, psums = [], [], []
    sm_t = selmapt_ref[...]
    for g in groups:
        s_t = _dot(kc_ref[:, kcols[g]], q_cat[g], _NT) + cmp_bias
        m = jnp.maximum(jnp.max(s_t, axis=0, keepdims=True), NEG)
        p = jnp.exp(s_t - m)
        p = p * (1.0 / jnp.maximum(jnp.sum(p, axis=0, keepdims=True), 1e-30))
        o_c.append(_dot(vct_ref[vrows[g], :], p.astype(BF16)))
        psum = p[:, 0:qb]
        for h in range(1, NSA_GROUP):
            psum = psum + p[:, h * qb:(h + 1) * qb]
        psums.append(psum)
    psum = jnp.concatenate(psums, axis=1)
    ph, pl_ = _split(psum)
    pl2 = (psum - ph.astype(F32) - pl_.astype(F32)).astype(BF16)
    imp_t = _dot(sm_t, ph) + _dot(sm_t, pl_) + _dot(sm_t, pl2)
    selm_t = _select_blocks_t(imp_t, jnp.concatenate([tq] * NSA_KV_HEADS, axis=1))
    for g in groups:
        selm = selm_t[:, g * qb:(g + 1) * qb].T
        blk_bias = jnp.where(selm > 0.5, 0.0, MASKV).astype(BF16)
        q_full.append(jnp.concatenate([jnp.concatenate([qh, blk_bias], axis=1) for qh in q_heads[g]], axis=0))
        blk_any = jnp.max(selm, axis=0, keepdims=True)
        bpt = kt // SEL_BLOCK
        for j in range(flags_ref.shape[1]):
            flags_ref[g, j] = (jnp.max(blk_any[:, j * bpt:(j + 1) * bpt]) > 0.5).astype(jnp.int32)
        for ref, val in zip((m_ref, l_ref, acc_ref), _flash_init_t(cols, hd)):
            ref[g] = val

    def sel_step(k0, g, bias):
        k_op = jnp.concatenate([ksel_ref[pl.ds(k0, kt), kcols[g]], onehot_ref[pl.ds(k0, kt), :]], axis=1)
        s_t = _dot(k_op, q_full[g], _NT)
        s_t = s_t if bias is None else s_t + bias
        m, l, acc = _flash_step_t((m_ref[g], l_ref[g], acc_ref[g]), s_t, vselt_ref[vrows[g], pl.ds(k0, kt)])
        m_ref[g], l_ref[g], acc_ref[g] = m, l, acc

    def sel_tile(j, c):
        for g in groups:
            @pl.when(flags_ref[g, j] > 0)
            def _():
                sel_step(pl.multiple_of(j * kt, kt), g, None)
        return c

    lax.fori_loop(0, jlast, sel_tile, 0)
    o_s = []
    for g in groups:
        sel_step(klast, g, diag_bias)
        o_s.append(finish((m_ref[g], l_ref[g], acc_ref[g])))

    outs = []
    for g in groups:
        for h in range(NSA_GROUP):
            c = 3 * (g * NSA_GROUP + h)
            r = slice(h * qb, (h + 1) * qb)
            outs.append(gates_t[c:c + 1, :] * o_c[g][:, r] + gates_t[c + 1:c + 2, :] * o_s[g][:, r]
                        + gates_t[c + 2:c + 3, :] * o_w[g][:, r])
    o_ref[...] = jnp.concatenate(outs, axis=0).T * sza_ref[...]


def _nsa_prompt(q_aug, kc_aug, vc_t, ksel, vsel_t, kwin, vwin_t, small, sza, batch):
    n = q_aug.shape[0]
    seq = n // batch
    nq = seq // Q_BLOCK
    n_cmp = kc_aug.shape[1]
    n_blk = -(-seq // SEL_BLOCK)
    assert n_blk <= AUG and seq % Q_BLOCK == 0 and seq >= WINDOW + Q_BLOCK
    kt = min(512, seq)
    selmap_t = _sel_map(n_cmp, n_blk, AUG).T
    pos = np.arange(seq)[:, None] // SEL_BLOCK
    onehot = jnp.asarray(pos == np.arange(AUG)[None, :], dtype=BF16)
    vw = NSA_KV_HEADS * HEAD_DIM
    kw = NSA_KV_HEADS * AUG
    row = lambda w: pl.BlockSpec((Q_BLOCK, w), lambda b, i: (b * nq + i, 0))
    kspec = pl.BlockSpec((None, seq, kw), lambda b, i: (b, 0, 0))
    vspec = pl.BlockSpec((None, vw, seq), lambda b, i: (b, 0, 0))
    return pl.pallas_call(
        functools.partial(_nsa_prompt_kernel, kt=kt),
        grid=(batch, nq),
        in_specs=[row(NSA_HEADS * AUG),
                  pl.BlockSpec((None, n_cmp, kw), lambda b, i: (b, 0, 0)),
                  pl.BlockSpec((None, vw, n_cmp), lambda b, i: (b, 0, 0)),
                  kspec, vspec, kspec, vspec,
                  pl.BlockSpec(onehot.shape, lambda b, i: (0, 0)),
                  row(SMALL_W), row(NSA_WIDTH),
                  pl.BlockSpec(selmap_t.shape, lambda b, i: (0, 0))],
        out_specs=row(NSA_WIDTH),
        out_shape=jax.ShapeDtypeStruct((n, NSA_WIDTH), F32),
        scratch_shapes=[pltpu.SMEM((NSA_KV_HEADS, seq // kt), jnp.int32),
                        pltpu.VMEM((NSA_KV_HEADS, 1, NSA_GROUP * Q_BLOCK), F32),
                        pltpu.VMEM((NSA_KV_HEADS, 1, NSA_GROUP * Q_BLOCK), F32),
                        pltpu.VMEM((NSA_KV_HEADS, HEAD_DIM, NSA_GROUP * Q_BLOCK), F32)],
        compiler_params=_cparams(("parallel", "arbitrary")),
        name="nsa_prompt",
    )(q_aug, kc_aug, vc_t, ksel.reshape(batch, seq, kw), vsel_t, kwin.reshape(batch, seq, kw), vwin_t, onehot, small, sza,
      selmap_t)


def _softplus(x):
    return jnp.maximum(x, 0.0) + jnp.log1p(jnp.exp(-jnp.abs(x)))


def _gdn_prompt_kernel(qkv_ref, small_ref, szb_ref, cw_ref, cb_ref, alog_ref, dtb_ref, gn_ref,
                       o_ref, sfin_ref, s_ref, prev_ref):
    c = pl.program_id(1)
    ck = GDN_CHUNK
    dk = GDN_DK

    @pl.when(c == 0)
    def _():
        s_ref[...] = jnp.zeros_like(s_ref)
        prev_ref[...] = jnp.zeros_like(prev_ref)

    u = qkv_ref[...]
    ext = jnp.concatenate([prev_ref[...], u], axis=0)
    y = cb_ref[...]
    for i in range(CONV_W):
        y = y + ext[8 - (CONV_W - 1) + i:8 - (CONV_W - 1) + i + ck] * cw_ref[i:i + 1, :]
    prev_ref[...] = u[ck - 8:ck]
    act = _silu(y)

    small = small_ref[...]
    g_all = -jnp.exp(alog_ref[...]) * _softplus(small + dtb_ref[...])
    beta_all = jax.nn.sigmoid(small)
    ii = lax.broadcasted_iota(jnp.int32, (ck, ck), 0)
    jj = lax.broadcasted_iota(jnp.int32, (ck, ck), 1)
    tri = jnp.where(ii >= jj, 1.0, 0.0).astype(F32)
    gc_all = lax.dot_general(tri, g_all, _NN, precision=lax.Precision.HIGHEST, preferred_element_type=F32)
    gc_t = gc_all.T

    pairs = range(GDN_HEADS // 2)
    pw2 = 2 * dk
    lane = lax.broadcasted_iota(jnp.int32, (ck, pw2), 1)
    left = lane < dk
    row2 = lax.broadcasted_iota(jnp.int32, (ck, pw2), 0)
    col2 = jnp.where(left, lane, lane - dk)
    eye2 = jnp.where(row2 == col2, 1.0, 0.0).astype(F32)

    def halves(col_l, col_r):
        return jnp.where(left, col_l, col_r)

    def head_norm(x):
        sq = x * x
        return halves(jnp.sum(jnp.where(left, sq, 0.0), axis=-1, keepdims=True),
                      jnp.sum(jnp.where(left, 0.0, sq), axis=-1, keepdims=True))

    def bd(b):
        return _split(jnp.concatenate([jnp.where(left, b, 0.0), jnp.where(left, 0.0, b)], axis=0))

    q, k, v, beta, gc, decay, g_last = [], [], [], [], [], [], []
    for p in pairs:
        qp = act[:, p * pw2:(p + 1) * pw2]
        kp = act[:, GDN_QK_WIDTH + p * pw2:GDN_QK_WIDTH + (p + 1) * pw2]
        q.append(qp * lax.rsqrt(head_norm(qp) + EPS) * (dk ** -0.5))
        k.append(kp * lax.rsqrt(head_norm(kp) + EPS))
        v.append(act[:, 2 * GDN_QK_WIDTH + p * pw2:2 * GDN_QK_WIDTH + (p + 1) * pw2])
        h0, h1 = 2 * p, 2 * p + 1
        beta.append(halves(beta_all[:, SMALL_B0 + h0:SMALL_B0 + h0 + 1], beta_all[:, SMALL_B0 + h1:SMALL_B0 + h1 + 1]))
        gcp = halves(gc_all[:, SMALL_A0 + h0:SMALL_A0 + h0 + 1], gc_all[:, SMALL_A0 + h1:SMALL_A0 + h1 + 1])
        grp = jnp.concatenate([gc_t[SMALL_A0 + h0:SMALL_A0 + h0 + 1, :], gc_t[SMALL_A0 + h1:SMALL_A0 + h1 + 1, :]], axis=1)
        gc.append(gcp)
        g_last.append(gcp[ck - 1:ck, :])
        decay.append(jnp.exp(jnp.where(row2 >= col2, gcp - grp, NEG)))
    kb = [k[p] * beta[p] for p in pairs]
    kbd = [bd(k[p]) for p in pairs]
    qk = [_dot3s(_split(jnp.concatenate([q[p], kb[p]], axis=0)), kbd[p], _NT) for p in pairs]
    a_in = [qk[p][0:ck] * decay[p] for p in pairs]
    lmat = [jnp.where(row2 > col2, qk[p][ck:2 * ck] * decay[p], 0.0) for p in pairs]
    x = [eye2 - lmat[p] for p in pairs]
    pw = lmat
    for it in range(int(math.log2(ck)) - 1):
        if it < 3:
            pw = [_dot3s(_split(pw[p]), bd(pw[p])) for p in pairs]
            x = [x[p] + _dot3s(_split(x[p]), bd(pw[p])) for p in pairs]
        else:
            pw = [_dot(pw[p].astype(BF16), bd(pw[p])[0]) for p in pairs]
            x = [x[p] + _dot(x[p].astype(BF16), bd(pw[p])[0]) for p in pairs]
    eg = [jnp.exp(gc[p]) for p in pairs]
    xs = [_split(x[p]) for p in pairs]
    uu = [_dot3s(xs[p], bd(v[p] * beta[p])) for p in pairs]
    ww = [_dot3s(xs[p], bd(kb[p] * eg[p])) for p in pairs]
    s_old = [s_ref[p] for p in pairs]
    qw_s = [_dot3s(_split(jnp.concatenate([q[p] * eg[p], ww[p]], axis=0)), bd(s_old[p])) for p in pairs]
    v_new = [uu[p] - qw_s[p][ck:2 * ck] for p in pairs]
    o = [qw_s[p][0:ck] + _dot(a_in[p].astype(BF16), bd(v_new[p])[0]) for p in pairs]
    gn2 = jnp.concatenate([gn_ref[...]] * 2, axis=1)
    for p in pairs:
        kd = k[p] * jnp.exp(g_last[p] - gc[p])
        r = _dot3s(_split(kd.T), _split(v_new[p]))
        s_ref[p] = s_old[p] * jnp.exp(g_last[p]) + jnp.where(left, r[0:dk], r[dk:pw2])
        on = o[p] * lax.rsqrt(head_norm(o[p]) * (1.0 / GDN_DV) + EPS) * gn2
        o_ref[:, p * pw2:(p + 1) * pw2] = on * szb_ref[:, p * pw2:(p + 1) * pw2]

    @pl.when(c == pl.num_programs(1) - 1)
    def _():
        for p in pairs:
            sp = s_ref[p]
            sfin_ref[2 * p] = sp[:, 0:GDN_DV]
            sfin_ref[2 * p + 1] = sp[:, GDN_DV:2 * GDN_DV]


def _lane_params(a_log, dt_bias):
    alog_l = jnp.zeros((1, SMALL_W), F32).at[0, SMALL_A0:SMALL_A0 + GDN_HEADS].set(a_log)
    dtb_l = jnp.zeros((1, SMALL_W), F32).at[0, SMALL_A0:SMALL_A0 + GDN_HEADS].set(dt_bias)
    return alog_l, dtb_l


def _gdn_prompt(qkvb, small, szb, conv_w, conv_b, a_log, dt_bias, gnorm, batch):
    n = qkvb.shape[0]
    seq = n // batch
    nc = seq // GDN_CHUNK
    alog_l, dtb_l = _lane_params(a_log, dt_bias)
    row = lambda w: pl.BlockSpec((GDN_CHUNK, w), lambda b, c: (b * nc + c, 0))
    full = lambda a: pl.BlockSpec(a.shape, lambda b, c: (0,) * a.ndim)
    cb = conv_b.reshape(1, -1)
    gn = gnorm.reshape(1, -1)
    return pl.pallas_call(
        _gdn_prompt_kernel,
        grid=(batch, nc),
        in_specs=[row(CONV_DIM), row(SMALL_W), row(GDN_WIDTH), full(conv_w), full(cb), full(alog_l), full(dtb_l), full(gn)],
        out_specs=[row(GDN_WIDTH),
                   pl.BlockSpec((None, GDN_HEADS, GDN_DK, GDN_DV), lambda b, c: (b, 0, 0, 0))],
        out_shape=[jax.ShapeDtypeStruct((n, GDN_WIDTH), F32),
                   jax.ShapeDtypeStruct((batch, GDN_HEADS, GDN_DK, GDN_DV), F32)],
        scratch_shapes=[pltpu.VMEM((GDN_HEADS // 2, GDN_DK, 2 * GDN_DV), F32), pltpu.VMEM((8, CONV_DIM), F32)],
        compiler_params=_cparams(("parallel", "arbitrary")),
        name="gdn_prompt",
    )(qkvb, small, szb, conv_w, cb, alog_l, dtb_l, gn)


def _prep_w_in(w_in, d_model):
    sizes = (NSA_WIDTH, 6 * NSA_KV_HEADS * HEAD_DIM, 3 * NSA_HEADS, NSA_WIDTH, CONV_DIM, GDN_HEADS, GDN_HEADS,
             GDN_WIDTH, 2 * d_model)
    pts = np.cumsum(np.array(sizes))[:-1].tolist()
    q_a, kv_a, g_a, z_a, qkv_b, a_b, b_b, z_b, gm = jnp.split(w_in, pts, axis=1)
    small = jnp.concatenate([g_a, a_b, b_b], axis=1)
    small = jnp.pad(small, ((0, 0), (0, SMALL_W - small.shape[1])))
    return jnp.concatenate([q_a, kv_a, z_a, qkv_b, z_b, gm, small], axis=1).astype(BF16)


def _kv_leaf(kv_t):
    b, _, t = kv_t.shape
    return kv_t.reshape(1, b, 2, NSA_KV_HEADS, HEAD_DIM, t).transpose(0, 1, 5, 2, 3, 4)


def _kv_rows_t(cache):
    n, r = cache.shape[:2]
    return cache.transpose(0, 2, 3, 4, 1).reshape(n, KV_ROW, r)


def _prompt_path(x, ada, lw, cw):
    (norm_g, w_bf, offs, conv_w, conv_b, a_log, dt_bias, gnorm, wa, wb, wo, final_g) = lw
    batch, seq, d = x.shape
    n = batch * seq
    x2 = x.reshape(n, d)
    ada3 = ada.reshape(batch, 1, 3 * d)
    tm = 256
    mod = lambda k: pl.BlockSpec((None, 1, d), lambda i: (i * tm // seq, 0, k))
    (q_aug, kvc_t, kvs_t, kvw_t, kvcb, ksel, vsel_t, kwin, vwin_t, sza, qkvb, szb, gms, small) = _inproj(
        x2, ada3, ada3, (mod(1), mod(0)), norm_g.reshape(1, d), w_bf, offs, tm, seq)
    kc_aug, vc_t = _compress_prompt(kvcb, *cw, batch)
    o_a = _nsa_prompt(q_aug, kc_aug, vc_t, ksel, vsel_t, kwin, vwin_t, small, sza, batch)
    o_b, s_new = _gdn_prompt(qkvb, small, szb, conv_w, conv_b, a_log, dt_bias, gnorm, batch)
    tmo = 512
    gate_spec = pl.BlockSpec((None, 1, d), lambda i: (i * tmo // seq, 0, 2))
    y = _outproj(x2, o_a, o_b, gms, ada3, gate_spec, wa, wb, wo, final_g.reshape(1, d), tmo)
    keep = min(WINDOW, seq)
    new_conv = qkvb.reshape(batch, seq, CONV_DIM)[None, :, seq - (CONV_W - 1):]
    return (y.reshape(batch, seq, d), _kv_leaf(kvc_t), _kv_leaf(kvs_t), _kv_leaf(kvw_t[:, :, seq - keep:]), new_conv,
            s_new[None])


def _page_fetch(pt_ref, cache_hbm, buf, sem, npages):
    def copy(seq, slot, j):
        c0 = pl.multiple_of(j * PAGE_SIZE, PAGE_SIZE)
        return pltpu.make_async_copy(cache_hbm.at[pt_ref[seq, j]], buf.at[slot, :, pl.ds(c0, PAGE_SIZE)], sem.at[slot])

    def start(seq, slot):
        for j in range(npages):
            copy(seq, slot, j).start()

    def wait(seq, slot):
        for j in range(npages):
            copy(seq, slot, j).wait()

    return start, wait


def _fetch_this_prefetch_next(start, wait):
    b = pl.program_id(0)
    nb = pl.num_programs(0)

    @pl.when(b == 0)
    def _():
        start(0, 0)

    @pl.when(b + 1 < nb)
    def _():
        start(b + 1, (b + 1) % 2)

    slot = b % 2
    wait(b, slot)
    return slot


def _compress_sample_kernel(pt_ref, cache_hbm, perm_ref, w1_ref, pe_ref, w2_ref, o_ref, buf, sem, xc_ref, *, npages):
    start, wait = _page_fetch(pt_ref, cache_hbm, buf, sem, npages)
    slot = _fetch_this_prefetch_next(start, wait)
    past = npages * PAGE_SIZE
    n16 = past // CMP_STRIDE
    gt = perm_ref.shape[0]
    cps = gt // CMP_STRIDE

    half = KV_ROW // 2
    kw = CMP_STRIDE * half

    def regroup(j, c):
        t0 = pl.multiple_of(j * gt, gt)
        x_t = buf[slot, :, pl.ds(t0, gt)].astype(BF16)
        y = _dot(perm_ref[...], x_t, _NT).astype(BF16)
        r0 = pl.multiple_of(j * cps, cps)
        for s in range(CMP_STRIDE):
            for kv in range(2):
                xc_ref[pl.ds(r0, cps), kv * kw + s * half:kv * kw + (s + 1) * half] = (
                    y[s * cps:(s + 1) * cps, kv * half:(kv + 1) * half])
        return c

    lax.fori_loop(0, past // gt, regroup, 0, unroll=min(8, past // gt))
    outs = []
    for kv in range(2):
        a = _dot(xc_ref[:, kv * kw:(kv + 1) * kw], w1_ref[kv])
        p0 = _dot(pe_ref[kv, 0], w1_ref[kv, :, 0:half]) + _dot(pe_ref[kv, 1], w1_ref[kv, :, half:2 * half])
        pre = a[:, 0:half] + pltpu.roll(a[:, half:2 * half], n16 - 1, 0) + p0[0:1]
        outs.append(_dot(_silu(pre).astype(BF16), w2_ref[kv]))
    o_ref[...] = jnp.concatenate(outs, axis=1).astype(BF16)


def _compress_weights_kv(pe_k, w1_k, w2_k, pe_v, w1_v, w2_v):
    r_cnt = CMP_BLOCK // CMP_STRIDE
    g = NSA_KV_HEADS
    eyeg = jnp.eye(g, dtype=F32)
    w1 = jnp.stack([w1_k, w1_v]).reshape(2, r_cnt, CMP_STRIDE, HEAD_DIM, CMP_HIDDEN)
    w1s = jnp.einsum("krsde,gG->ksgdrGe", w1, eyeg)
    w1s = w1s.reshape(2, CMP_STRIDE * g * HEAD_DIM, r_cnt * g * CMP_HIDDEN).astype(BF16)
    pe = jnp.stack([pe_k, pe_v]).reshape(2, r_cnt, CMP_STRIDE, 1, HEAD_DIM)
    pes = jnp.broadcast_to(pe, (2, r_cnt, CMP_STRIDE, g, HEAD_DIM)).reshape(2, r_cnt, 1, CMP_STRIDE * g * HEAD_DIM)
    pes = jnp.pad(pes, ((0, 0), (0, 0), (0, 7), (0, 0))).astype(BF16)
    w2s = jnp.einsum("ked,gG->kgeGd", jnp.stack([w2_k, w2_v]), eyeg).reshape(2, g * CMP_HIDDEN, g * HEAD_DIM).astype(BF16)
    return w1s, pes, w2s


def _compress_sample(cache_t, page_table, w1big, pebig, w2big):
    nseq, npages = page_table.shape
    past = npages * PAGE_SIZE
    n16 = past // CMP_STRIDE
    gt = 2 * PAGE_SIZE
    assert past % gt == 0 and gt // CMP_STRIDE == 16
    row = np.arange(gt)
    perm_t = jnp.asarray((row[:, None] % (gt // CMP_STRIDE)) * CMP_STRIDE + row[:, None] // (gt // CMP_STRIDE)
                         == row[None, :], dtype=BF16)
    grid_spec = pltpu.PrefetchScalarGridSpec(
        num_scalar_prefetch=1,
        grid=(nseq,),
        in_specs=[pl.BlockSpec(memory_space=pl.ANY),
                  pl.BlockSpec(perm_t.shape, lambda b, pt: (0, 0)),
                  pl.BlockSpec(w1big.shape, lambda b, pt: (0, 0, 0)),
                  pl.BlockSpec(pebig.shape, lambda b, pt: (0, 0, 0, 0)),
                  pl.BlockSpec(w2big.shape, lambda b, pt: (0, 0, 0))],
        out_specs=pl.BlockSpec((None, n16, KV_ROW), lambda b, pt: (b, 0, 0)),
        scratch_shapes=[pltpu.VMEM((2, KV_ROW, past), F32), pltpu.SemaphoreType.DMA((2,)),
                        pltpu.VMEM((n16, CMP_STRIDE * KV_ROW), BF16)],
    )
    return pl.pallas_call(
        functools.partial(_compress_sample_kernel, npages=npages),
        grid_spec=grid_spec,
        out_shape=jax.ShapeDtypeStruct((nseq, n16, KV_ROW), BF16),
        compiler_params=_cparams(("arbitrary",)),
        name="compress_sample",
    )(page_table, cache_t, perm_t, w1big, pebig, w2big)


def _nsa_sample_kernel(pt_ref, q_ref, kc_ref, cache_hbm, win_ref, nsel_ref, nwin_ref, nwint_ref, small_ref, sza_ref,
                       selmap_ref, onehot_ref, o_ref, wout_ref, buf, sem, *, npages, tn, kt):
    hd = HEAD_DIM
    rt = 8
    start, wait = _page_fetch(pt_ref, cache_hbm, buf, sem, npages)
    slot = _fetch_this_prefetch_next(start, wait)
    past = npages * PAGE_SIZE
    nbuf = win_ref.shape[1]
    n_cmp = kc_ref.shape[0]
    n_blk_lanes = selmap_ref.shape[1]
    q8 = q_ref[...].astype(F32)
    gates = jax.nn.sigmoid(small_ref[...])
    tau = lax.broadcasted_iota(jnp.int32, (rt, 1), 0) % tn
    t_col = past + tau
    t4 = jnp.concatenate([t_col] * NSA_GROUP, axis=0)
    rows = NSA_GROUP * rt
    newcol = lax.broadcasted_iota(jnp.int32, (1, SMALL_W), 1)
    new_dist = t4 - (past + newcol)
    new_ok = (newcol < tn) & (new_dist >= 0)
    zpad = jnp.zeros((SMALL_W - rt, hd), F32)
    groups = range(NSA_KV_HEADS)
    kcols = [slice(g * hd, (g + 1) * hd) for g in groups]
    vcols = [slice((NSA_KV_HEADS + g) * hd, (NSA_KV_HEADS + g + 1) * hd) for g in groups]

    def new_tile(ref, col):
        return jnp.concatenate([ref[:, col], zpad], axis=0).astype(BF16)

    def pos_cols(p0, n):
        pos = (p0 + lax.broadcasted_iota(jnp.int32, (n, 1), 0)).astype(F32)
        hi = jnp.floor(pos * (1.0 / POS_SPLIT))
        return _aug_cols(n, hi, pos - hi * POS_SPLIT)

    qgs, slopes, o_cs, imps = [], [], [], []
    for g in groups:
        kcol, vcol = kcols[g], vcols[g]
        qg = jnp.concatenate([q8[:, (g * NSA_GROUP + h) * AUG:(g * NSA_GROUP + h) * AUG + hd]
                              for h in range(NSA_GROUP)], axis=0).astype(BF16)
        slope = jnp.concatenate([jnp.full((rt, 1), _SLOPES[g * NSA_GROUP + h], F32) for h in range(NSA_GROUP)], axis=0)

        s = _dot(qg, kc_ref[:, kcol], _NT)
        end = lax.broadcasted_iota(jnp.int32, (1, n_cmp), 1) * CMP_STRIDE + (CMP_BLOCK - 1)
        dist = t4 - end
        mask = dist >= 0
        _, p = _softmax_block(s - slope * dist.astype(F32), mask)
        p = p / jnp.maximum(jnp.sum(p, axis=-1, keepdims=True), 1e-30)
        o_c = _dot(p.astype(BF16), kc_ref[:, vcol])
        psum = p[0:rt]
        for h in range(1, NSA_GROUP):
            psum = psum + p[h * rt:(h + 1) * rt]
        for lst, val in ((qgs, qg), (slopes, slope), (o_cs, o_c), (imps, _importance(psum, selmap_ref[...]))):
            lst.append(val)

    ncol = NSA_KV_HEADS * rt
    imp_pad = jnp.concatenate(imps + [jnp.zeros((SMALL_W - ncol, n_blk_lanes), F32)], axis=0)
    tq_row = past + lax.broadcasted_iota(jnp.int32, (1, SMALL_W), 1) % tn
    selm_all = _select_blocks_t(imp_pad.T, tq_row).T

    selms, q_augs, q_fulls = [], [], []
    for g in groups:
        selm = selm_all[g * rt:(g + 1) * rt]
        q_aug = jnp.concatenate([q8[:, (g * NSA_GROUP + h) * AUG:(g * NSA_GROUP + h + 1) * AUG]
                                 for h in range(NSA_GROUP)], axis=0)
        blk_bias = jnp.where(selm > 0.5, 0.0, MASKV)
        q_full = jnp.concatenate([q_aug, jnp.concatenate([blk_bias] * NSA_GROUP, axis=0)], axis=1).astype(BF16)
        for lst, val in ((selms, selm), (q_augs, q_aug), (q_fulls, q_full)):
            lst.append(val)

    bpt = kt // SEL_BLOCK
    carries = [_online_init(rows, hd) for _ in groups]
    for j in range(past // kt):
        k0 = j * kt
        for g in groups:
            k_t = buf[slot, kcols[g], k0:k0 + kt].astype(BF16)
            v_t = buf[slot, vcols[g], k0:k0 + kt].astype(BF16)
            q_rest = jnp.concatenate([q_fulls[g][:, hd:AUG], q_fulls[g][:, AUG + j * bpt:AUG + (j + 1) * bpt]], axis=1)
            s = _dot(q_fulls[g][:, 0:hd], k_t) + _dot(q_rest, onehot_ref[j])
            carries[g] = _flash_step(carries[g], s, v_t, _NT)

    outs = []
    for g in groups:
        kcol, vcol = kcols[g], vcols[g]
        qg, slope, o_c, selm, q_aug = qgs[g], slopes[g], o_cs[g], selms[g], q_augs[g]
        nb_new = past // SEL_BLOCK
        new_sel = jnp.concatenate([selm[:, nb_new:nb_new + 1]] * NSA_GROUP, axis=0) > 0.5
        kn = jnp.concatenate([jnp.concatenate([nsel_ref[:, kcol], zpad], axis=0), pos_cols(past, SMALL_W)], axis=1)
        s = _dot(q_aug.astype(BF16), kn.astype(BF16), _NT) + jnp.where(new_ok & new_sel, 0.0, MASKV)
        _, l, acc = _flash_step(carries[g], s, new_tile(nsel_ref, vcol))
        o_s = acc / jnp.maximum(l, 1e-30)

        s = _dot(qg, win_ref[kcol, :].astype(BF16))
        dist = t4 - (past - nbuf + lax.broadcasted_iota(jnp.int32, (1, nbuf), 1))
        mask = (dist >= 0) & (dist < WINDOW)
        carry = _online_update(_online_init(rows, hd), s - slope * dist.astype(F32), mask,
                               win_ref[vcol, :].astype(BF16), _NT)
        s = _dot(qg, new_tile(nwin_ref, kcol), _NT)
        _, l, acc = _online_update(carry, s - slope * new_dist.astype(F32), new_ok, new_tile(nwin_ref, vcol))
        o_w = acc / jnp.maximum(l, 1e-30)

        for h in range(NSA_GROUP):
            c = 3 * (g * NSA_GROUP + h)
            r = slice(h * rt, (h + 1) * rt)
            outs.append(gates[:, c:c + 1] * o_c[r] + gates[:, c + 1:c + 2] * o_s[r] + gates[:, c + 2:c + 3] * o_w[r])
    o_ref[...] = jnp.concatenate(outs, axis=1) * sza_ref[...]
    wout_ref[...] = jnp.concatenate([win_ref[:, tn:nbuf], nwint_ref[...]], axis=1)


def _nsa_sample(qs, kc, cache_t, win_t, kvs, kvw, small, sza, page_table, tn):
    nseq, npages = page_table.shape
    past = npages * PAGE_SIZE
    nbuf = win_t.shape[2]
    assert nbuf == WINDOW and 8 % tn == 0 and tn <= SEL_BLOCK
    n_cmp = kc.shape[1]
    n_blk = -(-(past + tn) // SEL_BLOCK)
    n_blk_lanes = -(-n_blk // 128) * 128
    selmap = _sel_map(n_cmp, n_blk, n_blk_lanes)
    kt = min(4096, past)
    ntile, bpt = past // kt, kt // SEL_BLOCK
    key = np.arange(past).reshape(ntile, 1, kt)
    pos_rows = np.zeros((ntile, AUG - HEAD_DIM, kt), np.float32)
    pos_rows[:, 0], pos_rows[:, 1] = key[:, 0] // POS_SPLIT, key[:, 0] % POS_SPLIT
    member = (key % kt // SEL_BLOCK == np.arange(bpt)[None, :, None]).astype(np.float32)
    onehot_t = jnp.asarray(np.concatenate([pos_rows, member], axis=1), dtype=BF16)
    seq3 = lambda r, w: pl.BlockSpec((None, r, w), lambda b, pt: (b, 0, 0))
    grid_spec = pltpu.PrefetchScalarGridSpec(
        num_scalar_prefetch=1,
        grid=(nseq,),
        in_specs=[seq3(8, NSA_HEADS * AUG), seq3(n_cmp, KV_ROW), pl.BlockSpec(memory_space=pl.ANY), seq3(KV_ROW, nbuf),
                  seq3(8, KV_ROW), seq3(8, KV_ROW), seq3(KV_ROW, tn), seq3(8, SMALL_W), seq3(8, NSA_WIDTH),
                  pl.BlockSpec(selmap.shape, lambda b, pt: (0, 0)),
                  pl.BlockSpec(onehot_t.shape, lambda b, pt: (0, 0, 0))],
        out_specs=[seq3(8, NSA_WIDTH), seq3(KV_ROW, nbuf)],
        scratch_shapes=[pltpu.VMEM((2, KV_ROW, past), F32), pltpu.SemaphoreType.DMA((2,))],
    )
    rep8 = lambda a: jnp.concatenate([a.reshape(nseq, tn, a.shape[-1])] * (8 // tn), axis=1)
    o8, win_new_t = pl.pallas_call(
        functools.partial(_nsa_sample_kernel, npages=npages, tn=tn, kt=kt),
        grid_spec=grid_spec,
        out_shape=[jax.ShapeDtypeStruct((nseq, 8, NSA_WIDTH), F32), jax.ShapeDtypeStruct((nseq, KV_ROW, nbuf), F32)],
        compiler_params=_cparams(("arbitrary",)),
        name="nsa_sample",
    )(page_table, rep8(qs), kc, cache_t, win_t, rep8(kvs), rep8(kvw), kvw.transpose(0, 2, 1), rep8(small), rep8(sza),
      selmap, onehot_t)
    return o8[:, :tn].reshape(nseq * tn, NSA_WIDTH), win_new_t


def _gdn_sample_kernel(eq_ref, ek_ref, ev_ref, cwq_ref, cwk_ref, cwv_ref, cbq_ref, cbk_ref, cbv_ref, small_ref,
                       alog_ref, dtb_ref, szb_ref, gn_ref, s_ref, o_ref, so_ref, qs_ref, ks_ref):
    h = pl.program_id(0)
    tn = o_ref.shape[0]
    dk, dv, nseq = so_ref.shape

    def conv(e_ref, cw_ref, cb_ref, t):
        y = cb_ref[...]
        for i in range(CONV_W):
            y = y + e_ref[t + i] * cw_ref[:, i:i + 1]
        return _silu(y)

    so_ref[...] = s_ref[...]
    neg_rate = -jnp.exp(alog_ref[...])
    for t in range(tn):
        q = conv(eq_ref, cwq_ref, cbq_ref, t)
        k = conv(ek_ref, cwk_ref, cbk_ref, t)
        v = conv(ev_ref, cwv_ref, cbv_ref, t)
        qs_ref[...] = q * lax.rsqrt(jnp.sum(q * q, axis=0, keepdims=True) + EPS) * (dk ** -0.5)
        ks_ref[...] = k * lax.rsqrt(jnp.sum(k * k, axis=0, keepdims=True) + EPS)
        a_in = small_ref[t, pl.ds(SMALL_A0 + h, 1), :]
        b_in = small_ref[t, pl.ds(SMALL_B0 + h, 1), :]
        decay = jnp.exp(neg_rate * _softplus(a_in + dtb_ref[...]))
        beta = jax.nn.sigmoid(b_in)

        def ks_step(i, acc):
            return acc + ks_ref[pl.ds(i, 1), :] * so_ref[i]

        k_s = lax.fori_loop(0, dk, ks_step, jnp.zeros((dv, nseq), F32), unroll=8)
        delta = beta * (v - decay * k_s)

        def upd_step(i, acc):
            s_new = decay * so_ref[i] + ks_ref[pl.ds(i, 1), :] * delta
            so_ref[i] = s_new
            return acc + qs_ref[pl.ds(i, 1), :] * s_new

        o = lax.fori_loop(0, dk, upd_step, jnp.zeros((dv, nseq), F32), unroll=8)
        o = o * lax.rsqrt(jnp.mean(o * o, axis=0, keepdims=True) + EPS) * gn_ref[...]
        o_ref[t] = o * szb_ref[t]


def _gdn_sample(qkvb, small, szb, state_conv, state_gdn, conv_w, conv_b, a_log, dt_bias, gnorm, tn):
    nseq = state_gdn.shape[0]
    ext = jnp.concatenate([state_conv, qkvb.reshape(nseq, tn, CONV_DIM)], axis=1)
    ext_t = ext.transpose(1, 2, 0)
    small_t = small.reshape(nseq, tn, SMALL_W).transpose(1, 2, 0)
    szb_t = szb.reshape(nseq, tn, GDN_WIDTH).transpose(1, 2, 0)
    s_t = state_gdn.transpose(1, 2, 3, 0)
    cw_t = conv_w.T
    cb_t = conv_b.reshape(-1, 1)
    alog_b = jnp.broadcast_to(a_log[:, None, None], (GDN_HEADS, 1, nseq))
    dtb_b = jnp.broadcast_to(dt_bias[:, None, None], (GDN_HEADS, 1, nseq))
    gn = gnorm.reshape(-1, 1)
    nqk = GDN_QK_WIDTH // GDN_DK
    chan = lambda off: pl.BlockSpec((CONV_W - 1 + tn, GDN_DK, nseq), lambda h: (0, off + h, 0))
    cwb = lambda off: pl.BlockSpec((GDN_DK, CONV_W), lambda h: (off + h, 0))
    cbb = lambda off: pl.BlockSpec((GDN_DK, 1), lambda h: (off + h, 0))
    perhead = pl.BlockSpec((None, 1, nseq), lambda h: (h, 0, 0))
    o_t, s_new = pl.pallas_call(
        _gdn_sample_kernel,
        grid=(GDN_HEADS,),
        in_specs=[chan(0), chan(nqk), chan(2 * nqk), cwb(0), cwb(nqk), cwb(2 * nqk), cbb(0), cbb(nqk), cbb(2 * nqk),
                  pl.BlockSpec((tn, SMALL_W, nseq), lambda h: (0, 0, 0)), perhead, perhead,
                  pl.BlockSpec((tn, GDN_DV, nseq), lambda h: (0, h, 0)),
                  pl.BlockSpec((GDN_DV, 1), lambda h: (0, 0)),
                  pl.BlockSpec((None, GDN_DK, GDN_DV, nseq), lambda h: (h, 0, 0, 0))],
        out_specs=[pl.BlockSpec((tn, GDN_DV, nseq), lambda h: (0, h, 0)),
                   pl.BlockSpec((None, GDN_DK, GDN_DV, nseq), lambda h: (h, 0, 0, 0))],
        out_shape=[jax.ShapeDtypeStruct((tn, GDN_WIDTH, nseq), F32),
                   jax.ShapeDtypeStruct((GDN_HEADS, GDN_DK, GDN_DV, nseq), F32)],
        scratch_shapes=[pltpu.VMEM((GDN_DK, nseq), F32), pltpu.VMEM((GDN_DK, nseq), F32)],
        compiler_params=_cparams(("parallel",)),
        name="gdn_sample",
    )(ext_t, ext_t, ext_t, cw_t, cw_t, cw_t, cb_t, cb_t, cb_t, small_t, alog_b, dtb_b, szb_t, gn, s_t)
    o_b = o_t.transpose(2, 0, 1).reshape(nseq * tn, GDN_WIDTH)
    return o_b, ext[:, tn:], s_new.transpose(3, 0, 1, 2)


def _sample_path(x, ada, lw, cw, cache_cmp, cache_sel, cache_win, state_conv, state_gdn, page_table):
    (norm_g, w_bf, offs, conv_w, conv_b, a_log, dt_bias, gnorm, wa, wb, wo, final_g) = lw
    nseq, tn, d = x.shape
    n = nseq * tn
    x2 = x.reshape(n, d)
    ada_rows = jnp.repeat(ada, tn, axis=0)
    tm = min(256, n)
    mod = lambda k: pl.BlockSpec((tm, d), lambda i: (i, k))
    (qs, kvc_t, kvs_t, kvw_t, _, _, _, _, _, sza, qkvb, szb, gms, small) = _inproj(
        x2, ada_rows, ada_rows, (mod(1), mod(0)), norm_g.reshape(1, d), w_bf, offs, tm, n)
    rows = lambda a_t: a_t.reshape(KV_ROW, nseq, tn).transpose(1, 2, 0)
    kvc, kvs, kvw = rows(kvc_t), rows(kvs_t), rows(kvw_t)
    kc = _compress_sample(_kv_rows_t(cache_cmp), page_table, *cw)
    o_a, win_new_t = _nsa_sample(qs, kc, _kv_rows_t(cache_sel), _kv_rows_t(cache_win), kvs, kvw, small, sza, page_table, tn)
    o_b, conv_new, s_new = _gdn_sample(qkvb, small, szb, state_conv, state_gdn, conv_w, conv_b, a_log, dt_bias, gnorm, tn)
    y = _outproj(x2, o_a, o_b, gms, ada_rows, mod(2), wa, wb, wo, final_g.reshape(1, d), tm)
    kvshape = (1, nseq, tn, 2, NSA_KV_HEADS, HEAD_DIM)
    return (y.reshape(nseq, tn, d), kvc.reshape(kvshape), kvs.reshape(kvshape), _kv_leaf(win_new_t),
            conv_new[None], s_new[None])


def kernel(x_prompt, x_sample, cache_cmp_kv, cache_sel_kv, cache_win_kv, state_conv, state_gdn, page_table, c_prompt, c_sample, norm_g, w_ada, b_ada, w_in, cmp_pe_k, cmp_w1_k, cmp_w2_k, cmp_pe_v, cmp_w1_v, cmp_w2_v, conv_w, conv_b, gdn_a_log, gdn_dt_bias, gdn_norm_g, w_o_nsa, w_o_gdn, w_out, final_g):
    assert norm_g.shape[0] == 1, "single trunk layer"
    d = x_prompt.shape[-1]
    l = 0
    offs, _ = _seg_offsets(d)
    w_bf = _prep_w_in(w_in[l], d)
    ada = _ada(jnp.concatenate([c_prompt, c_sample], axis=0), w_ada[l].astype(BF16), b_ada[l])
    cw = _compress_weights(cmp_pe_k[l], cmp_w1_k[l], cmp_w2_k[l], cmp_pe_v[l], cmp_w1_v[l], cmp_w2_v[l])
    lw = (norm_g[l], w_bf, offs, conv_w[l], conv_b[l], gdn_a_log[l], gdn_dt_bias[l], gdn_norm_g[l],
          w_o_nsa[l].astype(BF16), w_o_gdn[l].astype(BF16), w_out[l].astype(BF16), final_g)
    nb = c_prompt.shape[0]
    yp, cmp_p, sel_p, win_p, conv_p, gdn_p = _prompt_path(x_prompt, ada[:nb], lw, cw)
    ys, cmp_s, sel_s, win_s, conv_s, gdn_s = _sample_path(
        x_sample, ada[nb:], lw,
        _compress_weights_kv(cmp_pe_k[l], cmp_w1_k[l], cmp_w2_k[l], cmp_pe_v[l], cmp_w1_v[l], cmp_w2_v[l]),
        cache_cmp_kv[l], cache_sel_kv[l], cache_win_kv[l], state_conv[l], state_gdn[l],
        page_table)
    return (yp, ys, cmp_p, sel_p, win_p, conv_p, gdn_p, cmp_s, sel_s, win_s, conv_s, gdn_s)
```

```python
import functools
import math

import numpy as np
import jax
import jax.numpy as jnp
from jax import lax
from jax.experimental import pallas as pl
from jax.experimental.pallas import tpu as pltpu

F32 = jnp.float32
BF16 = jnp.bfloat16

NSA_HEADS = 8
NSA_KV_HEADS = 2
NSA_GROUP = NSA_HEADS // NSA_KV_HEADS
HEAD_DIM = 64
CMP_BLOCK = 32
CMP_STRIDE = 16
CMP_HIDDEN = 64
SEL_BLOCK = 64
N_SEL = 16
WINDOW = 512
Q_BLOCK = 128
GDN_HEADS = 8
GDN_DK = 64
GDN_DV = 64
CONV_W = 4
GDN_CHUNK = 64
PAGE_SIZE = 128
EPS = 1e-6

NSA_WIDTH = NSA_HEADS * HEAD_DIM
KV_ROW = 2 * NSA_KV_HEADS * HEAD_DIM
GDN_QK_WIDTH = GDN_HEADS * GDN_DK
GDN_WIDTH = GDN_HEADS * GDN_DV
CONV_DIM = 2 * GDN_QK_WIDTH + GDN_WIDTH
SMALL_W = 128
SMALL_A0 = 3 * NSA_HEADS
SMALL_B0 = SMALL_A0 + GDN_HEADS

NEG = -1e30
MASKV = -(2.0 ** 100)
AUG = 128
POS_SPLIT = 128
VMEM_LIMIT = 56 * 1024 * 1024

_NT = (((1,), (1,)), ((), ()))
_NN = (((1,), (0,)), ((), ()))


def _alibi_slopes():
    h = np.arange(1, NSA_HEADS + 1, dtype=np.float32)
    return [float(v) for v in np.power(np.float32(2.0), -np.float32(8.0) * h / np.float32(NSA_HEADS))]


_SLOPES = _alibi_slopes()


def _dot(a, b, dims=_NN):
    return lax.dot_general(a, b, dims, preferred_element_type=F32)


def _split(a):
    hi = a.astype(BF16)
    lo = (a - hi.astype(F32)).astype(BF16)
    return hi, lo


def _dot3s(a_split, b_split, dims=_NN):
    (ah, al), (bh, bl) = a_split, b_split
    return _dot(ah, bh, dims) + _dot(ah, bl, dims) + _dot(al, bh, dims)


def _silu(x):
    return x * jax.nn.sigmoid(x)


def _cparams(sem, flags=None):
    return pltpu.CompilerParams(dimension_semantics=sem, vmem_limit_bytes=VMEM_LIMIT, flags=flags)


def _ada_kernel(c_ref, w_ref, b_ref, o_ref):
    sc = _silu(c_ref[...]).astype(BF16)
    o_ref[...] = _dot(sc, w_ref[...]) + b_ref[...]


def _ada(c, w_bf, b):
    n, d = c.shape
    n_pad = -(-n // 8) * 8
    c = jnp.pad(c, ((0, n_pad - n), (0, 0)))
    out = pl.pallas_call(
        _ada_kernel,
        out_shape=jax.ShapeDtypeStruct((n_pad, w_bf.shape[1]), F32),
        name="ada",
    )(c, w_bf, b.reshape(1, -1))
    return out[:n]


_SEG = (("q", NSA_WIDTH), ("kvc", KV_ROW), ("kvs", KV_ROW), ("kvw", KV_ROW), ("za", NSA_WIDTH),
        ("qkvb", CONV_DIM), ("zb", GDN_WIDTH), ("gm", None), ("small", SMALL_W))


def _seg_offsets(d_model):
    offs, c = {}, 0
    for name, n in _SEG:
        n = 2 * d_model if n is None else n
        offs[name] = (c, n)
        c += n
    return offs, c


def _aug_cols(rows, c0, c1):
    lane = lax.broadcasted_iota(jnp.int32, (rows, AUG - HEAD_DIM), 1)
    return jnp.where(lane == 0, c0, jnp.where(lane == 1, c1, 0.0)).astype(F32)


def _aug_keys(kv, pos):
    hi = jnp.floor(pos * (1.0 / POS_SPLIT))
    cols = _aug_cols(kv.shape[0], hi, pos - hi * POS_SPLIT)
    parts = []
    for g in range(NSA_KV_HEADS):
        parts += [kv[:, g * HEAD_DIM:(g + 1) * HEAD_DIM], cols]
    k_aug = jnp.concatenate(parts, axis=1).astype(BF16)
    kv_t = kv.T
    return k_aug, kv_t[NSA_KV_HEADS * HEAD_DIM:].astype(BF16), kv_t


def _inproj_kernel(x_ref, scale_ref, shift_ref, ng_ref, w_ref,
                   q_ref, kvct_ref, kvst_ref, kvwt_ref, kvcb_ref, ksel_ref, vselt_ref, kwin_ref, vwint_ref,
                   sza_ref, qkvb_ref, szb_ref, gms_ref, small_ref, *, offs, seq):
    x = x_ref[...]
    tm = x.shape[0]
    y = x * lax.rsqrt(jnp.mean(x * x, axis=-1, keepdims=True) + EPS) * ng_ref[...]
    h = y * (1.0 + scale_ref[...]) + shift_ref[...]
    hb = h.astype(BF16)

    def seg(name):
        c0, n = offs[name]
        return _dot(hb, w_ref[:, c0:c0 + n])

    qv = seg("q") * (HEAD_DIM ** -0.5)
    parts = []
    for hh in range(NSA_HEADS):
        parts += [qv[:, hh * HEAD_DIM:(hh + 1) * HEAD_DIM], _aug_cols(tm, _SLOPES[hh] * POS_SPLIT, _SLOPES[hh])]
    q_ref[...] = jnp.concatenate(parts, axis=1).astype(BF16)

    pos = ((pl.program_id(0) * tm) % seq + lax.broadcasted_iota(jnp.int32, (tm, 1), 0)).astype(F32)
    v = seg("kvc")
    kvct_ref[...] = v.T
    kvcb_ref[...] = v.astype(BF16)
    for name, f_ref, k_ref, vt_ref in (("kvs", kvst_ref, ksel_ref, vselt_ref), ("kvw", kvwt_ref, kwin_ref, vwint_ref)):
        k_ref[...], vt_ref[...], f_ref[...] = _aug_keys(seg(name), pos)
    sza_ref[...] = _silu(seg("za"))
    qkvb_ref[...] = seg("qkvb")
    szb_ref[...] = _silu(seg("zb"))
    gms_ref[...] = jax.nn.sigmoid(seg("gm"))
    small_ref[...] = seg("small")


def _inproj(x2, scale_arr, shift_arr, mod_specs, ng, w_bf, offs, tm, seq):
    n, d = x2.shape
    wtot = w_bf.shape[1]
    nbatch = n // seq
    spt = seq // tm
    vw = NSA_KV_HEADS * HEAD_DIM
    row = lambda w: pl.BlockSpec((tm, w), lambda i: (i, 0))
    rows = lambda w, dt: (jax.ShapeDtypeStruct((n, w), dt), row(w))
    tr = lambda w, dt: (jax.ShapeDtypeStruct((nbatch, w, seq), dt),
                        pl.BlockSpec((None, w, tm), lambda i: (i // spt, 0, i % spt)))
    vt = tr(vw, BF16)
    kvt = tr(KV_ROW, F32)
    outs = [rows(NSA_HEADS * AUG, BF16), kvt, kvt, kvt, rows(KV_ROW, BF16),
            rows(NSA_KV_HEADS * AUG, BF16), vt, rows(NSA_KV_HEADS * AUG, BF16), vt,
            rows(offs["za"][1], F32), rows(offs["qkvb"][1], F32), rows(offs["zb"][1], F32), rows(offs["gm"][1], F32),
            rows(offs["small"][1], F32)]
    out_shape = [o[0] for o in outs]
    out_specs = [o[1] for o in outs]
    return pl.pallas_call(
        functools.partial(_inproj_kernel, offs=offs, seq=seq),
        grid=(n // tm,),
        in_specs=[row(d), mod_specs[0], mod_specs[1],
                  pl.BlockSpec((1, d), lambda i: (0, 0)),
                  pl.BlockSpec((d, wtot), lambda i: (0, 0))],
        out_specs=out_specs,
        out_shape=out_shape,
        compiler_params=_cparams(("parallel",)),
        name="inproj",
    )(x2, scale_arr, shift_arr, ng, w_bf)


def _outproj_kernel(x_ref, oa_ref, ob_ref, gms_ref, gate_ref, wa_ref, wb_ref, wo_ref, fg_ref, y_ref, *, d):
    ma = _dot(oa_ref[...].astype(BF16), wa_ref[...])
    mb = _dot(ob_ref[...].astype(BF16), wb_ref[...])
    m = gms_ref[:, 0:d] * ma + gms_ref[:, d:2 * d] * mb
    y = x_ref[...] + gate_ref[...] * _dot(m.astype(BF16), wo_ref[...])
    y_ref[...] = y * lax.rsqrt(jnp.mean(y * y, axis=-1, keepdims=True) + EPS) * fg_ref[...]


def _outproj(x2, oa, ob, gms, gate_arr, gate_spec, wa, wb, wo, fg, tm):
    n, d = x2.shape
    row = lambda w: pl.BlockSpec((tm, w), lambda i: (i, 0))
    full = lambda a: pl.BlockSpec(a.shape, lambda i: (0, 0))
    return pl.pallas_call(
        functools.partial(_outproj_kernel, d=d),
        grid=(n // tm,),
        in_specs=[row(d), row(oa.shape[1]), row(ob.shape[1]), row(2 * d), gate_spec,
                  full(wa), full(wb), full(wo), full(fg)],
        out_specs=row(d),
        out_shape=jax.ShapeDtypeStruct((n, d), F32),
        compiler_params=_cparams(("parallel",)),
        name="outproj",
    )(x2, oa, ob, gms, gate_arr, wa, wb, wo, fg)


def _compress_math(x, w1_ref, pe_ref, w2_ref):
    n16 = x.shape[0]
    a0 = _dot(x, w1_ref[0])
    a1 = _dot(x, w1_ref[1])
    p0 = _dot(pe_ref[0], w1_ref[0]) + _dot(pe_ref[1], w1_ref[1])
    pre = a0 + pltpu.roll(a1, n16 - 1, 0) + p0[0:1]
    return _dot(_silu(pre).astype(BF16), w2_ref[...])


def _compress_kernel(x_ref, w1_ref, pe_ref, w2_ref, k_ref, vt_ref):
    kv = _compress_math(x_ref[...], w1_ref, pe_ref, w2_ref)
    n16 = kv.shape[0]
    end = (lax.broadcasted_iota(jnp.int32, (n16, 1), 0) * CMP_STRIDE + (CMP_BLOCK - 1)).astype(F32)
    k_ref[...], vt_ref[...], _ = _aug_keys(kv, end)


def _compress_prompt(kvcb, w1big, pebig, w2big, batch):
    n = kvcb.shape[0]
    n16 = n // batch // CMP_STRIDE
    x = kvcb.reshape(batch, n16, CMP_STRIDE * KV_ROW)
    vw = NSA_KV_HEADS * HEAD_DIM
    return pl.pallas_call(
        _compress_kernel,
        grid=(batch,),
        in_specs=[pl.BlockSpec((None, n16, CMP_STRIDE * KV_ROW), lambda b: (b, 0, 0)),
                  pl.BlockSpec(w1big.shape, lambda b: (0, 0, 0)),
                  pl.BlockSpec(pebig.shape, lambda b: (0, 0, 0)),
                  pl.BlockSpec(w2big.shape, lambda b: (0, 0))],
        out_specs=[pl.BlockSpec((None, n16, NSA_KV_HEADS * AUG), lambda b: (b, 0, 0)),
                   pl.BlockSpec((None, vw, n16), lambda b: (b, 0, 0))],
        out_shape=[jax.ShapeDtypeStruct((batch, n16, NSA_KV_HEADS * AUG), BF16),
                   jax.ShapeDtypeStruct((batch, vw, n16), BF16)],
        compiler_params=_cparams(("parallel",)),
        name="compress_prompt",
    )(x, w1big, pebig, w2big)


def _compress_weights(pe_k, w1_k, w2_k, pe_v, w1_v, w2_v):
    r_cnt = CMP_BLOCK // CMP_STRIDE
    g = NSA_KV_HEADS
    eye2 = jnp.eye(2, dtype=F32)
    eyeg = jnp.eye(g, dtype=F32)
    w1 = jnp.stack([w1_k, w1_v]).reshape(2, r_cnt, CMP_STRIDE, HEAD_DIM, CMP_HIDDEN)
    w1big = jnp.einsum("krsde,kK,gG->rskgdKGe", w1, eye2, eyeg)
    w1big = w1big.reshape(r_cnt, CMP_STRIDE * KV_ROW, 2 * g * CMP_HIDDEN).astype(BF16)
    pe = jnp.stack([pe_k, pe_v]).reshape(2, r_cnt, CMP_STRIDE, HEAD_DIM)
    pebig = jnp.broadcast_to(pe.transpose(1, 2, 0, 3)[:, :, :, None, :], (r_cnt, CMP_STRIDE, 2, g, HEAD_DIM))
    pebig = pebig.reshape(r_cnt, 1, CMP_STRIDE * KV_ROW)
    pebig = jnp.pad(pebig, ((0, 0), (0, 7), (0, 0))).astype(BF16)
    w2 = jnp.stack([w2_k, w2_v])
    w2big = jnp.einsum("ked,kK,gG->kgeKGd", w2, eye2, eyeg).reshape(2 * g * CMP_HIDDEN, KV_ROW).astype(BF16)
    return w1big, pebig, w2big


def _sel_map(n_cmp_rows, n_blk, n_blk_pad):
    i = np.arange(n_cmp_rows)[:, None] * CMP_STRIDE
    j = np.arange(n_blk_pad)[None, :] * SEL_BLOCK
    ov = np.minimum(i + CMP_BLOCK, j + SEL_BLOCK) - np.maximum(i, j)
    m = np.clip(ov, 0, None).astype(np.float32) / np.float32(CMP_BLOCK)
    m[:, n_blk:] = 0.0
    return jnp.asarray(m, dtype=BF16)


def _softmax_block(s, mask):
    s = jnp.where(mask, s, NEG)
    m = jnp.max(s, axis=-1, keepdims=True)
    p = jnp.where(mask, jnp.exp(s - m), 0.0)
    return m, p


def _online_update(carry, s, mask, v_bf, v_dims=_NN):
    m, l, acc = carry
    s = jnp.where(mask, s, NEG)
    m_new = jnp.maximum(m, jnp.max(s, axis=-1, keepdims=True))
    alpha = jnp.exp(m - m_new)
    p = jnp.where(mask, jnp.exp(s - m_new), 0.0)
    l = alpha * l + jnp.sum(p, axis=-1, keepdims=True)
    acc = alpha * acc + _dot(p.astype(BF16), v_bf, v_dims)
    return m_new, l, acc


def _flash_step(carry, s, v_bf, v_dims=_NN):
    m, l, acc = carry
    m_new = jnp.maximum(m, jnp.max(s, axis=-1, keepdims=True))
    alpha = jnp.exp(m - m_new)
    p = jnp.exp(s - m_new)
    return (m_new, alpha * l + jnp.sum(p, axis=-1, keepdims=True),
            alpha * acc + _dot(p.astype(BF16), v_bf, v_dims))


def _online_init(rows, hd):
    return (jnp.full((rows, 1), NEG, F32), jnp.zeros((rows, 1), F32), jnp.zeros((rows, hd), F32))


def _importance(p_sum, selmap):
    ph, pl_ = _split(p_sum)
    pl2 = (p_sum - ph.astype(F32) - pl_.astype(F32)).astype(BF16)
    return _dot(ph, selmap) + _dot(pl_, selmap) + _dot(pl2, selmap)


def _flash_step_t(carry, s_t, v_t):
    m, l, acc = carry
    m_new = jnp.maximum(m, jnp.max(s_t, axis=0, keepdims=True))
    alpha = jnp.exp(m - m_new)
    p = jnp.exp(s_t - m_new)
    l = alpha * l + jnp.sum(p, axis=0, keepdims=True)
    acc = alpha * acc + _dot(v_t, p.astype(BF16))
    return m_new, l, acc


def _flash_init_t(cols, dv):
    return (jnp.full((1, cols), NEG, F32), jnp.zeros((1, cols), F32), jnp.zeros((dv, cols), F32))


def _select_blocks_t(imp_t, tq):
    nb, nq = imp_t.shape
    jj = lax.broadcasted_iota(jnp.int32, (nb, nq), 0)
    cur = tq // SEL_BLOCK
    forced = (jj == 0) | (jj == cur) | (jj == cur - 1)
    jf = jj.astype(F32)
    score = jnp.where(forced, NEG, jnp.where(jj <= cur, imp_t, NEG))
    selm = jnp.where(forced, 1.0, 0.0).astype(F32)
    for _ in range(N_SEL - 3):
        m = jnp.max(score, axis=0, keepdims=True)
        idx = jnp.min(jnp.where(score == m, jf, float(nb)), axis=0, keepdims=True)
        hit = jnp.where(m > 0.5 * NEG, idx, -1.0) == jf
        selm = jnp.where(hit, 1.0, selm)
        score = jnp.where(hit, NEG, score)
    return selm


def _nsa_prompt_kernel(q_ref, kc_ref, vct_ref, ksel_ref, vselt_ref, kwin_ref, vwint_ref, onehot_ref, small_ref, sza_ref,
                       selmapt_ref, o_ref, flags_ref, m_ref, l_ref, acc_ref, *, kt):
    qb = Q_BLOCK
    hd = HEAD_DIM
    q0 = pl.program_id(1) * qb
    n_cmp = kc_ref.shape[0]
    cols = NSA_GROUP * qb
    tq = q0 + lax.broadcasted_iota(jnp.int32, (1, qb), 1)
    gates_t = jax.nn.sigmoid(small_ref[...]).T
    wlen = WINDOW + qb
    w0 = pl.multiple_of(jnp.maximum(q0 - WINDOW, 0), qb)
    tile4 = lambda a: jnp.concatenate([a] * NSA_GROUP, axis=1)

    def keymask(pos_col, lo):
        dist = tq - pos_col
        ok = (dist >= 0) if lo is None else ((dist >= 0) & (dist < lo))
        return tile4(jnp.where(ok, 0.0, MASKV).astype(F32))

    cmp_bias = keymask(lax.broadcasted_iota(jnp.int32, (n_cmp, 1), 0) * CMP_STRIDE + (CMP_BLOCK - 1), None)
    win_bias = keymask(w0 + lax.broadcasted_iota(jnp.int32, (wlen, 1), 0), WINDOW)
    jlast = q0 // kt
    klast = pl.multiple_of(jlast * kt, kt)
    diag_bias = keymask(klast + lax.broadcasted_iota(jnp.int32, (kt, 1), 0), None)

    groups = range(NSA_KV_HEADS)
    kcols = [slice(g * AUG, (g + 1) * AUG) for g in groups]
    vrows = [slice(g * hd, (g + 1) * hd) for g in groups]
    q_heads = [[q_ref[:, (g * NSA_GROUP + h) * AUG:(g * NSA_GROUP + h + 1) * AUG] for h in range(NSA_GROUP)]
               for g in groups]
    q_cat = [jnp.concatenate(q_heads[g], axis=0) for g in groups]
    finish = lambda carry: carry[2] * (1.0 / jnp.maximum(carry[1], 1e-30))

    o_w = [finish(_flash_step_t(_flash_init_t(cols, hd), _dot(kwin_ref[pl.ds(w0, wlen), kcols[g]], q_cat[g], _NT) + win_bias,
                                vwint_ref[vrows[g], pl.ds(w0, wlen)])) for g in groups]

    o_c, q_full, psums = [], [], []
    sm_t = selmapt_ref[...]
    for g in groups:
        s_t = _dot(kc_ref[:, kcols[g]], q_cat[g], _NT) + cmp_bias
        m = jnp.maximum(jnp.max(s_t, axis=0, keepdims=True), NEG)
        p = jnp.exp(s_t - m)
        p = p * (1.0 / jnp.maximum(jnp.sum(p, axis=0, keepdims=True), 1e-30))
        o_c.append(_dot(vct_ref[vrows[g], :], p.astype(BF16)))
        psum = p[:, 0:qb]
        for h in range(1, NSA_GROUP):
            psum = psum + p[:, h * qb:(h + 1) * qb]
        psums.append(psum)
    psum = jnp.concatenate(psums, axis=1)
    ph, pl_ = _split(psum)
    pl2 = (psum - ph.astype(F32) - pl_.astype(F32)).astype(BF16)
    imp_t = _dot(sm_t, ph) + _dot(sm_t, pl_) + _dot(sm_t, pl2)
    selm_t = _select_blocks_t(imp_t, jnp.concatenate([tq] * NSA_KV_HEADS, axis=1))
    for g in groups:
        selm = selm_t[:, g * qb:(g + 1) * qb].T
        blk_bias = jnp.where(selm > 0.5, 0.0, MASKV).astype(BF16)
        q_full.append(jnp.concatenate([jnp.concatenate([qh, blk_bias], axis=1) for qh in q_heads[g]], axis=0))
        blk_any = jnp.max(selm, axis=0, keepdims=True)
        bpt = kt // SEL_BLOCK
        for j in range(flags_ref.shape[1]):
            flags_ref[g, j] = (jnp.max(blk_any[:, j * bpt:(j + 1) * bpt]) > 0.5).astype(jnp.int32)
        for ref, val in zip((m_ref, l_ref, acc_ref), _flash_init_t(cols, hd)):
            ref[g] = val

    def sel_step(k0, g, bias):
        k_op = jnp.concatenate([ksel_ref[pl.ds(k0, kt), kcols[g]], onehot_ref[pl.ds(k0, kt), :]], axis=1)
        s_t = _dot(k_op, q_full[g], _NT)
        s_t = s_t if bias is None else s_t + bias
        m, l, acc = _flash_step_t((m_ref[g], l_ref[g], acc_ref[g]), s_t, vselt_ref[vrows[g], pl.ds(k0, kt)])
        m_ref[g], l_ref[g], acc_ref[g] = m, l, acc

    def sel_tile(j, c):
        for g in groups:
            @pl.when(flags_ref[g, j] > 0)
            def _():
                sel_step(pl.multiple_of(j * kt, kt), g, None)
        return c

    lax.fori_loop(0, jlast, sel_tile, 0)
    o_s = []
    for g in groups:
        sel_step(klast, g, diag_bias)
        o_s.append(finish((m_ref[g], l_ref[g], acc_ref[g])))

    outs = []
    for g in groups:
        for h in range(NSA_GROUP):
            c = 3 * (g * NSA_GROUP + h)
            r = slice(h * qb, (h + 1) * qb)
            outs.append(gates_t[c:c + 1, :] * o_c[g][:, r] + gates_t[c + 1:c + 2, :] * o_s[g][:, r]
                        + gates_t[c + 2:c + 3, :] * o_w[g][:, r])
    o_ref[...] = jnp.concatenate(outs, axis=0).T * sza_ref[...]


def _nsa_prompt(q_aug, kc_aug, vc_t, ksel, vsel_t, kwin, vwin_t, small, sza, batch):
    n = q_aug.shape[0]
    seq = n // batch
    nq = seq // Q_BLOCK
    n_cmp = kc_aug.shape[1]
    n_blk = -(-seq // SEL_BLOCK)
    assert n_blk <= AUG and seq % Q_BLOCK == 0 and seq >= WINDOW + Q_BLOCK
    kt = min(512, seq)
    selmap_t = _sel_map(n_cmp, n_blk, AUG).T
    pos = np.arange(seq)[:, None] // SEL_BLOCK
    onehot = jnp.asarray(pos == np.arange(AUG)[None, :], dtype=BF16)
    vw = NSA_KV_HEADS * HEAD_DIM
    kw = NSA_KV_HEADS * AUG
    row = lambda w: pl.BlockSpec((Q_BLOCK, w), lambda b, i: (b * nq + i, 0))
    kspec = pl.BlockSpec((None, seq, kw), lambda b, i: (b, 0, 0))
    vspec = pl.BlockSpec((None, vw, seq), lambda b, i: (b, 0, 0))
    return pl.pallas_call(
        functools.partial(_nsa_prompt_kernel, kt=kt),
        grid=(batch, nq),
        in_specs=[row(NSA_HEADS * AUG),
                  pl.BlockSpec((None, n_cmp, kw), lambda b, i: (b, 0, 0)),
                  pl.BlockSpec((None, vw, n_cmp), lambda b, i: (b, 0, 0)),
                  kspec, vspec, kspec, vspec,
                  pl.BlockSpec(onehot.shape, lambda b, i: (0, 0)),
                  row(SMALL_W), row(NSA_WIDTH),
                  pl.BlockSpec(selmap_t.shape, lambda b, i: (0, 0))],
        out_specs=row(NSA_WIDTH),
        out_shape=jax.ShapeDtypeStruct((n, NSA_WIDTH), F32),
        scratch_shapes=[pltpu.SMEM((NSA_KV_HEADS, seq // kt), jnp.int32),
                        pltpu.VMEM((NSA_KV_HEADS, 1, NSA_GROUP * Q_BLOCK), F32),
                        pltpu.VMEM((NSA_KV_HEADS, 1, NSA_GROUP * Q_BLOCK), F32),
                        pltpu.VMEM((NSA_KV_HEADS, HEAD_DIM, NSA_GROUP * Q_BLOCK), F32)],
        compiler_params=_cparams(("parallel", "arbitrary")),
        name="nsa_prompt",
    )(q_aug, kc_aug, vc_t, ksel.reshape(batch, seq, kw), vsel_t, kwin.reshape(batch, seq, kw), vwin_t, onehot, small, sza,
      selmap_t)


def _softplus(x):
    return jnp.maximum(x, 0.0) + jnp.log1p(jnp.exp(-jnp.abs(x)))


def _gdn_prompt_kernel(qkv_ref, small_ref, szb_ref, cw_ref, cb_ref, alog_ref, dtb_ref, gn_ref,
                       o_ref, sfin_ref, s_ref, prev_ref):
    c = pl.program_id(0)
    ck = GDN_CHUNK
    dk = GDN_DK
    nb = qkv_ref.shape[0]

    @pl.when(c == 0)
    def _():
        s_ref[...] = jnp.zeros_like(s_ref)
        prev_ref[...] = jnp.zeros_like(prev_ref)

    ii = lax.broadcasted_iota(jnp.int32, (ck, ck), 0)
    jj = lax.broadcasted_iota(jnp.int32, (ck, ck), 1)
    tri = jnp.where(ii >= jj, 1.0, 0.0).astype(F32)
    acts, betas, gcs, gcts = [], [], [], []
    for b in range(nb):
        u = qkv_ref[b]
        ext = jnp.concatenate([prev_ref[b], u], axis=0)
        y = cb_ref[...]
        for i in range(CONV_W):
            y = y + ext[8 - (CONV_W - 1) + i:8 - (CONV_W - 1) + i + ck] * cw_ref[i:i + 1, :]
        prev_ref[b] = u[ck - 8:ck]
        acts.append(_silu(y))
        small = small_ref[b]
        g_all = -jnp.exp(alog_ref[...]) * _softplus(small + dtb_ref[...])
        betas.append(jax.nn.sigmoid(small))
        gcs.append(lax.dot_general(tri, g_all, _NN, precision=lax.Precision.HIGHEST, preferred_element_type=F32))
        gcts.append(gcs[b].T)

    npair = GDN_HEADS // 2
    pairs = range(nb * npair)
    pw2 = 2 * dk
    lane = lax.broadcasted_iota(jnp.int32, (ck, pw2), 1)
    left = lane < dk
    row2 = lax.broadcasted_iota(jnp.int32, (ck, pw2), 0)
    col2 = jnp.where(left, lane, lane - dk)
    eye2 = jnp.where(row2 == col2, 1.0, 0.0).astype(F32)

    def halves(col_l, col_r):
        return jnp.where(left, col_l, col_r)

    def head_norm(x):
        sq = x * x
        return halves(jnp.sum(jnp.where(left, sq, 0.0), axis=-1, keepdims=True),
                      jnp.sum(jnp.where(left, 0.0, sq), axis=-1, keepdims=True))

    def bd(b):
        return _split(jnp.concatenate([jnp.where(left, b, 0.0), jnp.where(left, 0.0, b)], axis=0))

    q, k, v, beta, gc, decay, g_last = [], [], [], [], [], [], []
    for bp in pairs:
        b, p = divmod(bp, npair)
        act, beta_all, gc_all, gc_t = acts[b], betas[b], gcs[b], gcts[b]
        qp = act[:, p * pw2:(p + 1) * pw2]
        kp = act[:, GDN_QK_WIDTH + p * pw2:GDN_QK_WIDTH + (p + 1) * pw2]
        q.append(qp * lax.rsqrt(head_norm(qp) + EPS) * (dk ** -0.5))
        k.append(kp * lax.rsqrt(head_norm(kp) + EPS))
        v.append(act[:, 2 * GDN_QK_WIDTH + p * pw2:2 * GDN_QK_WIDTH + (p + 1) * pw2])
        h0, h1 = 2 * p, 2 * p + 1
        beta.append(halves(beta_all[:, SMALL_B0 + h0:SMALL_B0 + h0 + 1], beta_all[:, SMALL_B0 + h1:SMALL_B0 + h1 + 1]))
        gcp = halves(gc_all[:, SMALL_A0 + h0:SMALL_A0 + h0 + 1], gc_all[:, SMALL_A0 + h1:SMALL_A0 + h1 + 1])
        grp = jnp.concatenate([gc_t[SMALL_A0 + h0:SMALL_A0 + h0 + 1, :], gc_t[SMALL_A0 + h1:SMALL_A0 + h1 + 1, :]], axis=1)
        gc.append(gcp)
        g_last.append(gcp[ck - 1:ck, :])
        decay.append(jnp.exp(jnp.where(row2 >= col2, gcp - grp, NEG)))
    kb = [k[p] * beta[p] for p in pairs]
    kbd = [bd(k[p]) for p in pairs]
    qk = [_dot3s(_split(jnp.concatenate([q[p], kb[p]], axis=0)), kbd[p], _NT) for p in pairs]
    a_in = [qk[p][0:ck] * decay[p] for p in pairs]
    lmat = [jnp.where(row2 > col2, qk[p][ck:2 * ck] * decay[p], 0.0) for p in pairs]
    x = [eye2 - lmat[p] for p in pairs]
    pw = lmat
    for it in range(int(math.log2(ck)) - 1):
        if it < 3:
            pw = [_dot3s(_split(pw[p]), bd(pw[p])) for p in pairs]
            x = [x[p] + _dot3s(_split(x[p]), bd(pw[p])) for p in pairs]
        else:
            pw = [_dot(pw[p].astype(BF16), bd(pw[p])[0]) for p in pairs]
            x = [x[p] + _dot(x[p].astype(BF16), bd(pw[p])[0]) for p in pairs]
    eg = [jnp.exp(gc[p]) for p in pairs]
    xs = [_split(x[p]) for p in pairs]
    uu = [_dot3s(xs[p], bd(v[p] * beta[p])) for p in pairs]
    ww = [_dot3s(xs[p], bd(kb[p] * eg[p])) for p in pairs]
    s_old = [s_ref[p] for p in pairs]
    qw_s = [_dot3s(_split(jnp.concatenate([q[p] * eg[p], ww[p]], axis=0)), bd(s_old[p])) for p in pairs]
    v_new = [uu[p] - qw_s[p][ck:2 * ck] for p in pairs]
    o = [qw_s[p][0:ck] + _dot(a_in[p].astype(BF16), bd(v_new[p])[0]) for p in pairs]
    gn2 = jnp.concatenate([gn_ref[...]] * 2, axis=1)
    for p in pairs:
        kd = k[p] * jnp.exp(g_last[p] - gc[p])
        r = _dot3s(_split(kd.T), _split(v_new[p]))
        s_ref[p] = s_old[p] * jnp.exp(g_last[p]) + jnp.where(left, r[0:dk], r[dk:pw2])
        on = o[p] * lax.rsqrt(head_norm(o[p]) * (1.0 / GDN_DV) + EPS) * gn2
        b, pp = divmod(p, npair)
        o_ref[b, :, pp * pw2:(pp + 1) * pw2] = on * szb_ref[b, :, pp * pw2:(pp + 1) * pw2]

    @pl.when(c == pl.num_programs(0) - 1)
    def _():
        for p in pairs:
            b, pp = divmod(p, npair)
            sp = s_ref[p]
            sfin_ref[b, 2 * pp] = sp[:, 0:GDN_DV]
            sfin_ref[b, 2 * pp + 1] = sp[:, GDN_DV:2 * GDN_DV]


def _lane_params(a_log, dt_bias):
    alog_l = jnp.zeros((1, SMALL_W), F32).at[0, SMALL_A0:SMALL_A0 + GDN_HEADS].set(a_log)
    dtb_l = jnp.zeros((1, SMALL_W), F32).at[0, SMALL_A0:SMALL_A0 + GDN_HEADS].set(dt_bias)
    return alog_l, dtb_l


def _gdn_prompt(qkvb, small, szb, conv_w, conv_b, a_log, dt_bias, gnorm, batch):
    n = qkvb.shape[0]
    seq = n // batch
    nc = seq // GDN_CHUNK
    alog_l, dtb_l = _lane_params(a_log, dt_bias)
    row = lambda w: pl.BlockSpec((batch, GDN_CHUNK, w), lambda c: (0, c, 0))
    full = lambda a: pl.BlockSpec(a.shape, lambda c: (0,) * a.ndim)
    cb = conv_b.reshape(1, -1)
    gn = gnorm.reshape(1, -1)
    seq3 = lambda a: a.reshape(batch, seq, a.shape[-1])
    o, s_new = pl.pallas_call(
        _gdn_prompt_kernel,
        grid=(nc,),
        in_specs=[row(CONV_DIM), row(SMALL_W), row(GDN_WIDTH), full(conv_w), full(cb), full(alog_l), full(dtb_l), full(gn)],
        out_specs=[row(GDN_WIDTH),
                   pl.BlockSpec((batch, GDN_HEADS, GDN_DK, GDN_DV), lambda c: (0, 0, 0, 0))],
        out_shape=[jax.ShapeDtypeStruct((batch, seq, GDN_WIDTH), F32),
                   jax.ShapeDtypeStruct((batch, GDN_HEADS, GDN_DK, GDN_DV), F32)],
        scratch_shapes=[pltpu.VMEM((batch * (GDN_HEADS // 2), GDN_DK, 2 * GDN_DV), F32),
                        pltpu.VMEM((batch, 8, CONV_DIM), F32)],
        compiler_params=_cparams(("arbitrary",)),
        name="gdn_prompt",
    )(seq3(qkvb), seq3(small), seq3(szb), conv_w, cb, alog_l, dtb_l, gn)
    return o.reshape(n, GDN_WIDTH), s_new


def _prep_w_in(w_in, d_model):
    sizes = (NSA_WIDTH, 6 * NSA_KV_HEADS * HEAD_DIM, 3 * NSA_HEADS, NSA_WIDTH, CONV_DIM, GDN_HEADS, GDN_HEADS,
             GDN_WIDTH, 2 * d_model)
    pts = np.cumsum(np.array(sizes))[:-1].tolist()
    q_a, kv_a, g_a, z_a, qkv_b, a_b, b_b, z_b, gm = jnp.split(w_in, pts, axis=1)
    small = jnp.concatenate([g_a, a_b, b_b], axis=1)
    small = jnp.pad(small, ((0, 0), (0, SMALL_W - small.shape[1])))
    return jnp.concatenate([q_a, kv_a, z_a, qkv_b, z_b, gm, small], axis=1).astype(BF16)


def _kv_leaf(kv_t):
    b, _, t = kv_t.shape
    return kv_t.reshape(1, b, 2, NSA_KV_HEADS, HEAD_DIM, t).transpose(0, 1, 5, 2, 3, 4)


def _kv_rows_t(cache):
    n, r = cache.shape[:2]
    return cache.transpose(0, 2, 3, 4, 1).reshape(n, KV_ROW, r)


def _prompt_path(x, ada, lw, cw):
    (norm_g, w_bf, offs, conv_w, conv_b, a_log, dt_bias, gnorm, wa, wb, wo, final_g) = lw
    batch, seq, d = x.shape
    n = batch * seq
    x2 = x.reshape(n, d)
    ada3 = ada.reshape(batch, 1, 3 * d)
    tm = 256
    mod = lambda k: pl.BlockSpec((None, 1, d), lambda i: (i * tm // seq, 0, k))
    (q_aug, kvc_t, kvs_t, kvw_t, kvcb, ksel, vsel_t, kwin, vwin_t, sza, qkvb, szb, gms, small) = _inproj(
        x2, ada3, ada3, (mod(1), mod(0)), norm_g.reshape(1, d), w_bf, offs, tm, seq)
    kc_aug, vc_t = _compress_prompt(kvcb, *cw, batch)
    o_a = _nsa_prompt(q_aug, kc_aug, vc_t, ksel, vsel_t, kwin, vwin_t, small, sza, batch)
    o_b, s_new = _gdn_prompt(qkvb, small, szb, conv_w, conv_b, a_log, dt_bias, gnorm, batch)
    tmo = 512
    gate_spec = pl.BlockSpec((None, 1, d), lambda i: (i * tmo // seq, 0, 2))
    y = _outproj(x2, o_a, o_b, gms, ada3, gate_spec, wa, wb, wo, final_g.reshape(1, d), tmo)
    keep = min(WINDOW, seq)
    new_conv = qkvb.reshape(batch, seq, CONV_DIM)[None, :, seq - (CONV_W - 1):]
    return (y.reshape(batch, seq, d), _kv_leaf(kvc_t), _kv_leaf(kvs_t), _kv_leaf(kvw_t[:, :, seq - keep:]), new_conv,
            s_new[None])


def _page_fetch(pt_ref, cache_hbm, buf, sem, npages):
    def copy(seq, slot, j):
        c0 = pl.multiple_of(j * PAGE_SIZE, PAGE_SIZE)
        return pltpu.make_async_copy(cache_hbm.at[pt_ref[seq, j]], buf.at[slot, :, pl.ds(c0, PAGE_SIZE)], sem.at[slot])

    def start(seq, slot):
        for j in range(npages):
            copy(seq, slot, j).start()

    def wait(seq, slot):
        for j in range(npages):
            copy(seq, slot, j).wait()

    return start, wait


def _fetch_this_prefetch_next(start, wait):
    b = pl.program_id(0)
    nb = pl.num_programs(0)

    @pl.when(b == 0)
    def _():
        start(0, 0)

    @pl.when(b + 1 < nb)
    def _():
        start(b + 1, (b + 1) % 2)

    slot = b % 2
    wait(b, slot)
    return slot


def _compress_sample_kernel(pt_ref, cache_hbm, perm_ref, w1_ref, pe_ref, w2_ref, o_ref, buf, sem, xc_ref, *, npages):
    start, wait = _page_fetch(pt_ref, cache_hbm, buf, sem, npages)
    slot = _fetch_this_prefetch_next(start, wait)
    past = npages * PAGE_SIZE
    n16 = past // CMP_STRIDE
    gt = perm_ref.shape[0]
    cps = gt // CMP_STRIDE

    half = KV_ROW // 2
    kw = CMP_STRIDE * half

    def regroup(j, c):
        t0 = pl.multiple_of(j * gt, gt)
        x_t = buf[slot, :, pl.ds(t0, gt)].astype(BF16)
        y = _dot(perm_ref[...], x_t, _NT).astype(BF16)
        r0 = pl.multiple_of(j * cps, cps)
        for s in range(CMP_STRIDE):
            for kv in range(2):
                xc_ref[pl.ds(r0, cps), kv * kw + s * half:kv * kw + (s + 1) * half] = (
                    y[s * cps:(s + 1) * cps, kv * half:(kv + 1) * half])
        return c

    lax.fori_loop(0, past // gt, regroup, 0, unroll=min(8, past // gt))
    outs = []
    for kv in range(2):
        a = _dot(xc_ref[:, kv * kw:(kv + 1) * kw], w1_ref[kv])
        p0 = _dot(pe_ref[kv, 0], w1_ref[kv, :, 0:half]) + _dot(pe_ref[kv, 1], w1_ref[kv, :, half:2 * half])
        pre = a[:, 0:half] + pltpu.roll(a[:, half:2 * half], n16 - 1, 0) + p0[0:1]
        outs.append(_dot(_silu(pre).astype(BF16), w2_ref[kv]))
    o_ref[...] = jnp.concatenate(outs, axis=1).astype(BF16)


def _compress_weights_kv(pe_k, w1_k, w2_k, pe_v, w1_v, w2_v):
    r_cnt = CMP_BLOCK // CMP_STRIDE
    g = NSA_KV_HEADS
    eyeg = jnp.eye(g, dtype=F32)
    w1 = jnp.stack([w1_k, w1_v]).reshape(2, r_cnt, CMP_STRIDE, HEAD_DIM, CMP_HIDDEN)
    w1s = jnp.einsum("krsde,gG->ksgdrGe", w1, eyeg)
    w1s = w1s.reshape(2, CMP_STRIDE * g * HEAD_DIM, r_cnt * g * CMP_HIDDEN).astype(BF16)
    pe = jnp.stack([pe_k, pe_v]).reshape(2, r_cnt, CMP_STRIDE, 1, HEAD_DIM)
    pes = jnp.broadcast_to(pe, (2, r_cnt, CMP_STRIDE, g, HEAD_DIM)).reshape(2, r_cnt, 1, CMP_STRIDE * g * HEAD_DIM)
    pes = jnp.pad(pes, ((0, 0), (0, 0), (0, 7), (0, 0))).astype(BF16)
    w2s = jnp.einsum("ked,gG->kgeGd", jnp.stack([w2_k, w2_v]), eyeg).reshape(2, g * CMP_HIDDEN, g * HEAD_DIM).astype(BF16)
    return w1s, pes, w2s


def _compress_sample(cache_t, page_table, w1big, pebig, w2big):
    nseq, npages = page_table.shape
    past = npages * PAGE_SIZE
    n16 = past // CMP_STRIDE
    gt = 2 * PAGE_SIZE
    assert past % gt == 0 and gt // CMP_STRIDE == 16
    row = np.arange(gt)
    perm_t = jnp.asarray((row[:, None] % (gt // CMP_STRIDE)) * CMP_STRIDE + row[:, None] // (gt // CMP_STRIDE)
                         == row[None, :], dtype=BF16)
    grid_spec = pltpu.PrefetchScalarGridSpec(
        num_scalar_prefetch=1,
        grid=(nseq,),
        in_specs=[pl.BlockSpec(memory_space=pl.ANY),
                  pl.BlockSpec(perm_t.shape, lambda b, pt: (0, 0)),
                  pl.BlockSpec(w1big.shape, lambda b, pt: (0, 0, 0)),
                  pl.BlockSpec(pebig.shape, lambda b, pt: (0, 0, 0, 0)),
                  pl.BlockSpec(w2big.shape, lambda b, pt: (0, 0, 0))],
        out_specs=pl.BlockSpec((None, n16, KV_ROW), lambda b, pt: (b, 0, 0)),
        scratch_shapes=[pltpu.VMEM((2, KV_ROW, past), F32), pltpu.SemaphoreType.DMA((2,)),
                        pltpu.VMEM((n16, CMP_STRIDE * KV_ROW), BF16)],
    )
    return pl.pallas_call(
        functools.partial(_compress_sample_kernel, npages=npages),
        grid_spec=grid_spec,
        out_shape=jax.ShapeDtypeStruct((nseq, n16, KV_ROW), BF16),
        compiler_params=_cparams(("arbitrary",)),
        name="compress_sample",
    )(page_table, cache_t, perm_t, w1big, pebig, w2big)


def _nsa_sample_kernel(pt_ref, q_ref, kc_ref, cache_hbm, win_ref, nsel_ref, nwin_ref, nwint_ref, small_ref, sza_ref,
                       selmap_ref, onehot_ref, o_ref, wout_ref, buf, sem, *, npages, tn, kt):
    hd = HEAD_DIM
    rt = 8
    start, wait = _page_fetch(pt_ref, cache_hbm, buf, sem, npages)
    slot = _fetch_this_prefetch_next(start, wait)
    past = npages * PAGE_SIZE
    nbuf = win_ref.shape[1]
    n_cmp = kc_ref.shape[0]
    n_blk_lanes = selmap_ref.shape[1]
    q8 = q_ref[...].astype(F32)
    gates = jax.nn.sigmoid(small_ref[...])
    tau = lax.broadcasted_iota(jnp.int32, (rt, 1), 0) % tn
    t_col = past + tau
    t4 = jnp.concatenate([t_col] * NSA_GROUP, axis=0)
    rows = NSA_GROUP * rt
    newcol = lax.broadcasted_iota(jnp.int32, (1, SMALL_W), 1)
    new_dist = t4 - (past + newcol)
    new_ok = (newcol < tn) & (new_dist >= 0)
    zpad = jnp.zeros((SMALL_W - rt, hd), F32)
    groups = range(NSA_KV_HEADS)
    kcols = [slice(g * hd, (g + 1) * hd) for g in groups]
    vcols = [slice((NSA_KV_HEADS + g) * hd, (NSA_KV_HEADS + g + 1) * hd) for g in groups]

    def new_tile(ref, col):
        return jnp.concatenate([ref[:, col], zpad], axis=0).astype(BF16)

    def pos_cols(p0, n):
        pos = (p0 + lax.broadcasted_iota(jnp.int32, (n, 1), 0)).astype(F32)
        hi = jnp.floor(pos * (1.0 / POS_SPLIT))
        return _aug_cols(n, hi, pos - hi * POS_SPLIT)

    qgs, slopes, o_cs, imps = [], [], [], []
    for g in groups:
        kcol, vcol = kcols[g], vcols[g]
        qg = jnp.concatenate([q8[:, (g * NSA_GROUP + h) * AUG:(g * NSA_GROUP + h) * AUG + hd]
                              for h in range(NSA_GROUP)], axis=0).astype(BF16)
        slope = jnp.concatenate([jnp.full((rt, 1), _SLOPES[g * NSA_GROUP + h], F32) for h in range(NSA_GROUP)], axis=0)

        s = _dot(qg, kc_ref[:, kcol], _NT)
        end = lax.broadcasted_iota(jnp.int32, (1, n_cmp), 1) * CMP_STRIDE + (CMP_BLOCK - 1)
        dist = t4 - end
        mask = dist >= 0
        _, p = _softmax_block(s - slope * dist.astype(F32), mask)
        p = p / jnp.maximum(jnp.sum(p, axis=-1, keepdims=True), 1e-30)
        o_c = _dot(p.astype(BF16), kc_ref[:, vcol])
        psum = p[0:rt]
        for h in range(1, NSA_GROUP):
            psum = psum + p[h * rt:(h + 1) * rt]
        for lst, val in ((qgs, qg), (slopes, slope), (o_cs, o_c), (imps, _importance(psum, selmap_ref[...]))):
            lst.append(val)

    ncol = NSA_KV_HEADS * rt
    imp_pad = jnp.concatenate(imps + [jnp.zeros((SMALL_W - ncol, n_blk_lanes), F32)], axis=0)
    tq_row = past + lax.broadcasted_iota(jnp.int32, (1, SMALL_W), 1) % tn
    selm_all = _select_blocks_t(imp_pad.T, tq_row).T

    selms, q_augs, q_fulls = [], [], []
    for g in groups:
        selm = selm_all[g * rt:(g + 1) * rt]
        q_aug = jnp.concatenate([q8[:, (g * NSA_GROUP + h) * AUG:(g * NSA_GROUP + h + 1) * AUG]
                                 for h in range(NSA_GROUP)], axis=0)
        blk_bias = jnp.where(selm > 0.5, 0.0, MASKV)
        q_full = jnp.concatenate([q_aug, jnp.concatenate([blk_bias] * NSA_GROUP, axis=0)], axis=1).astype(BF16)
        for lst, val in ((selms, selm), (q_augs, q_aug), (q_fulls, q_full)):
            lst.append(val)

    bpt = kt // SEL_BLOCK
    carries = [_online_init(rows, hd) for _ in groups]
    for j in range(past // kt):
        k0 = j * kt
        for g in groups:
            k_t = buf[slot, kcols[g], k0:k0 + kt].astype(BF16)
            v_t = buf[slot, vcols[g], k0:k0 + kt].astype(BF16)
            q_rest = jnp.concatenate([q_fulls[g][:, hd:AUG], q_fulls[g][:, AUG + j * bpt:AUG + (j + 1) * bpt]], axis=1)
            s = _dot(q_fulls[g][:, 0:hd], k_t) + _dot(q_rest, onehot_ref[j])
            carries[g] = _flash_step(carries[g], s, v_t, _NT)

    outs = []
    for g in groups:
        kcol, vcol = kcols[g], vcols[g]
        qg, slope, o_c, selm, q_aug = qgs[g], slopes[g], o_cs[g], selms[g], q_augs[g]
        nb_new = past // SEL_BLOCK
        new_sel = jnp.concatenate([selm[:, nb_new:nb_new + 1]] * NSA_GROUP, axis=0) > 0.5
        kn = jnp.concatenate([jnp.concatenate([nsel_ref[:, kcol], zpad], axis=0), pos_cols(past, SMALL_W)], axis=1)
        s = _dot(q_aug.astype(BF16), kn.astype(BF16), _NT) + jnp.where(new_ok & new_sel, 0.0, MASKV)
        _, l, acc = _flash_step(carries[g], s, new_tile(nsel_ref, vcol))
        o_s = acc / jnp.maximum(l, 1e-30)

        s = _dot(qg, win_ref[kcol, :].astype(BF16))
        dist = t4 - (past - nbuf + lax.broadcasted_iota(jnp.int32, (1, nbuf), 1))
        mask = (dist >= 0) & (dist < WINDOW)
        carry = _online_update(_online_init(rows, hd), s - slope * dist.astype(F32), mask,
                               win_ref[vcol, :].astype(BF16), _NT)
        s = _dot(qg, new_tile(nwin_ref, kcol), _NT)
        _, l, acc = _online_update(carry, s - slope * new_dist.astype(F32), new_ok, new_tile(nwin_ref, vcol))
        o_w = acc / jnp.maximum(l, 1e-30)

        for h in range(NSA_GROUP):
            c = 3 * (g * NSA_GROUP + h)
            r = slice(h * rt, (h + 1) * rt)
            outs.append(gates[:, c:c + 1] * o_c[r] + gates[:, c + 1:c + 2] * o_s[r] + gates[:, c + 2:c + 3] * o_w[r])
    o_ref[...] = jnp.concatenate(outs, axis=1) * sza_ref[...]
    wout_ref[...] = jnp.concatenate([win_ref[:, tn:nbuf], nwint_ref[...]], axis=1)


def _nsa_sample(qs, kc, cache_t, win_t, kvs, kvw, small, sza, page_table, tn):
    nseq, npages = page_table.shape
    past = npages * PAGE_SIZE
    nbuf = win_t.shape[2]
    assert nbuf == WINDOW and 8 % tn == 0 and tn <= SEL_BLOCK
    n_cmp = kc.shape[1]
    n_blk = -(-(past + tn) // SEL_BLOCK)
    n_blk_lanes = -(-n_blk // 128) * 128
    selmap = _sel_map(n_cmp, n_blk, n_blk_lanes)
    kt = min(4096, past)
    ntile, bpt = past // kt, kt // SEL_BLOCK
    key = np.arange(past).reshape(ntile, 1, kt)
    pos_rows = np.zeros((ntile, AUG - HEAD_DIM, kt), np.float32)
    pos_rows[:, 0], pos_rows[:, 1] = key[:, 0] // POS_SPLIT, key[:, 0] % POS_SPLIT
    member = (key % kt // SEL_BLOCK == np.arange(bpt)[None, :, None]).astype(np.float32)
    onehot_t = jnp.asarray(np.concatenate([pos_rows, member], axis=1), dtype=BF16)
    seq3 = lambda r, w: pl.BlockSpec((None, r, w), lambda b, pt: (b, 0, 0))
    grid_spec = pltpu.PrefetchScalarGridSpec(
        num_scalar_prefetch=1,
        grid=(nseq,),
        in_specs=[seq3(8, NSA_HEADS * AUG), seq3(n_cmp, KV_ROW), pl.BlockSpec(memory_space=pl.ANY), seq3(KV_ROW, nbuf),
                  seq3(8, KV_ROW), seq3(8, KV_ROW), seq3(KV_ROW, tn), seq3(8, SMALL_W), seq3(8, NSA_WIDTH),
                  pl.BlockSpec(selmap.shape, lambda b, pt: (0, 0)),
                  pl.BlockSpec(onehot_t.shape, lambda b, pt: (0, 0, 0))],
        out_specs=[seq3(8, NSA_WIDTH), seq3(KV_ROW, nbuf)],
        scratch_shapes=[pltpu.VMEM((2, KV_ROW, past), F32), pltpu.SemaphoreType.DMA((2,))],
    )
    rep8 = lambda a: jnp.concatenate([a.reshape(nseq, tn, a.shape[-1])] * (8 // tn), axis=1)
    o8, win_new_t = pl.pallas_call(
        functools.partial(_nsa_sample_kernel, npages=npages, tn=tn, kt=kt),
        grid_spec=grid_spec,
        out_shape=[jax.ShapeDtypeStruct((nseq, 8, NSA_WIDTH), F32), jax.ShapeDtypeStruct((nseq, KV_ROW, nbuf), F32)],
        compiler_params=_cparams(("arbitrary",)),
        name="nsa_sample",
    )(page_table, rep8(qs), kc, cache_t, win_t, rep8(kvs), rep8(kvw), kvw.transpose(0, 2, 1), rep8(small), rep8(sza),
      selmap, onehot_t)
    return o8[:, :tn].reshape(nseq * tn, NSA_WIDTH), win_new_t


def _gdn_sample_kernel(eq_ref, ek_ref, ev_ref, cwq_ref, cwk_ref, cwv_ref, cbq_ref, cbk_ref, cbv_ref, small_ref,
                       alog_ref, dtb_ref, szb_ref, gn_ref, s_ref, o_ref, so_ref, qs_ref, ks_ref):
    h = pl.program_id(0)
    tn = o_ref.shape[0]
    dk, dv, nseq = so_ref.shape

    def conv(e_ref, cw_ref, cb_ref, t):
        y = cb_ref[...]
        for i in range(CONV_W):
            y = y + e_ref[t + i] * cw_ref[:, i:i + 1]
        return _silu(y)

    so_ref[...] = s_ref[...]
    neg_rate = -jnp.exp(alog_ref[...])
    for t in range(tn):
        q = conv(eq_ref, cwq_ref, cbq_ref, t)
        k = conv(ek_ref, cwk_ref, cbk_ref, t)
        v = conv(ev_ref, cwv_ref, cbv_ref, t)
        qs_ref[...] = q * lax.rsqrt(jnp.sum(q * q, axis=0, keepdims=True) + EPS) * (dk ** -0.5)
        ks_ref[...] = k * lax.rsqrt(jnp.sum(k * k, axis=0, keepdims=True) + EPS)
        a_in = small_ref[t, pl.ds(SMALL_A0 + h, 1), :]
        b_in = small_ref[t, pl.ds(SMALL_B0 + h, 1), :]
        decay = jnp.exp(neg_rate * _softplus(a_in + dtb_ref[...]))
        beta = jax.nn.sigmoid(b_in)

        def ks_step(i, acc):
            return acc + ks_ref[pl.ds(i, 1), :] * so_ref[i]

        k_s = lax.fori_loop(0, dk, ks_step, jnp.zeros((dv, nseq), F32), unroll=8)
        delta = beta * (v - decay * k_s)

        def upd_step(i, acc):
            s_new = decay * so_ref[i] + ks_ref[pl.ds(i, 1), :] * delta
            so_ref[i] = s_new
            return acc + qs_ref[pl.ds(i, 1), :] * s_new

        o = lax.fori_loop(0, dk, upd_step, jnp.zeros((dv, nseq), F32), unroll=8)
        o = o * lax.rsqrt(jnp.mean(o * o, axis=0, keepdims=True) + EPS) * gn_ref[...]
        o_ref[t] = o * szb_ref[t]


def _gdn_sample(qkvb, small, szb, state_conv, state_gdn, conv_w, conv_b, a_log, dt_bias, gnorm, tn):
    nseq = state_gdn.shape[0]
    ext = jnp.concatenate([state_conv, qkvb.reshape(nseq, tn, CONV_DIM)], axis=1)
    ext_t = ext.transpose(1, 2, 0)
    small_t = small.reshape(nseq, tn, SMALL_W).transpose(1, 2, 0)
    szb_t = szb.reshape(nseq, tn, GDN_WIDTH).transpose(1, 2, 0)
    s_t = state_gdn.transpose(1, 2, 3, 0)
    cw_t = conv_w.T
    cb_t = conv_b.reshape(-1, 1)
    alog_b = jnp.broadcast_to(a_log[:, None, None], (GDN_HEADS, 1, nseq))
    dtb_b = jnp.broadcast_to(dt_bias[:, None, None], (GDN_HEADS, 1, nseq))
    gn = gnorm.reshape(-1, 1)
    nqk = GDN_QK_WIDTH // GDN_DK
    chan = lambda off: pl.BlockSpec((CONV_W - 1 + tn, GDN_DK, nseq), lambda h: (0, off + h, 0))
    cwb = lambda off: pl.BlockSpec((GDN_DK, CONV_W), lambda h: (off + h, 0))
    cbb = lambda off: pl.BlockSpec((GDN_DK, 1), lambda h: (off + h, 0))
    perhead = pl.BlockSpec((None, 1, nseq), lambda h: (h, 0, 0))
    o_t, s_new = pl.pallas_call(
        _gdn_sample_kernel,
        grid=(GDN_HEADS,),
        in_specs=[chan(0), chan(nqk), chan(2 * nqk), cwb(0), cwb(nqk), cwb(2 * nqk), cbb(0), cbb(nqk), cbb(2 * nqk),
                  pl.BlockSpec((tn, SMALL_W, nseq), lambda h: (0, 0, 0)), perhead, perhead,
                  pl.BlockSpec((tn, GDN_DV, nseq), lambda h: (0, h, 0)),
                  pl.BlockSpec((GDN_DV, 1), lambda h: (0, 0)),
                  pl.BlockSpec((None, GDN_DK, GDN_DV, nseq), lambda h: (h, 0, 0, 0))],
        out_specs=[pl.BlockSpec((tn, GDN_DV, nseq), lambda h: (0, h, 0)),
                   pl.BlockSpec((None, GDN_DK, GDN_DV, nseq), lambda h: (h, 0, 0, 0))],
        out_shape=[jax.ShapeDtypeStruct((tn, GDN_WIDTH, nseq), F32),
                   jax.ShapeDtypeStruct((GDN_HEADS, GDN_DK, GDN_DV, nseq), F32)],
        scratch_shapes=[pltpu.VMEM((GDN_DK, nseq), F32), pltpu.VMEM((GDN_DK, nseq), F32)],
        compiler_params=_cparams(("parallel",)),
        name="gdn_sample",
    )(ext_t, ext_t, ext_t, cw_t, cw_t, cw_t, cb_t, cb_t, cb_t, small_t, alog_b, dtb_b, szb_t, gn, s_t)
    o_b = o_t.transpose(2, 0, 1).reshape(nseq * tn, GDN_WIDTH)
    return o_b, ext[:, tn:], s_new.transpose(3, 0, 1, 2)


def _sample_path(x, ada, lw, cw, cache_cmp, cache_sel, cache_win, state_conv, state_gdn, page_table):
    (norm_g, w_bf, offs, conv_w, conv_b, a_log, dt_bias, gnorm, wa, wb, wo, final_g) = lw
    nseq, tn, d = x.shape
    n = nseq * tn
    x2 = x.reshape(n, d)
    ada_rows = jnp.repeat(ada, tn, axis=0)
    tm = min(256, n)
    mod = lambda k: pl.BlockSpec((tm, d), lambda i: (i, k))
    (qs, kvc_t, kvs_t, kvw_t, _, _, _, _, _, sza, qkvb, szb, gms, small) = _inproj(
        x2, ada_rows, ada_rows, (mod(1), mod(0)), norm_g.reshape(1, d), w_bf, offs, tm, n)
    rows = lambda a_t: a_t.reshape(KV_ROW, nseq, tn).transpose(1, 2, 0)
    kvc, kvs, kvw = rows(kvc_t), rows(kvs_t), rows(kvw_t)
    kc = _compress_sample(_kv_rows_t(cache_cmp), page_table, *cw)
    o_a, win_new_t = _nsa_sample(qs, kc, _kv_rows_t(cache_sel), _kv_rows_t(cache_win), kvs, kvw, small, sza, page_table, tn)
    o_b, conv_new, s_new = _gdn_sample(qkvb, small, szb, state_conv, state_gdn, conv_w, conv_b, a_log, dt_bias, gnorm, tn)
    y = _outproj(x2, o_a, o_b, gms, ada_rows, mod(2), wa, wb, wo, final_g.reshape(1, d), tm)
    kvshape = (1, nseq, tn, 2, NSA_KV_HEADS, HEAD_DIM)
    return (y.reshape(nseq, tn, d), kvc.reshape(kvshape), kvs.reshape(kvshape), _kv_leaf(win_new_t),
            conv_new[None], s_new[None])


def kernel(x_prompt, x_sample, cache_cmp_kv, cache_sel_kv, cache_win_kv, state_conv, state_gdn, page_table, c_prompt, c_sample, norm_g, w_ada, b_ada, w_in, cmp_pe_k, cmp_w1_k, cmp_w2_k, cmp_pe_v, cmp_w1_v, cmp_w2_v, conv_w, conv_b, gdn_a_log, gdn_dt_bias, gdn_norm_g, w_o_nsa, w_o_gdn, w_out, final_g):
    assert norm_g.shape[0] == 1, "single trunk layer"
    d = x_prompt.shape[-1]
    l = 0
    offs, _ = _seg_offsets(d)
    w_bf = _prep_w_in(w_in[l], d)
    ada = _ada(jnp.concatenate([c_prompt, c_sample], axis=0), w_ada[l].astype(BF16), b_ada[l])
    cw = _compress_weights(cmp_pe_k[l], cmp_w1_k[l], cmp_w2_k[l], cmp_pe_v[l], cmp_w1_v[l], cmp_w2_v[l])
    lw = (norm_g[l], w_bf, offs, conv_w[l], conv_b[l], gdn_a_log[l], gdn_dt_bias[l], gdn_norm_g[l],
          w_o_nsa[l].astype(BF16), w_o_gdn[l].astype(BF16), w_out[l].astype(BF16), final_g)
    nb = c_prompt.shape[0]
    yp, cmp_p, sel_p, win_p, conv_p, gdn_p = _prompt_path(x_prompt, ada[:nb], lw, cw)
    ys, cmp_s, sel_s, win_s, conv_s, gdn_s = _sample_path(
        x_sample, ada[nb:], lw,
        _compress_weights_kv(cmp_pe_k[l], cmp_w1_k[l], cmp_w2_k[l], cmp_pe_v[l], cmp_w1_v[l], cmp_w2_v[l]),
        cache_cmp_kv[l], cache_sel_kv[l], cache_win_kv[l], state_conv[l], state_gdn[l],
        page_table)
    return (yp, ys, cmp_p, sel_p, win_p, conv_p, gdn_p, cmp_s, sel_s, win_s, conv_s, gdn_s)
```

```python
import functools
import math

import numpy as np
import jax
import jax.numpy as jnp
from jax import lax
from jax.experimental import pallas as pl
from jax.experimental.pallas import tpu as pltpu

F32 = jnp.float32
BF16 = jnp.bfloat16

NSA_HEADS = 8
NSA_KV_HEADS = 2
NSA_GROUP = NSA_HEADS // NSA_KV_HEADS
HEAD_DIM = 64
CMP_BLOCK = 32
CMP_STRIDE = 16
CMP_HIDDEN = 64
SEL_BLOCK = 64
N_SEL = 16
WINDOW = 512
Q_BLOCK = 128
GDN_HEADS = 8
GDN_DK = 64
GDN_DV = 64
CONV_W = 4
GDN_CHUNK = 64
PAGE_SIZE = 128
EPS = 1e-6

NSA_WIDTH = NSA_HEADS * HEAD_DIM
KV_ROW = 2 * NSA_KV_HEADS * HEAD_DIM
GDN_QK_WIDTH = GDN_HEADS * GDN_DK
GDN_WIDTH = GDN_HEADS * GDN_DV
CONV_DIM = 2 * GDN_QK_WIDTH + GDN_WIDTH
SMALL_W = 128
SMALL_A0 = 3 * NSA_HEADS
SMALL_B0 = SMALL_A0 + GDN_HEADS

NEG = -1e30
MASKV = -(2.0 ** 100)
AUG = 128
POS_SPLIT = 128
VMEM_LIMIT = 56 * 1024 * 1024

_NT = (((1,), (1,)), ((), ()))
_NN = (((1,), (0,)), ((), ()))


def _alibi_slopes():
    h = np.arange(1, NSA_HEADS + 1, dtype=np.float32)
    return [float(v) for v in np.power(np.float32(2.0), -np.float32(8.0) * h / np.float32(NSA_HEADS))]


_SLOPES = _alibi_slopes()


def _dot(a, b, dims=_NN):
    return lax.dot_general(a, b, dims, preferred_element_type=F32)


def _split(a):
    hi = a.astype(BF16)
    lo = (a - hi.astype(F32)).astype(BF16)
    return hi, lo


def _dot3s(a_split, b_split, dims=_NN):
    (ah, al), (bh, bl) = a_split, b_split
    return _dot(ah, bh, dims) + _dot(ah, bl, dims) + _dot(al, bh, dims)


def _silu(x):
    return x * jax.nn.sigmoid(x)


def _cparams(sem, flags=None):
    return pltpu.CompilerParams(dimension_semantics=sem, vmem_limit_bytes=VMEM_LIMIT, flags=flags)


def _ada_kernel(c_ref, w_ref, b_ref, o_ref):
    sc = _silu(c_ref[...]).astype(BF16)
    o_ref[...] = _dot(sc, w_ref[...]) + b_ref[...]


def _ada(c, w_bf, b):
    n, d = c.shape
    n_pad = -(-n // 8) * 8
    c = jnp.pad(c, ((0, n_pad - n), (0, 0)))
    out = pl.pallas_call(
        _ada_kernel,
        out_shape=jax.ShapeDtypeStruct((n_pad, w_bf.shape[1]), F32),
        name="ada",
    )(c, w_bf, b.reshape(1, -1))
    return out[:n]


_SEG = (("q", NSA_WIDTH), ("kvc", KV_ROW), ("kvs", KV_ROW), ("kvw", KV_ROW), ("za", NSA_WIDTH),
        ("qkvb", CONV_DIM), ("zb", GDN_WIDTH), ("gm", None), ("small", SMALL_W))


def _seg_offsets(d_model):
    offs, c = {}, 0
    for name, n in _SEG:
        n = 2 * d_model if n is None else n
        offs[name] = (c, n)
        c += n
    return offs, c


def _aug_cols(rows, c0, c1):
    lane = lax.broadcasted_iota(jnp.int32, (rows, AUG - HEAD_DIM), 1)
    return jnp.where(lane == 0, c0, jnp.where(lane == 1, c1, 0.0)).astype(F32)


def _aug_keys(kv, pos):
    hi = jnp.floor(pos * (1.0 / POS_SPLIT))
    cols = _aug_cols(kv.shape[0], hi, pos - hi * POS_SPLIT)
    parts = []
    for g in range(NSA_KV_HEADS):
        parts += [kv[:, g * HEAD_DIM:(g + 1) * HEAD_DIM], cols]
    k_aug = jnp.concatenate(parts, axis=1).astype(BF16)
    kv_t = kv.T
    return k_aug, kv_t[NSA_KV_HEADS * HEAD_DIM:].astype(BF16), kv_t


def _inproj_kernel(x_ref, scale_ref, shift_ref, ng_ref, w_ref,
                   q_ref, kvct_ref, kvst_ref, kvwt_ref, kvcb_ref, ksel_ref, vselt_ref, kwin_ref, vwint_ref,
                   sza_ref, qkvb_ref, szb_ref, gms_ref, small_ref, *, offs, seq):
    x = x_ref[...]
    tm = x.shape[0]
    y = x * lax.rsqrt(jnp.mean(x * x, axis=-1, keepdims=True) + EPS) * ng_ref[...]
    h = y * (1.0 + scale_ref[...]) + shift_ref[...]
    hb = h.astype(BF16)

    def seg(name):
        c0, n = offs[name]
        return _dot(hb, w_ref[:, c0:c0 + n])

    qv = seg("q") * (HEAD_DIM ** -0.5)
    parts = []
    for hh in range(NSA_HEADS):
        parts += [qv[:, hh * HEAD_DIM:(hh + 1) * HEAD_DIM], _aug_cols(tm, _SLOPES[hh] * POS_SPLIT, _SLOPES[hh])]
    q_ref[...] = jnp.concatenate(parts, axis=1).astype(BF16)

    pos = ((pl.program_id(0) * tm) % seq + lax.broadcasted_iota(jnp.int32, (tm, 1), 0)).astype(F32)
    v = seg("kvc")
    kvct_ref[...] = v.T
    kvcb_ref[...] = v.astype(BF16)
    for name, f_ref, k_ref, vt_ref in (("kvs", kvst_ref, ksel_ref, vselt_ref), ("kvw", kvwt_ref, kwin_ref, vwint_ref)):
        k_ref[...], vt_ref[...], f_ref[...] = _aug_keys(seg(name), pos)
    sza_ref[...] = _silu(seg("za"))
    qkvb_ref[...] = seg("qkvb")
    szb_ref[...] = _silu(seg("zb"))
    gms_ref[...] = jax.nn.sigmoid(seg("gm"))
    small_ref[...] = seg("small")


def _inproj(x2, scale_arr, shift_arr, mod_specs, ng, w_bf, offs, tm, seq):
    n, d = x2.shape
    wtot = w_bf.shape[1]
    nbatch = n // seq
    spt = seq // tm
    vw = NSA_KV_HEADS * HEAD_DIM
    row = lambda w: pl.BlockSpec((tm, w), lambda i: (i, 0))
    rows = lambda w, dt: (jax.ShapeDtypeStruct((n, w), dt), row(w))
    tr = lambda w, dt: (jax.ShapeDtypeStruct((nbatch, w, seq), dt),
                        pl.BlockSpec((None, w, tm), lambda i: (i // spt, 0, i % spt)))
    vt = tr(vw, BF16)
    kvt = tr(KV_ROW, F32)
    outs = [rows(NSA_HEADS * AUG, BF16), kvt, kvt, kvt, rows(KV_ROW, BF16),
            rows(NSA_KV_HEADS * AUG, BF16), vt, rows(NSA_KV_HEADS * AUG, BF16), vt,
            rows(offs["za"][1], F32), rows(offs["qkvb"][1], F32), rows(offs["zb"][1], F32), rows(offs["gm"][1], F32),
            rows(offs["small"][1], F32)]
    out_shape = [o[0] for o in outs]
    out_specs = [o[1] for o in outs]
    return pl.pallas_call(
        functools.partial(_inproj_kernel, offs=offs, seq=seq),
        grid=(n // tm,),
        in_specs=[row(d), mod_specs[0], mod_specs[1],
                  pl.BlockSpec((1, d), lambda i: (0, 0)),
                  pl.BlockSpec((d, wtot), lambda i: (0, 0))],
        out_specs=out_specs,
        out_shape=out_shape,
        compiler_params=_cparams(("parallel",)),
        name="inproj",
    )(x2, scale_arr, shift_arr, ng, w_bf)


def _outproj_kernel(x_ref, oa_ref, ob_ref, gms_ref, gate_ref, wa_ref, wb_ref, wo_ref, fg_ref, y_ref, *, d):
    ma = _dot(oa_ref[...].astype(BF16), wa_ref[...])
    mb = _dot(ob_ref[...].astype(BF16), wb_ref[...])
    m = gms_ref[:, 0:d] * ma + gms_ref[:, d:2 * d] * mb
    y = x_ref[...] + gate_ref[...] * _dot(m.astype(BF16), wo_ref[...])
    y_ref[...] = y * lax.rsqrt(jnp.mean(y * y, axis=-1, keepdims=True) + EPS) * fg_ref[...]


def _outproj(x2, oa, ob, gms, gate_arr, gate_spec, wa, wb, wo, fg, tm):
    n, d = x2.shape
    row = lambda w: pl.BlockSpec((tm, w), lambda i: (i, 0))
    full = lambda a: pl.BlockSpec(a.shape, lambda i: (0, 0))
    return pl.pallas_call(
        functools.partial(_outproj_kernel, d=d),
        grid=(n // tm,),
        in_specs=[row(d), row(oa.shape[1]), row(ob.shape[1]), row(2 * d), gate_spec,
                  full(wa), full(wb), full(wo), full(fg)],
        out_specs=row(d),
        out_shape=jax.ShapeDtypeStruct((n, d), F32),
        compiler_params=_cparams(("parallel",)),
        name="outproj",
    )(x2, oa, ob, gms, gate_arr, wa, wb, wo, fg)


def _compress_math(x, w1_ref, pe_ref, w2_ref):
    n16 = x.shape[0]
    a0 = _dot(x, w1_ref[0])
    a1 = _dot(x, w1_ref[1])
    p0 = _dot(pe_ref[0], w1_ref[0]) + _dot(pe_ref[1], w1_ref[1])
    pre = a0 + pltpu.roll(a1, n16 - 1, 0) + p0[0:1]
    return _dot(_silu(pre).astype(BF16), w2_ref[...])


def _compress_kernel(x_ref, w1_ref, pe_ref, w2_ref, k_ref, vt_ref):
    kv = _compress_math(x_ref[...], w1_ref, pe_ref, w2_ref)
    n16 = kv.shape[0]
    end = (lax.broadcasted_iota(jnp.int32, (n16, 1), 0) * CMP_STRIDE + (CMP_BLOCK - 1)).astype(F32)
    k_ref[...], vt_ref[...], _ = _aug_keys(kv, end)


def _compress_prompt(kvcb, w1big, pebig, w2big, batch):
    n = kvcb.shape[0]
    n16 = n // batch // CMP_STRIDE
    x = kvcb.reshape(batch, n16, CMP_STRIDE * KV_ROW)
    vw = NSA_KV_HEADS * HEAD_DIM
    return pl.pallas_call(
        _compress_kernel,
        grid=(batch,),
        in_specs=[pl.BlockSpec((None, n16, CMP_STRIDE * KV_ROW), lambda b: (b, 0, 0)),
                  pl.BlockSpec(w1big.shape, lambda b: (0, 0, 0)),
                  pl.BlockSpec(pebig.shape, lambda b: (0, 0, 0)),
                  pl.BlockSpec(w2big.shape, lambda b: (0, 0))],
        out_specs=[pl.BlockSpec((None, n16, NSA_KV_HEADS * AUG), lambda b: (b, 0, 0)),
                   pl.BlockSpec((None, vw, n16), lambda b: (b, 0, 0))],
        out_shape=[jax.ShapeDtypeStruct((batch, n16, NSA_KV_HEADS * AUG), BF16),
                   jax.ShapeDtypeStruct((batch, vw, n16), BF16)],
        compiler_params=_cparams(("parallel",)),
        name="compress_prompt",
    )(x, w1big, pebig, w2big)


def _compress_weights(pe_k, w1_k, w2_k, pe_v, w1_v, w2_v):
    r_cnt = CMP_BLOCK // CMP_STRIDE
    g = NSA_KV_HEADS
    eye2 = jnp.eye(2, dtype=F32)
    eyeg = jnp.eye(g, dtype=F32)
    w1 = jnp.stack([w1_k, w1_v]).reshape(2, r_cnt, CMP_STRIDE, HEAD_DIM, CMP_HIDDEN)
    w1big = jnp.einsum("krsde,kK,gG->rskgdKGe", w1, eye2, eyeg)
    w1big = w1big.reshape(r_cnt, CMP_STRIDE * KV_ROW, 2 * g * CMP_HIDDEN).astype(BF16)
    pe = jnp.stack([pe_k, pe_v]).reshape(2, r_cnt, CMP_STRIDE, HEAD_DIM)
    pebig = jnp.broadcast_to(pe.transpose(1, 2, 0, 3)[:, :, :, None, :], (r_cnt, CMP_STRIDE, 2, g, HEAD_DIM))
    pebig = pebig.reshape(r_cnt, 1, CMP_STRIDE * KV_ROW)
    pebig = jnp.pad(pebig, ((0, 0), (0, 7), (0, 0))).astype(BF16)
    w2 = jnp.stack([w2_k, w2_v])
    w2big = jnp.einsum("ked,kK,gG->kgeKGd", w2, eye2, eyeg).reshape(2 * g * CMP_HIDDEN, KV_ROW).astype(BF16)
    return w1big, pebig, w2big


def _sel_map(n_cmp_rows, n_blk, n_blk_pad):
    i = np.arange(n_cmp_rows)[:, None] * CMP_STRIDE
    j = np.arange(n_blk_pad)[None, :] * SEL_BLOCK
    ov = np.minimum(i + CMP_BLOCK, j + SEL_BLOCK) - np.maximum(i, j)
    m = np.clip(ov, 0, None).astype(np.float32) / np.float32(CMP_BLOCK)
    m[:, n_blk:] = 0.0
    return jnp.asarray(m, dtype=BF16)


def _softmax_block(s, mask):
    s = jnp.where(mask, s, NEG)
    m = jnp.max(s, axis=-1, keepdims=True)
    p = jnp.where(mask, jnp.exp(s - m), 0.0)
    return m, p


def _online_update(carry, s, mask, v_bf, v_dims=_NN):
    m, l, acc = carry
    s = jnp.where(mask, s, NEG)
    m_new = jnp.maximum(m, jnp.max(s, axis=-1, keepdims=True))
    alpha = jnp.exp(m - m_new)
    p = jnp.where(mask, jnp.exp(s - m_new), 0.0)
    l = alpha * l + jnp.sum(p, axis=-1, keepdims=True)
    acc = alpha * acc + _dot(p.astype(BF16), v_bf, v_dims)
    return m_new, l, acc


def _flash_step(carry, s, v_bf, v_dims=_NN):
    m, l, acc = carry
    m_new = jnp.maximum(m, jnp.max(s, axis=-1, keepdims=True))
    alpha = jnp.exp(m - m_new)
    p = jnp.exp(s - m_new)
    return (m_new, alpha * l + jnp.sum(p, axis=-1, keepdims=True),
            alpha * acc + _dot(p.astype(BF16), v_bf, v_dims))


def _online_init(rows, hd):
    return (jnp.full((rows, 1), NEG, F32), jnp.zeros((rows, 1), F32), jnp.zeros((rows, hd), F32))


def _importance(p_sum, selmap):
    ph, pl_ = _split(p_sum)
    pl2 = (p_sum - ph.astype(F32) - pl_.astype(F32)).astype(BF16)
    return _dot(ph, selmap) + _dot(pl_, selmap) + _dot(pl2, selmap)


def _flash_step_t(carry, s_t, v_t):
    m, l, acc = carry
    m_new = jnp.maximum(m, jnp.max(s_t, axis=0, keepdims=True))
    alpha = jnp.exp(m - m_new)
    p = jnp.exp(s_t - m_new)
    l = alpha * l + jnp.sum(p, axis=0, keepdims=True)
    acc = alpha * acc + _dot(v_t, p.astype(BF16))
    return m_new, l, acc


def _flash_init_t(cols, dv):
    return (jnp.full((1, cols), NEG, F32), jnp.zeros((1, cols), F32), jnp.zeros((dv, cols), F32))


def _select_blocks_t(imp_t, tq):
    nb, nq = imp_t.shape
    jj = lax.broadcasted_iota(jnp.int32, (nb, nq), 0)
    cur = tq // SEL_BLOCK
    forced = (jj == 0) | (jj == cur) | (jj == cur - 1)
    jf = jj.astype(F32)
    score = jnp.where(forced, NEG, jnp.where(jj <= cur, imp_t, NEG))
    selm = jnp.where(forced, 1.0, 0.0).astype(F32)
    for _ in range(N_SEL - 3):
        m = jnp.max(score, axis=0, keepdims=True)
        idx = jnp.min(jnp.where(score == m, jf, float(nb)), axis=0, keepdims=True)
        hit = jnp.where(m > 0.5 * NEG, idx, -1.0) == jf
        selm = jnp.where(hit, 1.0, selm)
        score = jnp.where(hit, NEG, score)
    return selm


def _nsa_prompt_kernel(*refs, kt, nblk):
    for blk in range(nblk):
        _nsa_prompt_block(blk, nblk, *refs, kt=kt)


def _nsa_prompt_block(blk, nblk, q_ref, kc_ref, vct_ref, ksel_ref, vselt_ref, kwin_ref, vwint_ref, onehot_ref, small_ref,
                      sza_ref, selmapt_ref, o_ref, flags_ref, m_ref, l_ref, acc_ref, *, kt):
    qb = Q_BLOCK
    hd = HEAD_DIM
    rows = slice(blk * qb, (blk + 1) * qb)
    q0 = (pl.program_id(1) * nblk + blk) * qb
    n_cmp = kc_ref.shape[0]
    cols = NSA_GROUP * qb
    tq = q0 + lax.broadcasted_iota(jnp.int32, (1, qb), 1)
    gates_t = jax.nn.sigmoid(small_ref[rows, :]).T
    wlen = WINDOW + qb
    w0 = pl.multiple_of(jnp.maximum(q0 - WINDOW, 0), qb)
    tile4 = lambda a: jnp.concatenate([a] * NSA_GROUP, axis=1)

    def keymask(pos_col, lo):
        dist = tq - pos_col
        ok = (dist >= 0) if lo is None else ((dist >= 0) & (dist < lo))
        return tile4(jnp.where(ok, 0.0, MASKV).astype(F32))

    cmp_bias = keymask(lax.broadcasted_iota(jnp.int32, (n_cmp, 1), 0) * CMP_STRIDE + (CMP_BLOCK - 1), None)
    win_bias = keymask(w0 + lax.broadcasted_iota(jnp.int32, (wlen, 1), 0), WINDOW)
    jlast = q0 // kt
    klast = pl.multiple_of(jlast * kt, kt)
    diag_bias = keymask(klast + lax.broadcasted_iota(jnp.int32, (kt, 1), 0), None)

    groups = range(NSA_KV_HEADS)
    kcols = [slice(g * AUG, (g + 1) * AUG) for g in groups]
    vrows = [slice(g * hd, (g + 1) * hd) for g in groups]
    q_heads = [[q_ref[rows, (g * NSA_GROUP + h) * AUG:(g * NSA_GROUP + h + 1) * AUG] for h in range(NSA_GROUP)]
               for g in groups]
    q_cat = [jnp.concatenate(q_heads[g], axis=0) for g in groups]
    finish = lambda carry: carry[2] * (1.0 / jnp.maximum(carry[1], 1e-30))

    o_w = [finish(_flash_step_t(_flash_init_t(cols, hd), _dot(kwin_ref[pl.ds(w0, wlen), kcols[g]], q_cat[g], _NT) + win_bias,
                                vwint_ref[vrows[g], pl.ds(w0, wlen)])) for g in groups]

    o_c, q_full, psums = [], [], []
    sm_t = selmapt_ref[...]
    for g in groups:
        s_t = _dot(kc_ref[:, kcols[g]], q_cat[g], _NT) + cmp_bias
        m = jnp.maximum(jnp.max(s_t, axis=0, keepdims=True), NEG)
        p = jnp.exp(s_t - m)
        p = p * (1.0 / jnp.maximum(jnp.sum(p, axis=0, keepdims=True), 1e-30))
        o_c.append(_dot(vct_ref[vrows[g], :], p.astype(BF16)))
        psum = p[:, 0:qb]
        for h in range(1, NSA_GROUP):
            psum = psum + p[:, h * qb:(h + 1) * qb]
        psums.append(psum)
    psum = jnp.concatenate(psums, axis=1)
    ph, pl_ = _split(psum)
    pl2 = (psum - ph.astype(F32) - pl_.astype(F32)).astype(BF16)
    imp_t = _dot(sm_t, ph) + _dot(sm_t, pl_) + _dot(sm_t, pl2)
    selm_t = _select_blocks_t(imp_t, jnp.concatenate([tq] * NSA_KV_HEADS, axis=1))
    for g in groups:
        selm = selm_t[:, g * qb:(g + 1) * qb].T
        blk_bias = jnp.where(selm > 0.5, 0.0, MASKV).astype(BF16)
        q_full.append(jnp.concatenate([jnp.concatenate([qh, blk_bias], axis=1) for qh in q_heads[g]], axis=0))
        blk_any = jnp.max(selm, axis=0, keepdims=True)
        bpt = kt // SEL_BLOCK
        for j in range(flags_ref.shape[1]):
            flags_ref[g, j] = (jnp.max(blk_any[:, j * bpt:(j + 1) * bpt]) > 0.5).astype(jnp.int32)
        for ref, val in zip((m_ref, l_ref, acc_ref), _flash_init_t(cols, hd)):
            ref[g] = val

    def sel_step(k0, g, bias):
        k_op = jnp.concatenate([ksel_ref[pl.ds(k0, kt), kcols[g]], onehot_ref[pl.ds(k0, kt), :]], axis=1)
        s_t = _dot(k_op, q_full[g], _NT)
        s_t = s_t if bias is None else s_t + bias
        m, l, acc = _flash_step_t((m_ref[g], l_ref[g], acc_ref[g]), s_t, vselt_ref[vrows[g], pl.ds(k0, kt)])
        m_ref[g], l_ref[g], acc_ref[g] = m, l, acc

    def sel_tile(j, c):
        for g in groups:
            @pl.when(flags_ref[g, j] > 0)
            def _():
                sel_step(pl.multiple_of(j * kt, kt), g, None)
        return c

    lax.fori_loop(0, jlast, sel_tile, 0)
    o_s = []
    for g in groups:
        sel_step(klast, g, diag_bias)
        o_s.append(finish((m_ref[g], l_ref[g], acc_ref[g])))

    outs = []
    for g in groups:
        for h in range(NSA_GROUP):
            c = 3 * (g * NSA_GROUP + h)
            r = slice(h * qb, (h + 1) * qb)
            outs.append(gates_t[c:c + 1, :] * o_c[g][:, r] + gates_t[c + 1:c + 2, :] * o_s[g][:, r]
                        + gates_t[c + 2:c + 3, :] * o_w[g][:, r])
    o_ref[rows, :] = jnp.concatenate(outs, axis=0).T * sza_ref[rows, :]


def _nsa_prompt(q_aug, kc_aug, vc_t, ksel, vsel_t, kwin, vwin_t, small, sza, batch):
    n = q_aug.shape[0]
    seq = n // batch
    nq = seq // Q_BLOCK
    n_cmp = kc_aug.shape[1]
    n_blk = -(-seq // SEL_BLOCK)
    assert n_blk <= AUG and seq % Q_BLOCK == 0 and seq >= WINDOW + Q_BLOCK
    kt = min(512, seq)
    selmap_t = _sel_map(n_cmp, n_blk, AUG).T
    pos = np.arange(seq)[:, None] // SEL_BLOCK
    onehot = jnp.asarray(pos == np.arange(AUG)[None, :], dtype=BF16)
    vw = NSA_KV_HEADS * HEAD_DIM
    kw = NSA_KV_HEADS * AUG
    nblk = 2 if nq % 2 == 0 else 1
    nq = nq // nblk
    row = lambda w: pl.BlockSpec((nblk * Q_BLOCK, w), lambda b, i: (b * nq + i, 0))
    kspec = pl.BlockSpec((None, seq, kw), lambda b, i: (b, 0, 0))
    vspec = pl.BlockSpec((None, vw, seq), lambda b, i: (b, 0, 0))
    return pl.pallas_call(
        functools.partial(_nsa_prompt_kernel, kt=kt, nblk=nblk),
        grid=(batch, nq),
        in_specs=[row(NSA_HEADS * AUG),
                  pl.BlockSpec((None, n_cmp, kw), lambda b, i: (b, 0, 0)),
                  pl.BlockSpec((None, vw, n_cmp), lambda b, i: (b, 0, 0)),
                  kspec, vspec, kspec, vspec,
                  pl.BlockSpec(onehot.shape, lambda b, i: (0, 0)),
                  row(SMALL_W), row(NSA_WIDTH),
                  pl.BlockSpec(selmap_t.shape, lambda b, i: (0, 0))],
        out_specs=row(NSA_WIDTH),
        out_shape=jax.ShapeDtypeStruct((n, NSA_WIDTH), F32),
        scratch_shapes=[pltpu.SMEM((NSA_KV_HEADS, seq // kt), jnp.int32),
                        pltpu.VMEM((NSA_KV_HEADS, 1, NSA_GROUP * Q_BLOCK), F32),
                        pltpu.VMEM((NSA_KV_HEADS, 1, NSA_GROUP * Q_BLOCK), F32),
                        pltpu.VMEM((NSA_KV_HEADS, HEAD_DIM, NSA_GROUP * Q_BLOCK), F32)],
        compiler_params=_cparams(("parallel", "arbitrary")),
        name="nsa_prompt",
    )(q_aug, kc_aug, vc_t, ksel.reshape(batch, seq, kw), vsel_t, kwin.reshape(batch, seq, kw), vwin_t, onehot, small, sza,
      selmap_t)


def _softplus(x):
    return jnp.maximum(x, 0.0) + jnp.log1p(jnp.exp(-jnp.abs(x)))


def _gdn_prompt_kernel(qkv_ref, small_ref, szb_ref, cw_ref, cb_ref, alog_ref, dtb_ref, gn_ref,
                       o_ref, sfin_ref, s_ref, prev_ref):
    c = pl.program_id(0)
    ck = GDN_CHUNK
    dk = GDN_DK
    nb = qkv_ref.shape[0]

    @pl.when(c == 0)
    def _():
        s_ref[...] = jnp.zeros_like(s_ref)
        prev_ref[...] = jnp.zeros_like(prev_ref)

    ii = lax.broadcasted_iota(jnp.int32, (ck, ck), 0)
    jj = lax.broadcasted_iota(jnp.int32, (ck, ck), 1)
    tri = jnp.where(ii >= jj, 1.0, 0.0).astype(F32)
    acts, betas, gcs, gcts = [], [], [], []
    for b in range(nb):
        u = qkv_ref[b]
        ext = jnp.concatenate([prev_ref[b], u], axis=0)
        y = cb_ref[...]
        for i in range(CONV_W):
            y = y + ext[8 - (CONV_W - 1) + i:8 - (CONV_W - 1) + i + ck] * cw_ref[i:i + 1, :]
        prev_ref[b] = u[ck - 8:ck]
        acts.append(_silu(y))
        small = small_ref[b]
        g_all = -jnp.exp(alog_ref[...]) * _softplus(small + dtb_ref[...])
        betas.append(jax.nn.sigmoid(small))
        gcs.append(lax.dot_general(tri, g_all, _NN, precision=lax.Precision.HIGHEST, preferred_element_type=F32))
        gcts.append(gcs[b].T)

    npair = GDN_HEADS // 2
    pairs = range(nb * npair)
    pw2 = 2 * dk
    lane = lax.broadcasted_iota(jnp.int32, (ck, pw2), 1)
    left = lane < dk
    row2 = lax.broadcasted_iota(jnp.int32, (ck, pw2), 0)
    col2 = jnp.where(left, lane, lane - dk)
    eye2 = jnp.where(row2 == col2, 1.0, 0.0).astype(F32)

    def halves(col_l, col_r):
        return jnp.where(left, col_l, col_r)

    def head_norm(x):
        sq = x * x
        return halves(jnp.sum(jnp.where(left, sq, 0.0), axis=-1, keepdims=True),
                      jnp.sum(jnp.where(left, 0.0, sq), axis=-1, keepdims=True))

    def bd(b):
        return _split(jnp.concatenate([jnp.where(left, b, 0.0), jnp.where(left, 0.0, b)], axis=0))

    q, k, v, beta, gc, decay, g_last = [], [], [], [], [], [], []
    for bp in pairs:
        b, p = divmod(bp, npair)
        act, beta_all, gc_all, gc_t = acts[b], betas[b], gcs[b], gcts[b]
        qp = act[:, p * pw2:(p + 1) * pw2]
        kp = act[:, GDN_QK_WIDTH + p * pw2:GDN_QK_WIDTH + (p + 1) * pw2]
        q.append(qp * lax.rsqrt(head_norm(qp) + EPS) * (dk ** -0.5))
        k.append(kp * lax.rsqrt(head_norm(kp) + EPS))
        v.append(act[:, 2 * GDN_QK_WIDTH + p * pw2:2 * GDN_QK_WIDTH + (p + 1) * pw2])
        h0, h1 = 2 * p, 2 * p + 1
        beta.append(halves(beta_all[:, SMALL_B0 + h0:SMALL_B0 + h0 + 1], beta_all[:, SMALL_B0 + h1:SMALL_B0 + h1 + 1]))
        gcp = halves(gc_all[:, SMALL_A0 + h0:SMALL_A0 + h0 + 1], gc_all[:, SMALL_A0 + h1:SMALL_A0 + h1 + 1])
        grp = jnp.concatenate([gc_t[SMALL_A0 + h0:SMALL_A0 + h0 + 1, :], gc_t[SMALL_A0 + h1:SMALL_A0 + h1 + 1, :]], axis=1)
        gc.append(gcp)
        g_last.append(gcp[ck - 1:ck, :])
        decay.append(jnp.exp(jnp.where(row2 >= col2, gcp - grp, NEG)))
    kb = [k[p] * beta[p] for p in pairs]
    kbd = [bd(k[p]) for p in pairs]
    qk = [_dot3s(_split(jnp.concatenate([q[p], kb[p]], axis=0)), kbd[p], _NT) for p in pairs]
    a_in = [qk[p][0:ck] * decay[p] for p in pairs]
    lmat = [jnp.where(row2 > col2, qk[p][ck:2 * ck] * decay[p], 0.0) for p in pairs]
    x = [eye2 - lmat[p] for p in pairs]
    pw = lmat
    for it in range(int(math.log2(ck)) - 1):
        if it < 3:
            pw = [_dot3s(_split(pw[p]), bd(pw[p])) for p in pairs]
            x = [x[p] + _dot3s(_split(x[p]), bd(pw[p])) for p in pairs]
        else:
            pw = [_dot(pw[p].astype(BF16), bd(pw[p])[0]) for p in pairs]
            x = [x[p] + _dot(x[p].astype(BF16), bd(pw[p])[0]) for p in pairs]
    eg = [jnp.exp(gc[p]) for p in pairs]
    xs = [_split(x[p]) for p in pairs]
    uu = [_dot3s(xs[p], bd(v[p] * beta[p])) for p in pairs]
    ww = [_dot3s(xs[p], bd(kb[p] * eg[p])) for p in pairs]
    s_old = [s_ref[p] for p in pairs]
    qw_s = [_dot3s(_split(jnp.concatenate([q[p] * eg[p], ww[p]], axis=0)), bd(s_old[p])) for p in pairs]
    v_new = [uu[p] - qw_s[p][ck:2 * ck] for p in pairs]
    o = [qw_s[p][0:ck] + _dot(a_in[p].astype(BF16), bd(v_new[p])[0]) for p in pairs]
    gn2 = jnp.concatenate([gn_ref[...]] * 2, axis=1)
    for p in pairs:
        kd = k[p] * jnp.exp(g_last[p] - gc[p])
        r = _dot3s(_split(kd.T), _split(v_new[p]))
        s_ref[p] = s_old[p] * jnp.exp(g_last[p]) + jnp.where(left, r[0:dk], r[dk:pw2])
        on = o[p] * lax.rsqrt(head_norm(o[p]) * (1.0 / GDN_DV) + EPS) * gn2
        b, pp = divmod(p, npair)
        o_ref[b, :, pp * pw2:(pp + 1) * pw2] = on * szb_ref[b, :, pp * pw2:(pp + 1) * pw2]

    @pl.when(c == pl.num_programs(0) - 1)
    def _():
        for p in pairs:
            b, pp = divmod(p, npair)
            sp = s_ref[p]
            sfin_ref[b, 2 * pp] = sp[:, 0:GDN_DV]
            sfin_ref[b, 2 * pp + 1] = sp[:, GDN_DV:2 * GDN_DV]


def _lane_params(a_log, dt_bias):
    alog_l = jnp.zeros((1, SMALL_W), F32).at[0, SMALL_A0:SMALL_A0 + GDN_HEADS].set(a_log)
    dtb_l = jnp.zeros((1, SMALL_W), F32).at[0, SMALL_A0:SMALL_A0 + GDN_HEADS].set(dt_bias)
    return alog_l, dtb_l


def _gdn_prompt(qkvb, small, szb, conv_w, conv_b, a_log, dt_bias, gnorm, batch):
    n = qkvb.shape[0]
    seq = n // batch
    nc = seq // GDN_CHUNK
    alog_l, dtb_l = _lane_params(a_log, dt_bias)
    row = lambda w: pl.BlockSpec((batch, GDN_CHUNK, w), lambda c: (0, c, 0))
    full = lambda a: pl.BlockSpec(a.shape, lambda c: (0,) * a.ndim)
    cb = conv_b.reshape(1, -1)
    gn = gnorm.reshape(1, -1)
    seq3 = lambda a: a.reshape(batch, seq, a.shape[-1])
    o, s_new = pl.pallas_call(
        _gdn_prompt_kernel,
        grid=(nc,),
        in_specs=[row(CONV_DIM), row(SMALL_W), row(GDN_WIDTH), full(conv_w), full(cb), full(alog_l), full(dtb_l), full(gn)],
        out_specs=[row(GDN_WIDTH),
                   pl.BlockSpec((batch, GDN_HEADS, GDN_DK, GDN_DV), lambda c: (0, 0, 0, 0))],
        out_shape=[jax.ShapeDtypeStruct((batch, seq, GDN_WIDTH), F32),
                   jax.ShapeDtypeStruct((batch, GDN_HEADS, GDN_DK, GDN_DV), F32)],
        scratch_shapes=[pltpu.VMEM((batch * (GDN_HEADS // 2), GDN_DK, 2 * GDN_DV), F32),
                        pltpu.VMEM((batch, 8, CONV_DIM), F32)],
        compiler_params=_cparams(("arbitrary",)),
        name="gdn_prompt",
    )(seq3(qkvb), seq3(small), seq3(szb), conv_w, cb, alog_l, dtb_l, gn)
    return o.reshape(n, GDN_WIDTH), s_new


def _prep_w_in(w_in, d_model):
    sizes = (NSA_WIDTH, 6 * NSA_KV_HEADS * HEAD_DIM, 3 * NSA_HEADS, NSA_WIDTH, CONV_DIM, GDN_HEADS, GDN_HEADS,
             GDN_WIDTH, 2 * d_model)
    pts = np.cumsum(np.array(sizes))[:-1].tolist()
    q_a, kv_a, g_a, z_a, qkv_b, a_b, b_b, z_b, gm = jnp.split(w_in, pts, axis=1)
    small = jnp.concatenate([g_a, a_b, b_b], axis=1)
    small = jnp.pad(small, ((0, 0), (0, SMALL_W - small.shape[1])))
    return jnp.concatenate([q_a, kv_a, z_a, qkv_b, z_b, gm, small], axis=1).astype(BF16)


def _kv_leaf(kv_t):
    b, _, t = kv_t.shape
    return kv_t.reshape(1, b, 2, NSA_KV_HEADS, HEAD_DIM, t).transpose(0, 1, 5, 2, 3, 4)


def _kv_rows_t(cache):
    n, r = cache.shape[:2]
    return cache.transpose(0, 2, 3, 4, 1).reshape(n, KV_ROW, r)


def _prompt_path(x, ada, lw, cw):
    (norm_g, w_bf, offs, conv_w, conv_b, a_log, dt_bias, gnorm, wa, wb, wo, final_g) = lw
    batch, seq, d = x.shape
    n = batch * seq
    x2 = x.reshape(n, d)
    ada3 = ada.reshape(batch, 1, 3 * d)
    tm = 256
    mod = lambda k: pl.BlockSpec((None, 1, d), lambda i: (i * tm // seq, 0, k))
    (q_aug, kvc_t, kvs_t, kvw_t, kvcb, ksel, vsel_t, kwin, vwin_t, sza, qkvb, szb, gms, small) = _inproj(
        x2, ada3, ada3, (mod(1), mod(0)), norm_g.reshape(1, d), w_bf, offs, tm, seq)
    kc_aug, vc_t = _compress_prompt(kvcb, *cw, batch)
    o_a = _nsa_prompt(q_aug, kc_aug, vc_t, ksel, vsel_t, kwin, vwin_t, small, sza, batch)
    o_b, s_new = _gdn_prompt(qkvb, small, szb, conv_w, conv_b, a_log, dt_bias, gnorm, batch)
    tmo = 512
    gate_spec = pl.BlockSpec((None, 1, d), lambda i: (i * tmo // seq, 0, 2))
    y = _outproj(x2, o_a, o_b, gms, ada3, gate_spec, wa, wb, wo, final_g.reshape(1, d), tmo)
    keep = min(WINDOW, seq)
    new_conv = qkvb.reshape(batch, seq, CONV_DIM)[None, :, seq - (CONV_W - 1):]
    return (y.reshape(batch, seq, d), _kv_leaf(kvc_t), _kv_leaf(kvs_t), _kv_leaf(kvw_t[:, :, seq - keep:]), new_conv,
            s_new[None])


def _page_fetch(pt_ref, cache_hbm, buf, sem, npages):
    def copy(seq, slot, j):
        c0 = pl.multiple_of(j * PAGE_SIZE, PAGE_SIZE)
        return pltpu.make_async_copy(cache_hbm.at[pt_ref[seq, j]], buf.at[slot, :, pl.ds(c0, PAGE_SIZE)], sem.at[slot])

    def start(seq, slot):
        for j in range(npages):
            copy(seq, slot, j).start()

    def wait(seq, slot):
        for j in range(npages):
            copy(seq, slot, j).wait()

    return start, wait


def _fetch_this_prefetch_next(start, wait):
    b = pl.program_id(0)
    nb = pl.num_programs(0)

    @pl.when(b == 0)
    def _():
        start(0, 0)

    @pl.when(b + 1 < nb)
    def _():
        start(b + 1, (b + 1) % 2)

    slot = b % 2
    wait(b, slot)
    return slot


def _compress_sample_kernel(pt_ref, cache_hbm, perm_ref, w1_ref, pe_ref, w2_ref, o_ref, buf, sem, xc_ref, *, npages):
    start, wait = _page_fetch(pt_ref, cache_hbm, buf, sem, npages)
    slot = _fetch_this_prefetch_next(start, wait)
    past = npages * PAGE_SIZE
    n16 = past // CMP_STRIDE
    gt = perm_ref.shape[0]
    cps = gt // CMP_STRIDE

    half = KV_ROW // 2
    kw = CMP_STRIDE * half

    def regroup(j, c):
        t0 = pl.multiple_of(j * gt, gt)
        x_t = buf[slot, :, pl.ds(t0, gt)].astype(BF16)
        y = _dot(perm_ref[...], x_t, _NT).astype(BF16)
        r0 = pl.multiple_of(j * cps, cps)
        for s in range(CMP_STRIDE):
            for kv in range(2):
                xc_ref[pl.ds(r0, cps), kv * kw + s * half:kv * kw + (s + 1) * half] = (
                    y[s * cps:(s + 1) * cps, kv * half:(kv + 1) * half])
        return c

    lax.fori_loop(0, past // gt, regroup, 0, unroll=min(8, past // gt))
    outs = []
    for kv in range(2):
        a = _dot(xc_ref[:, kv * kw:(kv + 1) * kw], w1_ref[kv])
        p0 = _dot(pe_ref[kv, 0], w1_ref[kv, :, 0:half]) + _dot(pe_ref[kv, 1], w1_ref[kv, :, half:2 * half])
        pre = a[:, 0:half] + pltpu.roll(a[:, half:2 * half], n16 - 1, 0) + p0[0:1]
        outs.append(_dot(_silu(pre).astype(BF16), w2_ref[kv]))
    o_ref[...] = jnp.concatenate(outs, axis=1).astype(BF16)


def _compress_weights_kv(pe_k, w1_k, w2_k, pe_v, w1_v, w2_v):
    r_cnt = CMP_BLOCK // CMP_STRIDE
    g = NSA_KV_HEADS
    eyeg = jnp.eye(g, dtype=F32)
    w1 = jnp.stack([w1_k, w1_v]).reshape(2, r_cnt, CMP_STRIDE, HEAD_DIM, CMP_HIDDEN)
    w1s = jnp.einsum("krsde,gG->ksgdrGe", w1, eyeg)
    w1s = w1s.reshape(2, CMP_STRIDE * g * HEAD_DIM, r_cnt * g * CMP_HIDDEN).astype(BF16)
    pe = jnp.stack([pe_k, pe_v]).reshape(2, r_cnt, CMP_STRIDE, 1, HEAD_DIM)
    pes = jnp.broadcast_to(pe, (2, r_cnt, CMP_STRIDE, g, HEAD_DIM)).reshape(2, r_cnt, 1, CMP_STRIDE * g * HEAD_DIM)
    pes = jnp.pad(pes, ((0, 0), (0, 0), (0, 7), (0, 0))).astype(BF16)
    w2s = jnp.einsum("ked,gG->kgeGd", jnp.stack([w2_k, w2_v]), eyeg).reshape(2, g * CMP_HIDDEN, g * HEAD_DIM).astype(BF16)
    return w1s, pes, w2s


def _compress_sample(cache_t, page_table, w1big, pebig, w2big):
    nseq, npages = page_table.shape
    past = npages * PAGE_SIZE
    n16 = past // CMP_STRIDE
    gt = 2 * PAGE_SIZE
    assert past % gt == 0 and gt // CMP_STRIDE == 16
    row = np.arange(gt)
    perm_t = jnp.asarray((row[:, None] % (gt // CMP_STRIDE)) * CMP_STRIDE + row[:, None] // (gt // CMP_STRIDE)
                         == row[None, :], dtype=BF16)
    grid_spec = pltpu.PrefetchScalarGridSpec(
        num_scalar_prefetch=1,
        grid=(nseq,),
        in_specs=[pl.BlockSpec(memory_space=pl.ANY),
                  pl.BlockSpec(perm_t.shape, lambda b, pt: (0, 0)),
                  pl.BlockSpec(w1big.shape, lambda b, pt: (0, 0, 0)),
                  pl.BlockSpec(pebig.shape, lambda b, pt: (0, 0, 0, 0)),
                  pl.BlockSpec(w2big.shape, lambda b, pt: (0, 0, 0))],
        out_specs=pl.BlockSpec((None, n16, KV_ROW), lambda b, pt: (b, 0, 0)),
        scratch_shapes=[pltpu.VMEM((2, KV_ROW, past), F32), pltpu.SemaphoreType.DMA((2,)),
                        pltpu.VMEM((n16, CMP_STRIDE * KV_ROW), BF16)],
    )
    return pl.pallas_call(
        functools.partial(_compress_sample_kernel, npages=npages),
        grid_spec=grid_spec,
        out_shape=jax.ShapeDtypeStruct((nseq, n16, KV_ROW), BF16),
        compiler_params=_cparams(("arbitrary",)),
        name="compress_sample",
    )(page_table, cache_t, perm_t, w1big, pebig, w2big)


def _nsa_sample_kernel(pt_ref, q_ref, kc_ref, cache_hbm, win_ref, nsel_ref, nwin_ref, nwint_ref, small_ref, sza_ref,
                       selmap_ref, onehot_ref, o_ref, wout_ref, buf, sem, *, npages, tn, kt):
    hd = HEAD_DIM
    rt = 8
    start, wait = _page_fetch(pt_ref, cache_hbm, buf, sem, npages)
    slot = _fetch_this_prefetch_next(start, wait)
    past = npages * PAGE_SIZE
    nbuf = win_ref.shape[1]
    n_cmp = kc_ref.shape[0]
    n_blk_lanes = selmap_ref.shape[1]
    q8 = q_ref[...].astype(F32)
    gates = jax.nn.sigmoid(small_ref[...])
    tau = lax.broadcasted_iota(jnp.int32, (rt, 1), 0) % tn
    t_col = past + tau
    t4 = jnp.concatenate([t_col] * NSA_GROUP, axis=0)
    rows = NSA_GROUP * rt
    newcol = lax.broadcasted_iota(jnp.int32, (1, SMALL_W), 1)
    new_dist = t4 - (past + newcol)
    new_ok = (newcol < tn) & (new_dist >= 0)
    zpad = jnp.zeros((SMALL_W - rt, hd), F32)
    groups = range(NSA_KV_HEADS)
    kcols = [slice(g * hd, (g + 1) * hd) for g in groups]
    vcols = [slice((NSA_KV_HEADS + g) * hd, (NSA_KV_HEADS + g + 1) * hd) for g in groups]

    def new_tile(ref, col):
        return jnp.concatenate([ref[:, col], zpad], axis=0).astype(BF16)

    def pos_cols(p0, n):
        pos = (p0 + lax.broadcasted_iota(jnp.int32, (n, 1), 0)).astype(F32)
        hi = jnp.floor(pos * (1.0 / POS_SPLIT))
        return _aug_cols(n, hi, pos - hi * POS_SPLIT)

    qgs, slopes, o_cs, imps = [], [], [], []
    for g in groups:
        kcol, vcol = kcols[g], vcols[g]
        qg = jnp.concatenate([q8[:, (g * NSA_GROUP + h) * AUG:(g * NSA_GROUP + h) * AUG + hd]
                              for h in range(NSA_GROUP)], axis=0).astype(BF16)
        slope = jnp.concatenate([jnp.full((rt, 1), _SLOPES[g * NSA_GROUP + h], F32) for h in range(NSA_GROUP)], axis=0)

        s = _dot(qg, kc_ref[:, kcol], _NT)
        end = lax.broadcasted_iota(jnp.int32, (1, n_cmp), 1) * CMP_STRIDE + (CMP_BLOCK - 1)
        dist = t4 - end
        mask = dist >= 0
        _, p = _softmax_block(s - slope * dist.astype(F32), mask)
        p = p / jnp.maximum(jnp.sum(p, axis=-1, keepdims=True), 1e-30)
        o_c = _dot(p.astype(BF16), kc_ref[:, vcol])
        psum = p[0:rt]
        for h in range(1, NSA_GROUP):
            psum = psum + p[h * rt:(h + 1) * rt]
        for lst, val in ((qgs, qg), (slopes, slope), (o_cs, o_c), (imps, _importance(psum, selmap_ref[...]))):
            lst.append(val)

    ncol = NSA_KV_HEADS * rt
    imp_pad = jnp.concatenate(imps + [jnp.zeros((SMALL_W - ncol, n_blk_lanes), F32)], axis=0)
    tq_row = past + lax.broadcasted_iota(jnp.int32, (1, SMALL_W), 1) % tn
    selm_all = _select_blocks_t(imp_pad.T, tq_row).T

    selms, q_augs, q_fulls = [], [], []
    for g in groups:
        selm = selm_all[g * rt:(g + 1) * rt]
        q_aug = jnp.concatenate([q8[:, (g * NSA_GROUP + h) * AUG:(g * NSA_GROUP + h + 1) * AUG]
                                 for h in range(NSA_GROUP)], axis=0)
        blk_bias = jnp.where(selm > 0.5, 0.0, MASKV)
        q_full = jnp.concatenate([q_aug, jnp.concatenate([blk_bias] * NSA_GROUP, axis=0)], axis=1).astype(BF16)
        for lst, val in ((selms, selm), (q_augs, q_aug), (q_fulls, q_full)):
            lst.append(val)

    bpt = kt // SEL_BLOCK
    carries = [_online_init(rows, hd) for _ in groups]
    for j in range(past // kt):
        k0 = j * kt
        for g in groups:
            k_t = buf[slot, kcols[g], k0:k0 + kt].astype(BF16)
            v_t = buf[slot, vcols[g], k0:k0 + kt].astype(BF16)
            q_rest = jnp.concatenate([q_fulls[g][:, hd:AUG], q_fulls[g][:, AUG + j * bpt:AUG + (j + 1) * bpt]], axis=1)
            s = _dot(q_fulls[g][:, 0:hd], k_t) + _dot(q_rest, onehot_ref[j])
            carries[g] = _flash_step(carries[g], s, v_t, _NT)

    outs = []
    for g in groups:
        kcol, vcol = kcols[g], vcols[g]
        qg, slope, o_c, selm, q_aug = qgs[g], slopes[g], o_cs[g], selms[g], q_augs[g]
        nb_new = past // SEL_BLOCK
        new_sel = jnp.concatenate([selm[:, nb_new:nb_new + 1]] * NSA_GROUP, axis=0) > 0.5
        kn = jnp.concatenate([jnp.concatenate([nsel_ref[:, kcol], zpad], axis=0), pos_cols(past, SMALL_W)], axis=1)
        s = _dot(q_aug.astype(BF16), kn.astype(BF16), _NT) + jnp.where(new_ok & new_sel, 0.0, MASKV)
        _, l, acc = _flash_step(carries[g], s, new_tile(nsel_ref, vcol))
        o_s = acc / jnp.maximum(l, 1e-30)

        s = _dot(qg, win_ref[kcol, :].astype(BF16))
        dist = t4 - (past - nbuf + lax.broadcasted_iota(jnp.int32, (1, nbuf), 1))
        mask = (dist >= 0) & (dist < WINDOW)
        carry = _online_update(_online_init(rows, hd), s - slope * dist.astype(F32), mask,
                               win_ref[vcol, :].astype(BF16), _NT)
        s = _dot(qg, new_tile(nwin_ref, kcol), _NT)
        _, l, acc = _online_update(carry, s - slope * new_dist.astype(F32), new_ok, new_tile(nwin_ref, vcol))
        o_w = acc / jnp.maximum(l, 1e-30)

        for h in range(NSA_GROUP):
            c = 3 * (g * NSA_GROUP + h)
            r = slice(h * rt, (h + 1) * rt)
            outs.append(gates[:, c:c + 1] * o_c[r] + gates[:, c + 1:c + 2] * o_s[r] + gates[:, c + 2:c + 3] * o_w[r])
    o_ref[...] = jnp.concatenate(outs, axis=1) * sza_ref[...]
    wout_ref[...] = jnp.concatenate([win_ref[:, tn:nbuf], nwint_ref[...]], axis=1)


def _nsa_sample(qs, kc, cache_t, win_t, kvs, kvw, small, sza, page_table, tn):
    nseq, npages = page_table.shape
    past = npages * PAGE_SIZE
    nbuf = win_t.shape[2]
    assert nbuf == WINDOW and 8 % tn == 0 and tn <= SEL_BLOCK
    n_cmp = kc.shape[1]
    n_blk = -(-(past + tn) // SEL_BLOCK)
    n_blk_lanes = -(-n_blk // 128) * 128
    selmap = _sel_map(n_cmp, n_blk, n_blk_lanes)
    kt = min(4096, past)
    ntile, bpt = past // kt, kt // SEL_BLOCK
    key = np.arange(past).reshape(ntile, 1, kt)
    pos_rows = np.zeros((ntile, AUG - HEAD_DIM, kt), np.float32)
    pos_rows[:, 0], pos_rows[:, 1] = key[:, 0] // POS_SPLIT, key[:, 0] % POS_SPLIT
    member = (key % kt // SEL_BLOCK == np.arange(bpt)[None, :, None]).astype(np.float32)
    onehot_t = jnp.asarray(np.concatenate([pos_rows, member], axis=1), dtype=BF16)
    seq3 = lambda r, w: pl.BlockSpec((None, r, w), lambda b, pt: (b, 0, 0))
    grid_spec = pltpu.PrefetchScalarGridSpec(
        num_scalar_prefetch=1,
        grid=(nseq,),
        in_specs=[seq3(8, NSA_HEADS * AUG), seq3(n_cmp, KV_ROW), pl.BlockSpec(memory_space=pl.ANY), seq3(KV_ROW, nbuf),
                  seq3(8, KV_ROW), seq3(8, KV_ROW), seq3(KV_ROW, tn), seq3(8, SMALL_W), seq3(8, NSA_WIDTH),
                  pl.BlockSpec(selmap.shape, lambda b, pt: (0, 0)),
                  pl.BlockSpec(onehot_t.shape, lambda b, pt: (0, 0, 0))],
        out_specs=[seq3(8, NSA_WIDTH), seq3(KV_ROW, nbuf)],
        scratch_shapes=[pltpu.VMEM((2, KV_ROW, past), F32), pltpu.SemaphoreType.DMA((2,))],
    )
    rep8 = lambda a: jnp.concatenate([a.reshape(nseq, tn, a.shape[-1])] * (8 // tn), axis=1)
    o8, win_new_t = pl.pallas_call(
        functools.partial(_nsa_sample_kernel, npages=npages, tn=tn, kt=kt),
        grid_spec=grid_spec,
        out_shape=[jax.ShapeDtypeStruct((nseq, 8, NSA_WIDTH), F32), jax.ShapeDtypeStruct((nseq, KV_ROW, nbuf), F32)],
        compiler_params=_cparams(("arbitrary",)),
        name="nsa_sample",
    )(page_table, rep8(qs), kc, cache_t, win_t, rep8(kvs), rep8(kvw), kvw.transpose(0, 2, 1), rep8(small), rep8(sza),
      selmap, onehot_t)
    return o8[:, :tn].reshape(nseq * tn, NSA_WIDTH), win_new_t


def _gdn_sample_kernel(eq_ref, ek_ref, ev_ref, cwq_ref, cwk_ref, cwv_ref, cbq_ref, cbk_ref, cbv_ref, small_ref,
                       alog_ref, dtb_ref, szb_ref, gn_ref, s_ref, o_ref, so_ref, qs_ref, ks_ref):
    h = pl.program_id(0)
    tn = o_ref.shape[0]
    dk, dv, nseq = so_ref.shape

    def conv(e_ref, cw_ref, cb_ref, t):
        y = cb_ref[...]
        for i in range(CONV_W):
            y = y + e_ref[t + i] * cw_ref[:, i:i + 1]
        return _silu(y)

    so_ref[...] = s_ref[...]
    neg_rate = -jnp.exp(alog_ref[...])
    for t in range(tn):
        q = conv(eq_ref, cwq_ref, cbq_ref, t)
        k = conv(ek_ref, cwk_ref, cbk_ref, t)
        v = conv(ev_ref, cwv_ref, cbv_ref, t)
        qs_ref[...] = q * lax.rsqrt(jnp.sum(q * q, axis=0, keepdims=True) + EPS) * (dk ** -0.5)
        ks_ref[...] = k * lax.rsqrt(jnp.sum(k * k, axis=0, keepdims=True) + EPS)
        a_in = small_ref[t, pl.ds(SMALL_A0 + h, 1), :]
        b_in = small_ref[t, pl.ds(SMALL_B0 + h, 1), :]
        decay = jnp.exp(neg_rate * _softplus(a_in + dtb_ref[...]))
        beta = jax.nn.sigmoid(b_in)

        def ks_step(i, acc):
            return acc + ks_ref[pl.ds(i, 1), :] * so_ref[i]

        k_s = lax.fori_loop(0, dk, ks_step, jnp.zeros((dv, nseq), F32), unroll=8)
        delta = beta * (v - decay * k_s)

        def upd_step(i, acc):
            s_new = decay * so_ref[i] + ks_ref[pl.ds(i, 1), :] * delta
            so_ref[i] = s_new
            return acc + qs_ref[pl.ds(i, 1), :] * s_new

        o = lax.fori_loop(0, dk, upd_step, jnp.zeros((dv, nseq), F32), unroll=8)
        o = o * lax.rsqrt(jnp.mean(o * o, axis=0, keepdims=True) + EPS) * gn_ref[...]
        o_ref[t] = o * szb_ref[t]


def _gdn_sample(qkvb, small, szb, state_conv, state_gdn, conv_w, conv_b, a_log, dt_bias, gnorm, tn):
    nseq = state_gdn.shape[0]
    ext = jnp.concatenate([state_conv, qkvb.reshape(nseq, tn, CONV_DIM)], axis=1)
    ext_t = ext.transpose(1, 2, 0)
    small_t = small.reshape(nseq, tn, SMALL_W).transpose(1, 2, 0)
    szb_t = szb.reshape(nseq, tn, GDN_WIDTH).transpose(1, 2, 0)
    s_t = state_gdn.transpose(1, 2, 3, 0)
    cw_t = conv_w.T
    cb_t = conv_b.reshape(-1, 1)
    alog_b = jnp.broadcast_to(a_log[:, None, None], (GDN_HEADS, 1, nseq))
    dtb_b = jnp.broadcast_to(dt_bias[:, None, None], (GDN_HEADS, 1, nseq))
    gn = gnorm.reshape(-1, 1)
    nqk = GDN_QK_WIDTH // GDN_DK
    chan = lambda off: pl.BlockSpec((CONV_W - 1 + tn, GDN_DK, nseq), lambda h: (0, off + h, 0))
    cwb = lambda off: pl.BlockSpec((GDN_DK, CONV_W), lambda h: (off + h, 0))
    cbb = lambda off: pl.BlockSpec((GDN_DK, 1), lambda h: (off + h, 0))
    perhead = pl.BlockSpec((None, 1, nseq), lambda h: (h, 0, 0))
    o_t, s_new = pl.pallas_call(
        _gdn_sample_kernel,
        grid=(GDN_HEADS,),
        in_specs=[chan(0), chan(nqk), chan(2 * nqk), cwb(0), cwb(nqk), cwb(2 * nqk), cbb(0), cbb(nqk), cbb(2 * nqk),
                  pl.BlockSpec((tn, SMALL_W, nseq), lambda h: (0, 0, 0)), perhead, perhead,
                  pl.BlockSpec((tn, GDN_DV, nseq), lambda h: (0, h, 0)),
                  pl.BlockSpec((GDN_DV, 1), lambda h: (0, 0)),
                  pl.BlockSpec((None, GDN_DK, GDN_DV, nseq), lambda h: (h, 0, 0, 0))],
        out_specs=[pl.BlockSpec((tn, GDN_DV, nseq), lambda h: (0, h, 0)),
                   pl.BlockSpec((None, GDN_DK, GDN_DV, nseq), lambda h: (h, 0, 0, 0))],
        out_shape=[jax.ShapeDtypeStruct((tn, GDN_WIDTH, nseq), F32),
                   jax.ShapeDtypeStruct((GDN_HEADS, GDN_DK, GDN_DV, nseq), F32)],
        scratch_shapes=[pltpu.VMEM((GDN_DK, nseq), F32), pltpu.VMEM((GDN_DK, nseq), F32)],
        compiler_params=_cparams(("parallel",)),
        name="gdn_sample",
    )(ext_t, ext_t, ext_t, cw_t, cw_t, cw_t, cb_t, cb_t, cb_t, small_t, alog_b, dtb_b, szb_t, gn, s_t)
    o_b = o_t.transpose(2, 0, 1).reshape(nseq * tn, GDN_WIDTH)
    return o_b, ext[:, tn:], s_new.transpose(3, 0, 1, 2)


def _sample_path(x, ada, lw, cw, cache_cmp, cache_sel, cache_win, state_conv, state_gdn, page_table):
    (norm_g, w_bf, offs, conv_w, conv_b, a_log, dt_bias, gnorm, wa, wb, wo, final_g) = lw
    nseq, tn, d = x.shape
    n = nseq * tn
    x2 = x.reshape(n, d)
    ada_rows = jnp.repeat(ada, tn, axis=0)
    tm = min(256, n)
    mod = lambda k: pl.BlockSpec((tm, d), lambda i: (i, k))
    (qs, kvc_t, kvs_t, kvw_t, _, _, _, _, _, sza, qkvb, szb, gms, small) = _inproj(
        x2, ada_rows, ada_rows, (mod(1), mod(0)), norm_g.reshape(1, d), w_bf, offs, tm, n)
    rows = lambda a_t: a_t.reshape(KV_ROW, nseq, tn).transpose(1, 2, 0)
    kvc, kvs, kvw = rows(kvc_t), rows(kvs_t), rows(kvw_t)
    kc = _compress_sample(_kv_rows_t(cache_cmp), page_table, *cw)
    o_a, win_new_t = _nsa_sample(qs, kc, _kv_rows_t(cache_sel), _kv_rows_t(cache_win), kvs, kvw, small, sza, page_table, tn)
    o_b, conv_new, s_new = _gdn_sample(qkvb, small, szb, state_conv, state_gdn, conv_w, conv_b, a_log, dt_bias, gnorm, tn)
    y = _outproj(x2, o_a, o_b, gms, ada_rows, mod(2), wa, wb, wo, final_g.reshape(1, d), tm)
    kvshape = (1, nseq, tn, 2, NSA_KV_HEADS, HEAD_DIM)
    return (y.reshape(nseq, tn, d), kvc.reshape(kvshape), kvs.reshape(kvshape), _kv_leaf(win_new_t),
            conv_new[None], s_new[None])


def kernel(x_prompt, x_sample, cache_cmp_kv, cache_sel_kv, cache_win_kv, state_conv, state_gdn, page_table, c_prompt, c_sample, norm_g, w_ada, b_ada, w_in, cmp_pe_k, cmp_w1_k, cmp_w2_k, cmp_pe_v, cmp_w1_v, cmp_w2_v, conv_w, conv_b, gdn_a_log, gdn_dt_bias, gdn_norm_g, w_o_nsa, w_o_gdn, w_out, final_g):
    assert norm_g.shape[0] == 1, "single trunk layer"
    d = x_prompt.shape[-1]
    l = 0
    offs, _ = _seg_offsets(d)
    w_bf = _prep_w_in(w_in[l], d)
    ada = _ada(jnp.concatenate([c_prompt, c_sample], axis=0), w_ada[l].astype(BF16), b_ada[l])
    cw = _compress_weights(cmp_pe_k[l], cmp_w1_k[l], cmp_w2_k[l], cmp_pe_v[l], cmp_w1_v[l], cmp_w2_v[l])
    lw = (norm_g[l], w_bf, offs, conv_w[l], conv_b[l], gdn_a_log[l], gdn_dt_bias[l], gdn_norm_g[l],
          w_o_nsa[l].astype(BF16), w_o_gdn[l].astype(BF16), w_out[l].astype(BF16), final_g)
    nb = c_prompt.shape[0]
    yp, cmp_p, sel_p, win_p, conv_p, gdn_p = _prompt_path(x_prompt, ada[:nb], lw, cw)
    ys, cmp_s, sel_s, win_s, conv_s, gdn_s = _sample_path(
        x_sample, ada[nb:], lw,
        _compress_weights_kv(cmp_pe_k[l], cmp_w1_k[l], cmp_w2_k[l], cmp_pe_v[l], cmp_w1_v[l], cmp_w2_v[l]),
        cache_cmp_kv[l], cache_sel_kv[l], cache_win_kv[l], state_conv[l], state_gdn[l],
        page_table)
    return (yp, ys, cmp_p, sel_p, win_p, conv_p, gdn_p, cmp_s, sel_s, win_s, conv_s, gdn_s)
```
